```python
import jax, jax.numpy as jnp
from jax import lax
import numpy as np

D_MODEL = 1024
BATCH = 4
SEQ = 4096
DEPTH = 1
DEC_BATCH = 32
DEC_SEQ = 8
PAST_LEN = 16384
PAGE_SIZE = 128

HEAD_DIM = 64
N_HEADS_A = 8
N_HEADS_B = 8
D_A = N_HEADS_A * HEAD_DIM
D_B = N_HEADS_B * HEAD_DIM
DECAY_LORA = 64
AAA_LORA = 64
GATE_LORA = 128
D_FF = 2816
Q_BLOCK = 128
NORM_EPS = 1e-6
LNX_EPS = 64e-5
N_SHIFT = 3 * D_A + DECAY_LORA + AAA_LORA + GATE_LORA
N_IN = N_SHIFT + 3 * D_B + N_HEADS_B + 2 * D_MODEL

kernel_name = "rwkv7_fox_gated_hybrid_step"


def rms_norm(x, g, eps=NORM_EPS):
    xf = x.astype(jnp.float32)
    y = xf * lax.rsqrt(jnp.mean(xf * xf, axis=-1, keepdims=True) + eps)
    return (y * g.astype(jnp.float32)).astype(x.dtype)


def swiglu_ffn(x, g, wg, wu, wd):
    h = rms_norm(x, g)
    return (jax.nn.silu(h @ wg) * (h @ wu)) @ wd


def rwkv7_branch(p_a, prev, S0, mu, w0, w2, a0, a2, g2, k_k, k_a, r_k, lnx_w, lnx_b):
    B, T, _ = p_a.shape
    H, Dh = N_HEADS_A, HEAD_DIM
    f32 = jnp.float32
    p_prev = jnp.concatenate([prev[:, None, :].astype(p_a.dtype), p_a[:, :-1]], axis=1)
    xm = p_a + (p_prev - p_a) * mu
    o1, o2, o3 = D_A, 2 * D_A, 3 * D_A
    o4 = o3 + DECAY_LORA
    o5 = o4 + AAA_LORA
    r, k, v, xw, xa, xg = jnp.split(xm, [o1, o2, o3, o4, o5], axis=-1)
    w_log = -jax.nn.softplus(-(w0 + jnp.tanh(xw) @ w2).astype(f32)) - 0.5
    decay = jnp.exp(-jnp.exp(w_log))
    a = jax.nn.sigmoid((a0 + xa @ a2).astype(f32))
    g = jax.nn.sigmoid(xg) @ g2
    heads = lambda t: t.reshape(B, T, H, Dh).astype(f32)
    r, k, v, decay, a = heads(r), heads(k), heads(v), heads(decay), heads(a)
    kk = k * k_k.reshape(H, Dh).astype(f32)
    kk = kk / jnp.maximum(jnp.sqrt(jnp.sum(kk * kk, axis=-1, keepdims=True)), 1e-12)
    k = k * (1.0 + (a - 1.0) * k_a.reshape(H, Dh).astype(f32))

    def step(S, inp):
        r_t, w_t, k_t, v_t, kk_t, a_t = inp
        sa = jnp.einsum('bhvk,bhk->bhv', S, -kk_t)
        S = (S * w_t[:, :, None, :] + sa[..., None] * (kk_t * a_t)[:, :, None, :]
             + v_t[..., None] * k_t[:, :, None, :])
        return S, jnp.einsum('bhvk,bhk->bhv', S, r_t)

    tm = lambda t: jnp.moveaxis(t, 1, 0)
    S_fin, y = lax.scan(step, S0.astype(f32), (tm(r), tm(decay), tm(k), tm(v), tm(kk), tm(a)))
    y = jnp.moveaxis(y, 0, 1)
    mean = jnp.mean(y, axis=-1, keepdims=True)
    var = jnp.mean(jnp.square(y - mean), axis=-1, keepdims=True)
    y = ((y - mean) * lax.rsqrt(var + LNX_EPS) * lnx_w.reshape(H, Dh).astype(f32)
         + lnx_b.reshape(H, Dh).astype(f32))
    y = y + jnp.sum(r * k * r_k.astype(f32), axis=-1, keepdims=True) * v
    y = y.reshape(B, T, D_A).astype(p_a.dtype) * g
    return y, p_a[:, -1], S_fin.astype(S0.dtype)


def fox_prompt(q, k, v, logf):
    B, S, H, Dh = q.shape
    scale = 1.0 / float(np.sqrt(HEAD_DIM))
    C = jnp.cumsum(logf.astype(jnp.float32), axis=1).transpose(0, 2, 1)
    kpos = jnp.arange(S)

    def block(i):
        start = i * Q_BLOCK
        qb = lax.dynamic_slice_in_dim(q, start, Q_BLOCK, axis=1)
        cb = lax.dynamic_slice_in_dim(C, start, Q_BLOCK, axis=2)
        s = (jnp.einsum('bqhd,bkhd->bhqk', qb, k).astype(jnp.float32) * scale
             + cb[..., None] - C[:, :, None, :])
        qpos = start + jnp.arange(Q_BLOCK)
        s = jnp.where(qpos[:, None] >= kpos[None, :], s, -jnp.inf)
        p = jax.nn.softmax(s, axis=-1).astype(v.dtype)
        return jnp.einsum('bhqk,bkhd->bqhd', p, v)

    o = lax.map(block, jnp.arange(S // Q_BLOCK))
    return o.transpose(1, 0, 2, 3, 4).reshape(B, S, H * Dh)


def fox_sample(q, k, v, logf, pk, pv, plogf):
    B, T, H, Dh = q.shape
    f32 = jnp.float32
    scale = 1.0 / float(np.sqrt(HEAD_DIM))
    Cn = jnp.cumsum(logf.astype(f32), axis=1).transpose(0, 2, 1)
    plf = plogf.astype(f32)
    suffix = (lax.cumsum(plf, axis=1, reverse=True) - plf).transpose(0, 2, 1)
    s_past = (jnp.einsum('bqhd,bkhd->bhqk', q, pk).astype(f32) * scale
              + Cn[..., None] + suffix[:, :, None, :])
    s_new = (jnp.einsum('bqhd,bkhd->bhqk', q, k).astype(f32) * scale
             + Cn[..., None] - Cn[:, :, None, :])
    tpos = jnp.arange(T)
    s_new = jnp.where(tpos[:, None] >= tpos[None, :], s_new, -jnp.inf)
    P = pk.shape[1]
    p = jax.nn.softmax(jnp.concatenate([s_past, s_new], axis=-1), axis=-1).astype(v.dtype)
    o = jnp.einsum('bhqk,bkhd->bqhd', p[..., :P], pv) + jnp.einsum('bhqk,bkhd->bqhd', p[..., P:], v)
    return o.reshape(B, T, H * Dh)


def layer(x, prev_shift, S0, past, lw):
    (ffn1_norm, ffn1_wg, ffn1_wu, ffn1_wd, mix_norm, w_in, shift_mu,
     rwkv_w0, rwkv_w2, rwkv_a0, rwkv_a2, rwkv_g2, rwkv_k_k, rwkv_k_a, rwkv_r_k, rwkv_lnx_w, rwkv_lnx_b,
     fox_b_f, fox_q_norm, fox_k_norm, w_o_a, w_o_b, w_out,
     ffn2_norm, ffn2_wg, ffn2_wu, ffn2_wd) = lw
    B, T, _ = x.shape
    x = x + 0.5 * swiglu_ffn(x, ffn1_norm, ffn1_wg, ffn1_wu, ffn1_wd)
    h = rms_norm(x, mix_norm)
    p = h @ w_in
    c1 = N_SHIFT
    c2 = c1 + D_B
    c3 = c2 + D_B
    c4 = c3 + D_B
    c5 = c4 + N_HEADS_B
    c6 = c5 + D_MODEL
    p_a, q, k, v, f_logit, g_a, g_b = jnp.split(p, [c1, c2, c3, c4, c5, c6], axis=-1)
    y_a, new_shift, new_S = rwkv7_branch(p_a, prev_shift, S0, shift_mu, rwkv_w0, rwkv_w2, rwkv_a0, rwkv_a2,
                                         rwkv_g2, rwkv_k_k, rwkv_k_a, rwkv_r_k, rwkv_lnx_w, rwkv_lnx_b)
    q = rms_norm(q.reshape(B, T, N_HEADS_B, HEAD_DIM), fox_q_norm)
    k = rms_norm(k.reshape(B, T, N_HEADS_B, HEAD_DIM), fox_k_norm)
    v = v.reshape(B, T, N_HEADS_B, HEAD_DIM)
    logf = jax.nn.log_sigmoid(f_logit.astype(jnp.float32) + fox_b_f.astype(jnp.float32))
    if past is None:
        y_b = fox_prompt(q, k, v, logf)
    else:
        pk, pv, plogf = past
        y_b = fox_sample(q, k, v, logf, pk, pv, plogf)
    merged = jax.nn.sigmoid(g_a) * (y_a @ w_o_a) + jax.nn.sigmoid(g_b) * (y_b @ w_o_b)
    x = x + merged @ w_out
    x = x + 0.5 * swiglu_ffn(x, ffn2_norm, ffn2_wg, ffn2_wu, ffn2_wd)
    return x, (k, v, logf.astype(x.dtype), new_S, new_shift)


def setup_inputs(seed: int = 0) -> dict:
    key = jax.random.key(seed)
    ks = iter(jax.random.split(key, 64))
    f32 = jnp.float32
    nrm = lambda shape, scale: scale * jax.random.normal(next(ks), shape, f32)
    L = DEPTH
    n_pages = PAST_LEN // PAGE_SIZE
    n_pool = (DEC_BATCH * n_pages * 5) // 4
    d_in = D_MODEL ** -0.5
    inp = {}
    inp["x_prompt"] = nrm((BATCH, SEQ, D_MODEL), 1.0)
    inp["x_sample"] = nrm((DEC_BATCH, DEC_SEQ, D_MODEL), 1.0)
    inp["cache_k"] = nrm((L, n_pool, PAGE_SIZE, N_HEADS_B, HEAD_DIM), 1.0)
    inp["cache_v"] = nrm((L, n_pool, PAGE_SIZE, N_HEADS_B, HEAD_DIM), 1.0)
    inp["cache_logf"] = jax.nn.log_sigmoid(3.0 + nrm((L, n_pool, PAGE_SIZE, N_HEADS_B), 0.5))
    inp["state_wkv"] = nrm((L, DEC_BATCH, N_HEADS_A, HEAD_DIM, HEAD_DIM), 0.1)
    inp["state_shift"] = nrm((L, DEC_BATCH, N_SHIFT), 1.0)
    perm = jax.random.permutation(next(ks), n_pool)
    inp["page_table"] = perm[:DEC_BATCH * n_pages].reshape(DEC_BATCH, n_pages).astype(jnp.int32)
    gain = lambda n: 1.0 + nrm((L, n), 0.05)
    inp["ffn1_norm"] = gain(D_MODEL)
    inp["ffn1_wg"] = nrm((L, D_MODEL, D_FF), d_in)
    inp["ffn1_wu"] = nrm((L, D_MODEL, D_FF), d_in)
    inp["ffn1_wd"] = nrm((L, D_FF, D_MODEL), D_FF ** -0.5)
    inp["mix_norm"] = gain(D_MODEL)
    inp["w_in"] = nrm((L, D_MODEL, N_IN), d_in)
    inp["shift_mu"] = jax.random.uniform(next(ks), (L, N_SHIFT), f32)
    inp["rwkv_w0"] = -2.0 + nrm((L, D_A), 1.0)
    inp["rwkv_w2"] = nrm((L, DECAY_LORA, D_A), 0.5 * DECAY_LORA ** -0.5)
    inp["rwkv_a0"] = nrm((L, D_A), 0.1)
    inp["rwkv_a2"] = nrm((L, AAA_LORA, D_A), 0.5 * AAA_LORA ** -0.5)
    inp["rwkv_g2"] = nrm((L, GATE_LORA, D_A), GATE_LORA ** -0.5)
    inp["rwkv_k_k"] = 0.85 + nrm((L, D_A), 0.05)
    inp["rwkv_k_a"] = 1.0 + nrm((L, D_A), 0.05)
    inp["rwkv_r_k"] = nrm((L, N_HEADS_A, HEAD_DIM), 0.1)
    inp["rwkv_lnx_w"] = gain(D_A)
    inp["rwkv_lnx_b"] = nrm((L, D_A), 0.02)
    inp["fox_b_f"] = 3.0 + nrm((L, N_HEADS_B), 0.5)
    inp["fox_q_norm"] = gain(HEAD_DIM)
    inp["fox_k_norm"] = gain(HEAD_DIM)
    inp["w_o_a"] = nrm((L, D_A, D_MODEL), D_A ** -0.5)
    inp["w_o_b"] = nrm((L, D_B, D_MODEL), D_B ** -0.5)
    inp["w_out"] = nrm((L, D_MODEL, D_MODEL), d_in)
    inp["ffn2_norm"] = gain(D_MODEL)
    inp["ffn2_wg"] = nrm((L, D_MODEL, D_FF), d_in)
    inp["ffn2_wu"] = nrm((L, D_MODEL, D_FF), d_in)
    inp["ffn2_wd"] = nrm((L, D_FF, D_MODEL), D_FF ** -0.5)
    return inp


def reference(x_prompt, x_sample, cache_k, cache_v, cache_logf, state_wkv, state_shift, page_table,
              ffn1_norm, ffn1_wg, ffn1_wu, ffn1_wd, mix_norm, w_in, shift_mu,
              rwkv_w0, rwkv_w2, rwkv_a0, rwkv_a2, rwkv_g2, rwkv_k_k, rwkv_k_a, rwkv_r_k, rwkv_lnx_w, rwkv_lnx_b,
              fox_b_f, fox_q_norm, fox_k_norm, w_o_a, w_o_b, w_out,
              ffn2_norm, ffn2_wg, ffn2_wu, ffn2_wd):
    Bp = x_prompt.shape[0]
    Bd = x_sample.shape[0]
    n_pages = page_table.shape[1]
    past_len = n_pages * PAGE_SIZE
    xp, xs = x_prompt, x_sample
    pk_l, pv_l, pf_l, pS_l, psh_l = [], [], [], [], []
    sk_l, sv_l, sf_l, sS_l, ssh_l = [], [], [], [], []
    for l in range(DEPTH):
        lw = (ffn1_norm[l], ffn1_wg[l], ffn1_wu[l], ffn1_wd[l], mix_norm[l], w_in[l], shift_mu[l],
              rwkv_w0[l], rwkv_w2[l], rwkv_a0[l], rwkv_a2[l], rwkv_g2[l], rwkv_k_k[l], rwkv_k_a[l], rwkv_r_k[l],
              rwkv_lnx_w[l], rwkv_lnx_b[l], fox_b_f[l], fox_q_norm[l], fox_k_norm[l], w_o_a[l], w_o_b[l], w_out[l],
              ffn2_norm[l], ffn2_wg[l], ffn2_wu[l], ffn2_wd[l])
        prev0 = jnp.zeros((Bp, N_SHIFT), xp.dtype)
        S00 = jnp.zeros((Bp, N_HEADS_A, HEAD_DIM, HEAD_DIM), xp.dtype)
        xp, (k_p, v_p, f_p, S_p, sh_p) = layer(xp, prev0, S00, None, lw)
        pk = cache_k[l][page_table].reshape(Bd, past_len, N_HEADS_B, HEAD_DIM)
        pv = cache_v[l][page_table].reshape(Bd, past_len, N_HEADS_B, HEAD_DIM)
        plf = cache_logf[l][page_table].reshape(Bd, past_len, N_HEADS_B)
        xs, (k_s, v_s, f_s, S_s, sh_s) = layer(xs, state_shift[l], state_wkv[l], (pk, pv, plf), lw)
        pk_l.append(k_p); pv_l.append(v_p); pf_l.append(f_p); pS_l.append(S_p); psh_l.append(sh_p)
        sk_l.append(k_s); sv_l.append(v_s); sf_l.append(f_s); sS_l.append(S_s); ssh_l.append(sh_s)
    prompt_k = jnp.stack(pk_l, 0)
    prompt_v = jnp.stack(pv_l, 0)
    prompt_logf = jnp.stack(pf_l, 0)
    prompt_wkv = jnp.stack(pS_l, 0)
    prompt_shift = jnp.stack(psh_l, 0)
    sample_k = jnp.stack(sk_l, 0)
    sample_v = jnp.stack(sv_l, 0)
    sample_logf = jnp.stack(sf_l, 0)
    sample_wkv = jnp.stack(sS_l, 0)
    sample_shift = jnp.stack(ssh_l, 0)
    return (xp, xs, prompt_k, prompt_v, prompt_logf, prompt_wkv, prompt_shift,
            sample_k, sample_v, sample_logf, sample_wkv, sample_shift)
```

```python
import functools

import jax
import jax.numpy as jnp
from jax import lax
from jax.experimental import pallas as pl
from jax.experimental.pallas import tpu as pltpu

F32 = jnp.float32
BF16 = jnp.bfloat16

HEAD_DIM = 64
LANES = 128
PAIR = LANES // HEAD_DIM
NORM_EPS = 1e-6
LNX_EPS = 64e-5
NEG_BIG = -1e30
CHUNK = 64
VMEM_LIMIT = 56 * 1024 * 1024

NN = (((1,), (0,)), ((), ()))
NT = (((1,), (1,)), ((), ()))
TN = (((0,), (0,)), ((), ()))


def _pieces(x, n):
    if x.dtype == BF16:
        return [x]
    out = []
    rem = x
    for i in range(n):
        p = rem.astype(BF16)
        out.append(p)
        if i + 1 < n:
            rem = rem - p.astype(F32)
    return out


def _mm(a, b, dims=NN, pa=1, pb=1):
    pa_l = _pieces(a, pa)
    pb_l = _pieces(b, pb)
    order = max(len(pa_l), len(pb_l))
    acc = None
    for i in reversed(range(len(pa_l))):
        for j in reversed(range(len(pb_l))):
            if i + j >= order:
                continue
            t = lax.dot_general(pa_l[i], pb_l[j], dims, preferred_element_type=F32)
            acc = t if acc is None else acc + t
    return acc


def _sigmoid(x):
    return 1.0 / (1.0 + jnp.exp(-x))


def _softplus(x):
    return jnp.maximum(x, 0.0) + jnp.log1p(jnp.exp(-jnp.abs(x)))


def _rms(x, g):
    return x * lax.rsqrt(jnp.mean(x * x, axis=-1, keepdims=True) + NORM_EPS) * g


def _iota(shape, dim):
    return lax.broadcasted_iota(jnp.int32, shape, dim)


def _div_pow2(x, n):
    assert n & (n - 1) == 0
    return lax.shift_right_logical(x, n.bit_length() - 1)


def _mod_pow2(x, n):
    assert n & (n - 1) == 0
    return x & (n - 1)


def _resident(shape):
    nd = len(shape)
    return pl.BlockSpec(shape, lambda *_: (0,) * nd, pipeline_mode=pl.Buffered(1))


def _params(sem):
    return pltpu.CompilerParams(dimension_semantics=sem, vmem_limit_bytes=VMEM_LIMIT)


def _row_tile(n):
    for t in (256, 128, 64, 32, 16, 8):
        if n % t == 0:
            return t
    raise ValueError(f"row count {n} is not a multiple of 8")


def _ffn_kernel(x_ref, g_ref, wg_ref, wu_ref, wd_ref, o_ref):
    x = x_ref[...]
    h = _rms(x, g_ref[...]).astype(BF16)
    gate = jnp.dot(h, wg_ref[...], preferred_element_type=F32)
    up = jnp.dot(h, wu_ref[...], preferred_element_type=F32)
    act = (gate * _sigmoid(gate) * up).astype(BF16)
    o_ref[...] = x + 0.5 * jnp.dot(act, wd_ref[...], preferred_element_type=F32)


def _ffn(x, g, wg, wu, wd):
    n, d = x.shape
    dff = wg.shape[1]
    tm = _row_tile(n)
    return pl.pallas_call(
        _ffn_kernel,
        grid=(n // tm,),
        in_specs=[pl.BlockSpec((tm, d), lambda i: (i, 0)), _resident((1, d)),
                  _resident((d, dff)), _resident((d, dff)), _resident((dff, d))],
        out_specs=pl.BlockSpec((tm, d), lambda i: (i, 0)),
        out_shape=jax.ShapeDtypeStruct((n, d), F32),
        compiler_params=_params(("arbitrary",)),
        name="ffn",
    )(x, g, wg, wu, wd)


def _head_sum(x, bd_ref, pieces):
    return _mm(x, bd_ref[...], NN, pa=pieces, pb=1)


def _mixproj_kernel(x_ref, g_ref, w_ref, wf_ref, bf_ref, qn_ref, kn_ref, bd_ref,
                    pa_ref, q_ref, k_ref, v_ref, ga_ref, gb_ref, lf_ref, cum_ref, carry_ref,
                    *, n_shift, d_b, d_model, n_heads, seq_len, tm):
    i = pl.program_id(0)
    h = _rms(x_ref[...], g_ref[...]).astype(BF16)
    c1 = n_shift
    c2 = c1 + d_b
    c3 = c2 + d_b
    c4 = c3 + d_b
    c5 = c4 + d_model
    c6 = c5 + d_model
    pa_ref[...] = jnp.dot(h, w_ref[:, 0:c1], preferred_element_type=F32)
    q = jnp.dot(h, w_ref[:, c1:c2], preferred_element_type=F32)
    k = jnp.dot(h, w_ref[:, c2:c3], preferred_element_type=F32)
    v_ref[...] = jnp.dot(h, w_ref[:, c3:c4], preferred_element_type=F32)
    ga_ref[...] = jnp.dot(h, w_ref[:, c4:c5], preferred_element_type=F32)
    gb_ref[...] = jnp.dot(h, w_ref[:, c5:c6], preferred_element_type=F32)
    inv_hd = 1.0 / HEAD_DIM
    q = q * lax.rsqrt(_head_sum(q * q, bd_ref, 2) * inv_hd + NORM_EPS) * qn_ref[...]
    k = k * lax.rsqrt(_head_sum(k * k, bd_ref, 2) * inv_hd + NORM_EPS) * kn_ref[...]
    q_ref[...] = q * (1.0 / float(HEAD_DIM) ** 0.5)
    k_ref[...] = k
    z = jnp.dot(h, wf_ref[...], preferred_element_type=F32) + bf_ref[...]
    logf = jnp.minimum(z, 0.0) - jnp.log1p(jnp.exp(-jnp.abs(z)))
    logf = jnp.where(_iota(logf.shape, 1) < n_heads, logf, 0.0)
    row = _iota((tm, tm), 0)
    col = _iota((tm, tm), 1)
    keep = col <= row
    if seq_len < tm:
        keep = keep & (_div_pow2(row, seq_len) == _div_pow2(col, seq_len))
    tri = jnp.where(keep, 1.0, 0.0).astype(BF16)
    cum = _mm(tri, logf, NN, pa=1, pb=3)
    if seq_len > tm:
        tiles_per_seq = seq_len // tm

        @pl.when(i % tiles_per_seq == 0)
        def _():
            carry_ref[...] = jnp.zeros_like(carry_ref)

        cum = cum + carry_ref[...]
        carry_ref[...] = cum[tm - 1:tm, :]
    lf_ref[...] = logf[:, :n_heads]
    cum_ref[...] = cum[:, :n_heads]


def _mixproj(x, g, w_main, w_f, b_f, qn, kn, bd, *, n_shift, d_b, n_heads, seq_len):
    n, d = x.shape
    tm = _row_tile(n)
    assert seq_len % tm == 0 or tm % seq_len == 0
    row = lambda w: pl.BlockSpec((tm, w), lambda i: (i, 0))
    shp = lambda w: jax.ShapeDtypeStruct((n, w), F32)
    kern = functools.partial(_mixproj_kernel, n_shift=n_shift, d_b=d_b, d_model=d, n_heads=n_heads,
                             seq_len=seq_len, tm=tm)
    return pl.pallas_call(
        kern,
        grid=(n // tm,),
        in_specs=[row(d), _resident((1, d)), _resident(w_main.shape), _resident(w_f.shape),
                  _resident((1, LANES)), _resident((1, d_b)), _resident((1, d_b)), _resident(bd.shape)],
        out_specs=[row(n_shift), row(d_b), row(d_b), row(d_b), row(d), row(d), row(n_heads), row(n_heads)],
        out_shape=[shp(n_shift), shp(d_b), shp(d_b), shp(d_b), shp(d), shp(d), shp(n_heads), shp(n_heads)],
        scratch_shapes=[pltpu.VMEM((1, LANES), F32)],
        compiler_params=_params(("arbitrary",)),
        name="mixproj",
    )(x, g, w_main, w_f, b_f, qn, kn, bd)


def _rprep_kernel(p_ref, init_ref, mu_ref, w0_ref, a0_ref, lora_ref, g2_ref, kk_ref, ka_ref, bd_ref,
                  r_ref, k_ref, v_ref, kap_ref, b_ref, lw_ref, g_ref, last_ref, *, d_a, seq_len, tm):
    i = pl.program_id(0)
    p = p_ref[...]
    prev = pltpu.roll(p, 1, 0)
    rowi = _iota(p.shape, 0)
    if seq_len >= tm:
        tiles_per_seq = seq_len // tm

        @pl.when(i % tiles_per_seq == 0)
        def _():
            last_ref[...] = init_ref[...]

        prev = jnp.where(rowi == 0, last_ref[...], prev)
        last_ref[...] = p[tm - 1:tm, :]
    else:
        prev = jnp.where(_mod_pow2(rowi, seq_len) == 0, init_ref[...], prev)
    xm = p + (prev - p) * mu_ref[...]
    r = xm[:, 0:d_a]
    k = xm[:, d_a:2 * d_a]
    v = xm[:, 2 * d_a:3 * d_a]
    xwa = xm[:, 3 * d_a:3 * d_a + LANES]
    xg = xm[:, 3 * d_a + LANES:3 * d_a + 2 * LANES]
    xwa = jnp.where(_iota(xwa.shape, 1) < HEAD_DIM, jnp.tanh(xwa), xwa)
    lo = _mm(xwa, lora_ref[...], NN, pa=2, pb=2)
    w_log = -_softplus(-(w0_ref[...] + lo[:, 0:d_a])) - 0.5
    lw_ref[...] = -jnp.exp(w_log)
    a = _sigmoid(a0_ref[...] + lo[:, d_a:2 * d_a])
    g_ref[...] = _mm(_sigmoid(xg), g2_ref[...], NN, pa=2, pb=2)
    kk = k * kk_ref[...]
    ss = _head_sum(kk * kk, bd_ref, 3)
    kk = kk / jnp.maximum(jnp.sqrt(ss), 1e-12)
    r_ref[...] = r
    k_ref[...] = k * (1.0 + (a - 1.0) * ka_ref[...])
    v_ref[...] = v
    kap_ref[...] = kk
    b_ref[...] = kk * a


def _rprep(p_a, init_rows, mu, w0, a0, lora, g2, k_k, k_a, bd, *, d_a, seq_len):
    n, ns = p_a.shape
    tm = _row_tile(n)
    row = lambda w: pl.BlockSpec((tm, w), lambda i: (i, 0))
    shp = jax.ShapeDtypeStruct((n, d_a), F32)
    if seq_len >= tm:
        tps = seq_len // tm
        init_spec = pl.BlockSpec((None, 1, ns), lambda i: (i // tps, 0, 0))
        init_rows = init_rows.reshape(-1, 1, ns)
    else:
        init_spec = row(ns)
    kern = functools.partial(_rprep_kernel, d_a=d_a, seq_len=seq_len, tm=tm)
    return pl.pallas_call(
        kern,
        grid=(n // tm,),
        in_specs=[row(ns), init_spec, _resident((1, ns)), _resident((1, d_a)), _resident((1, d_a)),
                  _resident(lora.shape), _resident(g2.shape), _resident((1, d_a)), _resident((1, d_a)),
                  _resident(bd.shape)],
        out_specs=[row(d_a)] * 7,
        out_shape=[shp] * 7,
        scratch_shapes=[pltpu.VMEM((1, ns), F32)],
        compiler_params=_params(("arbitrary",)),
        name="rprep",
    )(p_a, init_rows, mu, w0, a0, lora, g2, k_k, k_a, bd)


def _stack(x, m0):
    return jnp.concatenate([jnp.where(m0, x, 0.0), jnp.where(m0, 0.0, x)], axis=0)


def _rscan_kernel(r_ref, k_ref, v_ref, kap_ref, b_ref, lw_ref, s0_ref, y_ref, st_ref, s_scr, *, n_chunks, hp):
    c = CHUNK
    c2 = PAIR * c
    blk = pl.program_id(2)

    @pl.when(blk == 0)
    def _():
        s_scr[...] = s0_ref[...]

    m0 = _iota((c, LANES), 1) < HEAD_DIM
    ri = _iota((c2, c2), 0)
    ci = _iota((c2, c2), 1)
    same = _div_pow2(ri, c) == _div_pow2(ci, c)
    strict = same & (ci < ri)
    incl = same & (ci <= ri)
    eye = ri == ci
    tri_c = jnp.where(_iota((c, c), 1) <= _iota((c, c), 0), 1.0, 0.0).astype(BF16)

    def chunk_body(j, carry):
        sl = pl.ds(pl.multiple_of(j * c, c), c)
        lw = lw_ref[sl, :]
        cum = _mm(tri_c, lw, NN, pa=1, pb=3)
        cum_end = cum[c - 1:c, :]
        e_neg = jnp.exp(-cum)
        e_rem = jnp.exp(cum_end - cum)
        kap = kap_ref[sl, :]
        b = b_ref[sl, :]
        k = k_ref[sl, :]
        a_t = _stack(-kap * jnp.exp(cum - lw), m0)
        r_t = _stack(r_ref[sl, :] * jnp.exp(cum), m0)
        b_t = _stack(b * e_neg, m0)
        k_t = _stack(k * e_neg, m0)
        b_p = _stack(b * e_rem, m0)
        k_p = _stack(k * e_rem, m0)
        v_s = _stack(v_ref[sl, :], m0)

        l_ab = jnp.where(strict, _mm(a_t, b_t, NT, hp, hp), 0.0)
        a_ak = jnp.where(strict, _mm(a_t, k_t, NT, hp, hp), 0.0)
        a_rb = jnp.where(incl, _mm(r_t, b_t, NT, hp, hp), 0.0)
        a_rk = jnp.where(incl, _mm(r_t, k_t, NT, hp, hp), 0.0)

        t_inv = jnp.where(eye, 1.0, 0.0) + l_ab
        l_pow = l_ab
        n_sq = c.bit_length() - 2
        for _ in range(n_sq):
            l_pow = _mm(l_pow, l_pow, NN, hp, hp)
            t_inv = t_inv + _mm(t_inv, l_pow, NN, hp, hp)

        a_p = _mm(t_inv, a_t, NN, hp, hp)
        u0 = _mm(t_inv, _mm(a_ak, v_s, NN, hp, hp), NN, hp, hp)
        r_p = r_t + _mm(a_rb, a_p, NN, hp, hp)
        y0 = _mm(a_rk, v_s, NN, hp, hp) + _mm(a_rb, u0, NN, hp, hp)
        m_c = jnp.where(eye, jnp.exp(cum_end), 0.0) + _mm(a_p, b_p, TN, hp, hp)
        n_c = _mm(u0, b_p, TN, hp, hp) + _mm(v_s, k_p, TN, hp, hp)

        s = s_scr[...]
        y_st = _mm(r_p, s, NT, hp, hp) + y0
        y_ref[sl, :] = y_st[0:c, :] + y_st[c:c2, :]
        s_scr[...] = _mm(s, m_c, NN, hp, hp) + n_c
        return carry

    lax.fori_loop(0, n_chunks, chunk_body, 0)

    @pl.when(blk == pl.num_programs(2) - 1)
    def _():
        st_ref[...] = s_scr[...]


def _rscan(r, k, v, kap, b, lw, s0, *, hp=2):
    bsz, t, d_a = r.shape
    npair = d_a // LANES
    n_chunks = min(8, t // CHUNK)
    tg = n_chunks * CHUNK
    assert t % tg == 0
    seq = pl.BlockSpec((None, tg, LANES), lambda bi, j, ci: (bi, ci, j))
    st = pl.BlockSpec((None, None, LANES, LANES), lambda bi, j, ci: (bi, j, 0, 0))
    kern = functools.partial(_rscan_kernel, n_chunks=n_chunks, hp=hp)
    return pl.pallas_call(
        kern,
        grid=(bsz, npair, t // tg),
        in_specs=[seq] * 6 + [st],
        out_specs=[seq, st],
        out_shape=[jax.ShapeDtypeStruct((bsz, t, d_a), F32),
                   jax.ShapeDtypeStruct((bsz, npair, LANES, LANES), F32)],
        scratch_shapes=[pltpu.VMEM((LANES, LANES), F32)],
        compiler_params=_params(("arbitrary", "arbitrary", "arbitrary")),
        name="rscan",
    )(r, k, v, kap, b, lw, s0)


def _pair_state(s):
    bsz, h, dv, dk = s.shape
    s = s.reshape(bsz, h // PAIR, PAIR, dv, dk)
    eye = jnp.eye(PAIR, dtype=s.dtype)
    out = jnp.einsum("bjpvk,pq->bjpvqk", s, eye)
    return out.reshape(bsz, h // PAIR, PAIR * dv, PAIR * dk)


def _unpair_state(sp, h):
    bsz, npair = sp.shape[:2]
    s = sp.reshape(bsz, npair, PAIR, HEAD_DIM, PAIR, HEAD_DIM)
    s = jnp.stack([s[:, :, p, :, p, :] for p in range(PAIR)], axis=2)
    return s.reshape(bsz, h, HEAD_DIM, HEAD_DIM)


def _attn_kernel(q_ref, k_ref, v_ref, cq_ref, ck_ref, o_ref, m_scr, l_scr, acc_scr, *, tq):
    qi = pl.program_id(2)
    ki = pl.program_id(3)

    @pl.when(ki == 0)
    def _():
        m_scr[...] = jnp.full_like(m_scr, NEG_BIG)
        l_scr[...] = jnp.zeros_like(l_scr)
        acc_scr[...] = jnp.zeros_like(acc_scr)

    @pl.when(ki <= qi)
    def _():
        q = q_ref[...]
        m0 = _iota(q.shape, 1) < HEAD_DIM
        q_st = _stack(q, m0).astype(BF16)
        s = lax.dot_general(q_st, k_ref[...].astype(BF16), NT, preferred_element_type=F32)
        rows = _iota(s.shape, 0)
        cols = _iota(s.shape, 1)
        ck = jnp.where(rows < tq, ck_ref[0:1, :], ck_ref[1:2, :])
        s = s + cq_ref[...] - ck
        s = jnp.where((ki < qi) | (cols <= _mod_pow2(rows, tq)), s, NEG_BIG)
        m_old = m_scr[...]
        m_new = jnp.maximum(m_old, jnp.max(s, axis=-1, keepdims=True))
        alpha = jnp.exp(m_old - m_new)
        p = jnp.exp(s - m_new)
        l_scr[...] = alpha * l_scr[...] + jnp.sum(p, axis=-1, keepdims=True)
        acc_scr[...] = alpha * acc_scr[...] + jnp.dot(p.astype(BF16), v_ref[...].astype(BF16),
                                                      preferred_element_type=F32)
        m_scr[...] = m_new

    @pl.when(ki == qi)
    def _():
        o = acc_scr[...] / l_scr[...]
        m0 = _iota((tq, LANES), 1) < HEAD_DIM
        o_ref[...] = jnp.where(m0, o[0:tq, :], o[tq:2 * tq, :])


def _attn(q, k, v, cq, ck, *, tq=256):
    bsz, t, d_b = q.shape
    npair = d_b // LANES
    nq = t // tq
    qspec = pl.BlockSpec((None, tq, LANES), lambda b, j, qi, ki: (b, qi, j))
    kspec = pl.BlockSpec((None, tq, LANES), lambda b, j, qi, ki: (b, jnp.minimum(ki, qi), j))
    cqspec = pl.BlockSpec((None, None, None, 2 * tq, 1), lambda b, j, qi, ki: (b, j, qi, 0, 0))
    ckspec = pl.BlockSpec((None, None, 2, tq), lambda b, j, qi, ki: (b, j, 0, jnp.minimum(ki, qi)))
    return pl.pallas_call(
        functools.partial(_attn_kernel, tq=tq),
        grid=(bsz, npair, nq, nq),
        in_specs=[qspec, kspec, kspec, cqspec, ckspec],
        out_specs=qspec,
        out_shape=jax.ShapeDtypeStruct((bsz, t, d_b), F32),
        scratch_shapes=[pltpu.VMEM((2 * tq, 1), F32), pltpu.VMEM((2 * tq, 1), F32),
                        pltpu.VMEM((2 * tq, LANES), F32)],
        compiler_params=_params(("arbitrary",) * 4),
        name="attn",
    )(q, k, v, cq, ck)


def _dattn_kernel(pt_ref, q_ref, cn_ref, kc_ref, vc_ref, lf_ref, kn_ref, vn_ref, bn_ref, o_ref,
                  wq_scr, m_scr, l_scr, acc_scr, suf_scr, *, n_heads, t_new, page):
    del pt_ref
    pi = pl.program_id(1)
    n_rows = n_heads * t_new

    @pl.when(pi == 0)
    def _():
        q = q_ref[...]
        head = _div_pow2(_iota(q.shape, 1), HEAD_DIM)
        wq_scr[...] = jnp.concatenate([jnp.where(head == h, q, 0.0) for h in range(n_heads)],
                                      axis=0).astype(BF16)
        m_scr[...] = jnp.full_like(m_scr, NEG_BIG)
        l_scr[...] = jnp.zeros_like(l_scr)
        acc_scr[...] = jnp.zeros_like(acc_scr)
        suf_scr[...] = jnp.zeros_like(suf_scr)

    def update(s, vals):
        m_old = m_scr[...]
        m_new = jnp.maximum(m_old, jnp.max(s, axis=-1, keepdims=True))
        alpha = jnp.exp(m_old - m_new)
        p = jnp.exp(s - m_new)
        l_scr[...] = alpha * l_scr[...] + jnp.sum(p, axis=-1, keepdims=True)
        acc_scr[...] = alpha * acc_scr[...] + jnp.dot(p.astype(BF16), vals.astype(BF16),
                                                      preferred_element_type=F32)
        m_scr[...] = m_new

    lf = lf_ref[...]
    later = jnp.where(_iota((page, page), 0) > _iota((page, page), 1), 1.0, 0.0).astype(BF16)
    suf = _mm(lf, later, NN, pa=3, pb=1) + suf_scr[...]
    suf_scr[...] = suf_scr[...] + jnp.sum(lf, axis=-1, keepdims=True)
    bias = jnp.broadcast_to(suf[:, None, :], (n_heads, t_new, page)).reshape(n_rows, page)
    s = lax.dot_general(wq_scr[...], kc_ref[...].astype(BF16), NT, preferred_element_type=F32)
    update(s + cn_ref[...] + bias, vc_ref[...])

    @pl.when(pi == pl.num_programs(1) - 1)
    def _():
        s_new = lax.dot_general(wq_scr[...], kn_ref[...].astype(BF16), NT, preferred_element_type=F32)
        update(s_new + bn_ref[...], vn_ref[...])
        o = acc_scr[...] / l_scr[...]
        head = _div_pow2(_iota((t_new, o.shape[1]), 1), HEAD_DIM)
        out = jnp.zeros((t_new, o.shape[1]), F32)
        for h in range(n_heads):
            out = out + jnp.where(head == h, o[h * t_new:(h + 1) * t_new, :], 0.0)
        o_ref[...] = out


def _dattn(page_table, q, cn_col, cache_k, cache_v, cache_lf_t, k_new, v_new, bias_new, *, n_heads, t_new):
    bsz, n_pages = page_table.shape
    _, page, d_b = cache_k.shape
    n_rows = n_heads * t_new
    per_b = lambda *blk: pl.BlockSpec((None,) + blk, lambda b, p, pt: (b,) + (0,) * len(blk))
    paged = lambda *blk: pl.BlockSpec((None,) + blk, lambda b, p, pt: (pt[b, n_pages - 1 - p],) + (0,) * len(blk))
    grid_spec = pltpu.PrefetchScalarGridSpec(
        num_scalar_prefetch=1,
        grid=(bsz, n_pages),
        in_specs=[per_b(t_new, d_b), per_b(n_rows, 1), paged(page, d_b), paged(page, d_b),
                  paged(n_heads, page), per_b(page, d_b), per_b(page, d_b), per_b(n_rows, page)],
        out_specs=per_b(t_new, d_b),
        scratch_shapes=[pltpu.VMEM((n_rows, d_b), BF16), pltpu.VMEM((n_rows, 1), F32),
                        pltpu.VMEM((n_rows, 1), F32), pltpu.VMEM((n_rows, d_b), F32),
                        pltpu.VMEM((n_heads, 1), F32)],
    )
    kern = functools.partial(_dattn_kernel, n_heads=n_heads, t_new=t_new, page=page)
    return pl.pallas_call(
        kern,
        grid_spec=grid_spec,
        out_shape=jax.ShapeDtypeStruct((bsz, t_new, d_b), F32),
        compiler_params=_params(("arbitrary", "arbitrary")),
        name="dattn",
    )(page_table, q, cn_col, cache_k, cache_v, cache_lf_t, k_new, v_new, bias_new)


def _merge_kernel(x_ref, y_ref, r_ref, k_ref, v_ref, g_ref, yb_ref, ga_ref, gb_ref,
                  lnw_ref, lnb_ref, rk_ref, bd_ref, woa_ref, wob_ref, wout_ref, o_ref):
    inv_hd = 1.0 / HEAD_DIM
    y = y_ref[...]
    mean = _head_sum(y, bd_ref, 3) * inv_hd
    yc = y - mean
    var = _head_sum(yc * yc, bd_ref, 3) * inv_hd
    yn = yc * lax.rsqrt(var + LNX_EPS) * lnw_ref[...] + lnb_ref[...]
    v = v_ref[...]
    bonus = _head_sum(r_ref[...] * k_ref[...] * rk_ref[...], bd_ref, 3) * v
    ya = ((yn + bonus) * g_ref[...]).astype(BF16)
    pa = jnp.dot(ya, woa_ref[...], preferred_element_type=F32)
    pb = jnp.dot(yb_ref[...].astype(BF16), wob_ref[...], preferred_element_type=F32)
    merged = _sigmoid(ga_ref[...]) * pa + _sigmoid(gb_ref[...]) * pb
    o_ref[...] = x_ref[...] + jnp.dot(merged.astype(BF16), wout_ref[...], preferred_element_type=F32)


def _merge(x, y, r, k, v, g, yb, ga, gb, lnw, lnb, rk, bd, woa, wob, wout):
    n, d = x.shape
    d_a = y.shape[1]
    tm = _row_tile(n)
    row = lambda w: pl.BlockSpec((tm, w), lambda i: (i, 0))
    return pl.pallas_call(
        _merge_kernel,
        grid=(n // tm,),
        in_specs=[row(d)] + [row(d_a)] * 6 + [row(d), row(d)] + [_resident((1, d_a))] * 3
                 + [_resident(bd.shape), _resident(woa.shape), _resident(wob.shape), _resident(wout.shape)],
        out_specs=row(d),
        out_shape=jax.ShapeDtypeStruct((n, d), F32),
        compiler_params=_params(("arbitrary",)),
        name="merge",
    )(x, y, r, k, v, g, yb, ga, gb, lnw, lnb, rk, bd, woa, wob, wout)


def _prep_weights(lw, n_heads_a, n_heads_b):
    (ffn1_norm, ffn1_wg, ffn1_wu, ffn1_wd, mix_norm, w_in, shift_mu, w0, w2, a0, a2, g2, k_k, k_a, r_k,
     lnx_w, lnx_b, b_f, q_norm, k_norm, w_o_a, w_o_b, w_out, ffn2_norm, ffn2_wg, ffn2_wu, ffn2_wd) = lw
    d = w_in.shape[0]
    d_a = n_heads_a * HEAD_DIM
    d_b = n_heads_b * HEAD_DIM
    n_shift = shift_mu.shape[0]
    c4 = n_shift + 3 * d_b
    row = lambda t: t.reshape(1, -1)
    w_main = jnp.concatenate([w_in[:, :c4], w_in[:, c4 + n_heads_b:]], axis=1).astype(BF16)
    w_f = jnp.pad(w_in[:, c4:c4 + n_heads_b], ((0, 0), (0, LANES - n_heads_b))).astype(BF16)
    dl, al = w2.shape[0], a2.shape[0]
    assert dl == HEAD_DIM and al == HEAD_DIM and g2.shape[0] == LANES
    lora = jnp.zeros((LANES, 2 * d_a), F32).at[:dl, :d_a].set(w2).at[dl:, d_a:].set(a2)
    bd = jnp.kron(jnp.eye(max(d_a, d_b) // HEAD_DIM, dtype=F32), jnp.ones((HEAD_DIM, HEAD_DIM), F32)).astype(BF16)
    return dict(
        d=d, d_a=d_a, d_b=d_b, n_shift=n_shift, n_heads_a=n_heads_a, n_heads_b=n_heads_b,
        ffn1=(row(ffn1_norm), ffn1_wg.astype(BF16), ffn1_wu.astype(BF16), ffn1_wd.astype(BF16)),
        ffn2=(row(ffn2_norm), ffn2_wg.astype(BF16), ffn2_wu.astype(BF16), ffn2_wd.astype(BF16)),
        mix_norm=row(mix_norm), w_main=w_main, w_f=w_f,
        b_f=jnp.pad(b_f, (0, LANES - n_heads_b)).reshape(1, LANES),
        q_norm=row(jnp.tile(q_norm, n_heads_b)), k_norm=row(jnp.tile(k_norm, n_heads_b)), bd=bd,
        mu=row(shift_mu), w0=row(w0), a0=row(a0), lora=lora, g2=g2, k_k=row(k_k), k_a=row(k_a),
        r_k=row(r_k), lnx_w=row(lnx_w), lnx_b=row(lnx_b),
        w_o_a=w_o_a.astype(BF16), w_o_b=w_o_b.astype(BF16), w_out=w_out.astype(BF16),
    )


def _layer(x3, prev_shift, s0, past, w):
    bsz, t, d = x3.shape
    d_a, d_b, ha, hb = w["d_a"], w["d_b"], w["n_heads_a"], w["n_heads_b"]
    n = bsz * t
    x = x3.reshape(n, d)
    x1 = _ffn(x, *w["ffn1"])
    p_a, q, k, v, g_a, g_b, logf, cum = _mixproj(
        x1, w["mix_norm"], w["w_main"], w["w_f"], w["b_f"], w["q_norm"], w["k_norm"], w["bd"],
        n_shift=w["n_shift"], d_b=d_b, n_heads=hb, seq_len=t)

    tm = _row_tile(n)
    init_rows = prev_shift if t >= tm else jnp.repeat(prev_shift, t, axis=0)
    r, km, va, kap, b, lw, g = _rprep(p_a, init_rows, w["mu"], w["w0"], w["a0"], w["lora"], w["g2"],
                                      w["k_k"], w["k_a"], w["bd"], d_a=d_a, seq_len=t)
    t_pad = -(-t // CHUNK) * CHUNK
    seq = lambda a: jnp.pad(a.reshape(bsz, t, d_a), ((0, 0), (0, t_pad - t), (0, 0)))
    y_raw, s_fin = _rscan(seq(r), seq(km), seq(va), seq(kap), seq(b), seq(lw), _pair_state(s0))
    y_raw = y_raw[:, :t].reshape(n, d_a)
    new_s = _unpair_state(s_fin, ha)
    new_shift = p_a.reshape(bsz, t, -1)[:, -1]

    npair = d_b // LANES
    if past is None:
        tq = 256
        cq = cum.reshape(bsz, t // tq, tq, npair, PAIR).transpose(0, 3, 1, 4, 2).reshape(bsz, npair, t // tq, PAIR * tq, 1)
        ck = cum.reshape(bsz, t, npair, PAIR).transpose(0, 2, 3, 1)
        y_b = _attn(q.reshape(bsz, t, d_b), k.reshape(bsz, t, d_b), v.reshape(bsz, t, d_b), cq, ck, tq=tq)
    else:
        cache_k, cache_v, cache_lf, page_table = past
        n_pool, page = cache_k.shape[:2]
        cn = cum.reshape(bsz, t, hb)
        cn_col = cn.transpose(0, 2, 1).reshape(bsz, hb * t, 1)
        tpos = jnp.arange(t)
        ok = tpos[None, :] <= tpos[:, None]
        bn = cn.transpose(0, 2, 1)[:, :, :, None] - cn.transpose(0, 2, 1)[:, :, None, :]
        bn = jnp.where(ok[None, None], bn, NEG_BIG).reshape(bsz, hb * t, t)
        bn = jnp.pad(bn, ((0, 0), (0, 0), (0, page - t)), constant_values=NEG_BIG)
        padp = lambda a: jnp.pad(a.reshape(bsz, t, d_b), ((0, 0), (0, page - t), (0, 0)))
        y_b = _dattn(page_table, q.reshape(bsz, t, d_b), cn_col,
                     cache_k.reshape(n_pool, page, d_b), cache_v.reshape(n_pool, page, d_b),
                     cache_lf.transpose(0, 2, 1), padp(k), padp(v), bn, n_heads=hb, t_new=t)
    y_b = y_b.reshape(n, d_b)

    x2 = _merge(x1, y_raw, r, km, va, g, y_b, g_a, g_b, w["lnx_w"], w["lnx_b"], w["r_k"], w["bd"],
                w["w_o_a"], w["w_o_b"], w["w_out"])
    x3o = _ffn(x2, *w["ffn2"]).reshape(bsz, t, d)
    return x3o, (k.reshape(bsz, t, hb, HEAD_DIM), v.reshape(bsz, t, hb, HEAD_DIM),
                 logf.reshape(bsz, t, hb), new_s, new_shift)


def kernel(x_prompt, x_sample, cache_k, cache_v, cache_logf, state_wkv, state_shift, page_table,
           ffn1_norm, ffn1_wg, ffn1_wu, ffn1_wd, mix_norm, w_in, shift_mu,
           rwkv_w0, rwkv_w2, rwkv_a0, rwkv_a2, rwkv_g2, rwkv_k_k, rwkv_k_a, rwkv_r_k, rwkv_lnx_w, rwkv_lnx_b,
           fox_b_f, fox_q_norm, fox_k_norm, w_o_a, w_o_b, w_out,
           ffn2_norm, ffn2_wg, ffn2_wu, ffn2_wd):
    depth = w_in.shape[0]
    n_heads_a = state_wkv.shape[2]
    n_heads_b = cache_k.shape[3]
    bp = x_prompt.shape[0]
    xp, xs = x_prompt, x_sample
    outs_p, outs_s = [], []
    for l in range(depth):
        lw = (ffn1_norm[l], ffn1_wg[l], ffn1_wu[l], ffn1_wd[l], mix_norm[l], w_in[l], shift_mu[l],
              rwkv_w0[l], rwkv_w2[l], rwkv_a0[l], rwkv_a2[l], rwkv_g2[l], rwkv_k_k[l], rwkv_k_a[l],
              rwkv_r_k[l].reshape(-1), rwkv_lnx_w[l], rwkv_lnx_b[l], fox_b_f[l], fox_q_norm[l], fox_k_norm[l],
              w_o_a[l], w_o_b[l], w_out[l], ffn2_norm[l], ffn2_wg[l], ffn2_wu[l], ffn2_wd[l])
        w = _prep_weights(lw, n_heads_a, n_heads_b)
        prev0 = jnp.zeros((bp, w["n_shift"]), xp.dtype)
        s00 = jnp.zeros((bp, n_heads_a, HEAD_DIM, HEAD_DIM), xp.dtype)
        xp, op = _layer(xp, prev0, s00, None, w)
        xs, os_ = _layer(xs, state_shift[l], state_wkv[l], (cache_k[l], cache_v[l], cache_logf[l], page_table), w)
        outs_p.append(op)
        outs_s.append(os_)
    stack = lambda outs, i: jnp.stack([o[i] for o in outs], 0)
    return ((xp, xs) + tuple(stack(outs_p, i) for i in range(5)) + tuple(stack(outs_s, i) for i in range(5)))
```

```python
import functools

import jax
import jax.numpy as jnp
from jax import lax
from jax.experimental import pallas as pl
from jax.experimental.pallas import tpu as pltpu

F32 = jnp.float32
BF16 = jnp.bfloat16

HEAD_DIM = 64
LANES = 128
PAIR = LANES // HEAD_DIM
NORM_EPS = 1e-6
LNX_EPS = 64e-5
NEG_BIG = -1e30
CHUNK = 64
VMEM_LIMIT = 56 * 1024 * 1024

NN = (((1,), (0,)), ((), ()))
NT = (((1,), (1,)), ((), ()))
TN = (((0,), (0,)), ((), ()))


def _pieces(x, n):
    if x.dtype == BF16:
        return [x]
    out = []
    rem = x
    for i in range(n):
        p = rem.astype(BF16)
        out.append(p)
        if i + 1 < n:
            rem = rem - p.astype(F32)
    return out


def _mm(a, b, dims=NN, pa=1, pb=1):
    pa_l = _pieces(a, pa)
    pb_l = _pieces(b, pb)
    order = max(len(pa_l), len(pb_l))
    acc = None
    for i in reversed(range(len(pa_l))):
        for j in reversed(range(len(pb_l))):
            if i + j >= order:
                continue
            t = lax.dot_general(pa_l[i], pb_l[j], dims, preferred_element_type=F32)
            acc = t if acc is None else acc + t
    return acc


def _sigmoid(x):
    return 1.0 / (1.0 + jnp.exp(-x))


def _softplus(x):
    return jnp.maximum(x, 0.0) + jnp.log1p(jnp.exp(-jnp.abs(x)))


def _rms(x, g):
    return x * lax.rsqrt(jnp.mean(x * x, axis=-1, keepdims=True) + NORM_EPS) * g


def _iota(shape, dim):
    return lax.broadcasted_iota(jnp.int32, shape, dim)


def _div_pow2(x, n):
    assert n & (n - 1) == 0
    return lax.shift_right_logical(x, n.bit_length() - 1)


def _mod_pow2(x, n):
    assert n & (n - 1) == 0
    return x & (n - 1)


def _resident(shape):
    nd = len(shape)
    return pl.BlockSpec(shape, lambda *_: (0,) * nd, pipeline_mode=pl.Buffered(1))


def _params(sem):
    return pltpu.CompilerParams(dimension_semantics=sem, vmem_limit_bytes=VMEM_LIMIT)


def _row_tile(n):
    for t in (256, 128, 64, 32, 16, 8):
        if n % t == 0:
            return t
    raise ValueError(f"row count {n} is not a multiple of 8")


def _ffn_kernel(x_ref, g_ref, wg_ref, wu_ref, wd_ref, o_ref):
    x = x_ref[...]
    h = _rms(x, g_ref[...]).astype(BF16)
    gate = jnp.dot(h, wg_ref[...], preferred_element_type=F32)
    up = jnp.dot(h, wu_ref[...], preferred_element_type=F32)
    act = (gate * _sigmoid(gate) * up).astype(BF16)
    o_ref[...] = x + 0.5 * jnp.dot(act, wd_ref[...], preferred_element_type=F32)


def _ffn(x, g, wg, wu, wd):
    n, d = x.shape
    dff = wg.shape[1]
    tm = _row_tile(n)
    return pl.pallas_call(
        _ffn_kernel,
        grid=(n // tm,),
        in_specs=[pl.BlockSpec((tm, d), lambda i: (i, 0)), _resident((1, d)),
                  _resident((d, dff)), _resident((d, dff)), _resident((dff, d))],
        out_specs=pl.BlockSpec((tm, d), lambda i: (i, 0)),
        out_shape=jax.ShapeDtypeStruct((n, d), F32),
        compiler_params=_params(("arbitrary",)),
        name="ffn",
    )(x, g, wg, wu, wd)


def _head_sum(x, bd_ref, pieces):
    return _mm(x, bd_ref[...], NN, pa=pieces, pb=1)


def _mixproj_kernel(x_ref, g_ref, w_ref, wf_ref, bf_ref, qn_ref, kn_ref, bd_ref,
                    pa_ref, q_ref, k_ref, v_ref, ga_ref, gb_ref, lf_ref, cum_ref, carry_ref,
                    *, n_shift, d_b, d_model, n_heads, seq_len, tm):
    i = pl.program_id(0)
    h = _rms(x_ref[...], g_ref[...]).astype(BF16)
    c1 = n_shift
    c2 = c1 + d_b
    c3 = c2 + d_b
    c4 = c3 + d_b
    c5 = c4 + d_model
    c6 = c5 + d_model
    pa_ref[...] = jnp.dot(h, w_ref[:, 0:c1], preferred_element_type=F32)
    q = jnp.dot(h, w_ref[:, c1:c2], preferred_element_type=F32)
    k = jnp.dot(h, w_ref[:, c2:c3], preferred_element_type=F32)
    v_ref[...] = jnp.dot(h, w_ref[:, c3:c4], preferred_element_type=F32)
    ga_ref[...] = jnp.dot(h, w_ref[:, c4:c5], preferred_element_type=F32)
    gb_ref[...] = jnp.dot(h, w_ref[:, c5:c6], preferred_element_type=F32)
    inv_hd = 1.0 / HEAD_DIM
    q = q * lax.rsqrt(_head_sum(q * q, bd_ref, 2) * inv_hd + NORM_EPS) * qn_ref[...]
    k = k * lax.rsqrt(_head_sum(k * k, bd_ref, 2) * inv_hd + NORM_EPS) * kn_ref[...]
    q_ref[...] = q * (1.0 / float(HEAD_DIM) ** 0.5)
    k_ref[...] = k
    z = jnp.dot(h, wf_ref[...], preferred_element_type=F32) + bf_ref[...]
    logf = jnp.minimum(z, 0.0) - jnp.log1p(jnp.exp(-jnp.abs(z)))
    logf = jnp.where(_iota(logf.shape, 1) < n_heads, logf, 0.0)
    row = _iota((tm, tm), 0)
    col = _iota((tm, tm), 1)
    keep = col <= row
    if seq_len < tm:
        keep = keep & (_div_pow2(row, seq_len) == _div_pow2(col, seq_len))
    tri = jnp.where(keep, 1.0, 0.0).astype(BF16)
    cum = _mm(tri, logf, NN, pa=1, pb=3)
    if seq_len > tm:
        tiles_per_seq = seq_len // tm

        @pl.when(i % tiles_per_seq == 0)
        def _():
            carry_ref[...] = jnp.zeros_like(carry_ref)

        cum = cum + carry_ref[...]
        carry_ref[...] = cum[tm - 1:tm, :]
    lf_ref[...] = logf[:, :n_heads]
    cum_ref[...] = cum[:, :n_heads]


def _mixproj(x, g, w_main, w_f, b_f, qn, kn, bd, *, n_shift, d_b, n_heads, seq_len):
    n, d = x.shape
    tm = _row_tile(n)
    assert seq_len % tm == 0 or tm % seq_len == 0
    row = lambda w: pl.BlockSpec((tm, w), lambda i: (i, 0))
    shp = lambda w: jax.ShapeDtypeStruct((n, w), F32)
    kern = functools.partial(_mixproj_kernel, n_shift=n_shift, d_b=d_b, d_model=d, n_heads=n_heads,
                             seq_len=seq_len, tm=tm)
    return pl.pallas_call(
        kern,
        grid=(n // tm,),
        in_specs=[row(d), _resident((1, d)), _resident(w_main.shape), _resident(w_f.shape),
                  _resident((1, LANES)), _resident((1, d_b)), _resident((1, d_b)), _resident(bd.shape)],
        out_specs=[row(n_shift), row(d_b), row(d_b), row(d_b), row(d), row(d), row(n_heads), row(n_heads)],
        out_shape=[shp(n_shift), shp(d_b), shp(d_b), shp(d_b), shp(d), shp(d), shp(n_heads), shp(n_heads)],
        scratch_shapes=[pltpu.VMEM((1, LANES), F32)],
        compiler_params=_params(("arbitrary",)),
        name="mixproj",
    )(x, g, w_main, w_f, b_f, qn, kn, bd)


def _rprep_kernel(p_ref, init_ref, mu_ref, w0_ref, a0_ref, lora_ref, g2_ref, kk_ref, ka_ref, bd_ref,
                  r_ref, k_ref, v_ref, kap_ref, b_ref, lw_ref, g_ref, last_ref, *, d_a, seq_len, tm):
    i = pl.program_id(0)
    p = p_ref[...]
    prev = pltpu.roll(p, 1, 0)
    rowi = _iota(p.shape, 0)
    if seq_len >= tm:
        tiles_per_seq = seq_len // tm

        @pl.when(i % tiles_per_seq == 0)
        def _():
            last_ref[...] = init_ref[...]

        prev = jnp.where(rowi == 0, last_ref[...], prev)
        last_ref[...] = p[tm - 1:tm, :]
    else:
        prev = jnp.where(_mod_pow2(rowi, seq_len) == 0, init_ref[...], prev)
    xm = p + (prev - p) * mu_ref[...]
    r = xm[:, 0:d_a]
    k = xm[:, d_a:2 * d_a]
    v = xm[:, 2 * d_a:3 * d_a]
    xwa = xm[:, 3 * d_a:3 * d_a + LANES]
    xg = xm[:, 3 * d_a + LANES:3 * d_a + 2 * LANES]
    xwa = jnp.where(_iota(xwa.shape, 1) < HEAD_DIM, jnp.tanh(xwa), xwa)
    lo = _mm(xwa, lora_ref[...], NN, pa=2, pb=2)
    w_log = -_softplus(-(w0_ref[...] + lo[:, 0:d_a])) - 0.5
    lw_ref[...] = -jnp.exp(w_log)
    a = _sigmoid(a0_ref[...] + lo[:, d_a:2 * d_a])
    g_ref[...] = _mm(_sigmoid(xg), g2_ref[...], NN, pa=2, pb=2)
    kk = k * kk_ref[...]
    ss = _head_sum(kk * kk, bd_ref, 3)
    kk = kk / jnp.maximum(jnp.sqrt(ss), 1e-12)
    r_ref[...] = r
    k_ref[...] = k * (1.0 + (a - 1.0) * ka_ref[...])
    v_ref[...] = v
    kap_ref[...] = kk
    b_ref[...] = kk * a


def _rprep(p_a, init_rows, mu, w0, a0, lora, g2, k_k, k_a, bd, *, d_a, seq_len):
    n, ns = p_a.shape
    tm = _row_tile(n)
    row = lambda w: pl.BlockSpec((tm, w), lambda i: (i, 0))
    shp = jax.ShapeDtypeStruct((n, d_a), F32)
    if seq_len >= tm:
        tps = seq_len // tm
        init_spec = pl.BlockSpec((None, 1, ns), lambda i: (i // tps, 0, 0))
        init_rows = init_rows.reshape(-1, 1, ns)
    else:
        init_spec = row(ns)
    kern = functools.partial(_rprep_kernel, d_a=d_a, seq_len=seq_len, tm=tm)
    return pl.pallas_call(
        kern,
        grid=(n // tm,),
        in_specs=[row(ns), init_spec, _resident((1, ns)), _resident((1, d_a)), _resident((1, d_a)),
                  _resident(lora.shape), _resident(g2.shape), _resident((1, d_a)), _resident((1, d_a)),
                  _resident(bd.shape)],
        out_specs=[row(d_a)] * 7,
        out_shape=[shp] * 7,
        scratch_shapes=[pltpu.VMEM((1, ns), F32)],
        compiler_params=_params(("arbitrary",)),
        name="rprep",
    )(p_a, init_rows, mu, w0, a0, lora, g2, k_k, k_a, bd)


def _stack(x, m0):
    return jnp.concatenate([jnp.where(m0, x, 0.0), jnp.where(m0, 0.0, x)], axis=0)


def _rscan_kernel(r_ref, k_ref, v_ref, kap_ref, b_ref, lw_ref, s0_ref, y_ref, st_ref, s_scr, *, n_chunks, hp):
    c = CHUNK
    c2 = PAIR * c
    blk = pl.program_id(2)

    @pl.when(blk == 0)
    def _():
        s_scr[...] = s0_ref[...]

    m0 = _iota((c, LANES), 1) < HEAD_DIM
    ri = _iota((c2, c2), 0)
    ci = _iota((c2, c2), 1)
    same = _div_pow2(ri, c) == _div_pow2(ci, c)
    strict = same & (ci < ri)
    incl = same & (ci <= ri)
    eye = ri == ci
    tri_c = jnp.where(_iota((c, c), 1) <= _iota((c, c), 0), 1.0, 0.0).astype(BF16)

    def chunk_body(j, carry):
        sl = pl.ds(pl.multiple_of(j * c, c), c)
        lw = lw_ref[sl, :]
        cum = _mm(tri_c, lw, NN, pa=1, pb=3)
        cum_end = cum[c - 1:c, :]
        e_neg = jnp.exp(-cum)
        e_rem = jnp.exp(cum_end - cum)
        kap = kap_ref[sl, :]
        b = b_ref[sl, :]
        k = k_ref[sl, :]
        a_t = _stack(-kap * jnp.exp(cum - lw), m0)
        r_t = _stack(r_ref[sl, :] * jnp.exp(cum), m0)
        b_t = _stack(b * e_neg, m0)
        k_t = _stack(k * e_neg, m0)
        b_p = _stack(b * e_rem, m0)
        k_p = _stack(k * e_rem, m0)
        v_s = _stack(v_ref[sl, :], m0)

        l_ab = jnp.where(strict, _mm(a_t, b_t, NT, hp, hp), 0.0)
        a_ak = jnp.where(strict, _mm(a_t, k_t, NT, hp, hp), 0.0)
        a_rb = jnp.where(incl, _mm(r_t, b_t, NT, hp, hp), 0.0)
        a_rk = jnp.where(incl, _mm(r_t, k_t, NT, hp, hp), 0.0)

        t_inv = jnp.where(eye, 1.0, 0.0) + l_ab
        l_pow = l_ab
        n_sq = c.bit_length() - 2
        for _ in range(n_sq):
            l_pow = _mm(l_pow, l_pow, NN, hp, hp)
            t_inv = t_inv + _mm(t_inv, l_pow, NN, hp, hp)

        a_p = _mm(t_inv, a_t, NN, hp, hp)
        u0 = _mm(t_inv, _mm(a_ak, v_s, NN, hp, hp), NN, hp, hp)
        r_p = r_t + _mm(a_rb, a_p, NN, hp, hp)
        y0 = _mm(a_rk, v_s, NN, hp, hp) + _mm(a_rb, u0, NN, hp, hp)
        m_c = jnp.where(eye, jnp.exp(cum_end), 0.0) + _mm(a_p, b_p, TN, hp, hp)
        n_c = _mm(u0, b_p, TN, hp, hp) + _mm(v_s, k_p, TN, hp, hp)

        s = s_scr[...]
        y_st = _mm(r_p, s, NT, hp, hp) + y0
        y_ref[sl, :] = y_st[0:c, :] + y_st[c:c2, :]
        s_scr[...] = _mm(s, m_c, NN, hp, hp) + n_c
        return carry

    lax.fori_loop(0, n_chunks, chunk_body, 0, unroll=2 if n_chunks % 2 == 0 else 1)

    @pl.when(blk == pl.num_programs(2) - 1)
    def _():
        st_ref[...] = s_scr[...]


def _rscan(r, k, v, kap, b, lw, s0, *, hp=2):
    bsz, t, d_a = r.shape
    npair = d_a // LANES
    n_chunks = min(8, t // CHUNK)
    tg = n_chunks * CHUNK
    assert t % tg == 0
    seq = pl.BlockSpec((None, tg, LANES), lambda bi, j, ci: (bi, ci, j))
    st = pl.BlockSpec((None, None, LANES, LANES), lambda bi, j, ci: (bi, j, 0, 0))
    kern = functools.partial(_rscan_kernel, n_chunks=n_chunks, hp=hp)
    return pl.pallas_call(
        kern,
        grid=(bsz, npair, t // tg),
        in_specs=[seq] * 6 + [st],
        out_specs=[seq, st],
        out_shape=[jax.ShapeDtypeStruct((bsz, t, d_a), F32),
                   jax.ShapeDtypeStruct((bsz, npair, LANES, LANES), F32)],
        scratch_shapes=[pltpu.VMEM((LANES, LANES), F32)],
        compiler_params=_params(("arbitrary", "arbitrary", "arbitrary")),
        name="rscan",
    )(r, k, v, kap, b, lw, s0)


def _pair_state(s):
    bsz, h, dv, dk = s.shape
    s = s.reshape(bsz, h // PAIR, PAIR, dv, dk)
    eye = jnp.eye(PAIR, dtype=s.dtype)
    out = jnp.einsum("bjpvk,pq->bjpvqk", s, eye)
    return out.reshape(bsz, h // PAIR, PAIR * dv, PAIR * dk)


def _unpair_state(sp, h):
    bsz, npair = sp.shape[:2]
    s = sp.reshape(bsz, npair, PAIR, HEAD_DIM, PAIR, HEAD_DIM)
    s = jnp.stack([s[:, :, p, :, p, :] for p in range(PAIR)], axis=2)
    return s.reshape(bsz, h, HEAD_DIM, HEAD_DIM)


def _lanes(x, n):
    return x if n == LANES else jnp.concatenate([x] * (n // LANES), axis=1)


def _attn_kernel(q_ref, k_ref, v_ref, cq_ref, ck_ref, o_ref, m_scr, l_scr, acc_scr, *, tq):
    qi = pl.program_id(2)
    tk = tq
    q = q_ref[...]
    m0 = _iota(q.shape, 1) < HEAD_DIM
    q_st = _stack(q, m0).astype(BF16)
    cq = jnp.broadcast_to(cq_ref[...], (2 * tq, tk))
    rows = _iota((2 * tq, tk), 0)
    cols = _iota((2 * tq, tk), 1)
    top = rows < tq
    causal = cols <= _mod_pow2(rows, tq)
    m_scr[...] = jnp.full_like(m_scr, NEG_BIG)
    l_scr[...] = jnp.zeros_like(l_scr)
    acc_scr[...] = jnp.zeros_like(acc_scr)

    def tile(ki, diagonal):
        ks = pl.ds(pl.multiple_of(ki * tk, tk), tk)
        s = lax.dot_general(q_st, k_ref[ks, :].astype(BF16), NT, preferred_element_type=F32)
        s = s + cq - jnp.where(top, ck_ref[0:1, ks], ck_ref[1:2, ks])
        if diagonal:
            s = jnp.where(causal, s, NEG_BIG)
        m_old = m_scr[...]
        m_new = jnp.maximum(m_old, jnp.max(s, axis=-1, keepdims=True))
        alpha = jnp.exp(m_old - m_new)
        p = jnp.exp(s - _lanes(m_new, tk))
        l_scr[...] = alpha * l_scr[...] + jnp.sum(p, axis=-1, keepdims=True)
        acc_scr[...] = alpha * acc_scr[...] + jnp.dot(p.astype(BF16), v_ref[ks, :].astype(BF16),
                                                      preferred_element_type=F32)
        m_scr[...] = m_new

    def off_diagonal(ki, carry):
        tile(ki, False)
        return carry

    lax.fori_loop(0, qi, off_diagonal, 0)
    tile(qi, True)
    o = acc_scr[...] / l_scr[...]
    o_ref[...] = jnp.where(m0, o[0:tq, :], o[tq:2 * tq, :])


def _attn(q, k, v, cq, ck, *, tq):
    bsz, t, d_b = q.shape
    npair = d_b // LANES
    nq = t // tq
    qspec = pl.BlockSpec((None, tq, LANES), lambda b, j, qi: (b, qi, j))
    kspec = pl.BlockSpec((None, t, LANES), lambda b, j, qi: (b, 0, j))
    cqspec = pl.BlockSpec((None, None, None, 2 * tq, 1), lambda b, j, qi: (b, j, qi, 0, 0))
    ckspec = pl.BlockSpec((None, None, 2, t), lambda b, j, qi: (b, j, 0, 0))
    return pl.pallas_call(
        functools.partial(_attn_kernel, tq=tq),
        grid=(bsz, npair, nq),
        in_specs=[qspec, kspec, kspec, cqspec, ckspec],
        out_specs=qspec,
        out_shape=jax.ShapeDtypeStruct((bsz, t, d_b), F32),
        scratch_shapes=[pltpu.VMEM((2 * tq, LANES), F32), pltpu.VMEM((2 * tq, LANES), F32),
                        pltpu.VMEM((2 * tq, LANES), F32)],
        compiler_params=_params(("arbitrary",) * 3),
        name="attn",
    )(q, k, v, cq, ck)


def _dattn_kernel(pt_ref, q_ref, cn_ref, *refs, n_heads, t_new, page, pg):
    del pt_ref
    kc = refs[0:pg]
    vc = refs[pg:2 * pg]
    lfs = refs[2 * pg:3 * pg]
    kn_ref, vn_ref, bn_ref, o_ref, wq_scr, m_scr, l_scr, acc_scr, suf_scr = refs[3 * pg:]
    step = pl.program_id(1)
    n_rows = n_heads * t_new

    @pl.when(step == 0)
    def _():
        q = q_ref[...]
        head = _div_pow2(_iota(q.shape, 1), HEAD_DIM)
        wq_scr[...] = jnp.concatenate([jnp.where(head == h, q, 0.0) for h in range(n_heads)],
                                      axis=0).astype(BF16)
        m_scr[...] = jnp.full_like(m_scr, NEG_BIG)
        l_scr[...] = jnp.zeros_like(l_scr)
        acc_scr[...] = jnp.zeros_like(acc_scr)
        suf_scr[...] = jnp.zeros_like(suf_scr)

    def update(s, vals_t):
        n = len(vals_t)
        m_old = m_scr[...]
        m_new = jnp.maximum(m_old, jnp.max(s, axis=-1, keepdims=True))
        alpha = jnp.exp(m_old - m_new)
        p = jnp.exp(s - _lanes(m_new, n * page))
        l_scr[...] = alpha * l_scr[...] + jnp.sum(p, axis=-1, keepdims=True)
        p = p.astype(BF16)
        pv = None
        for i in range(n):
            t = lax.dot_general(p[:, i * page:(i + 1) * page], vals_t[i].astype(BF16), NT,
                                preferred_element_type=F32)
            pv = t if pv is None else pv + t
        acc_scr[...] = _lanes(alpha, acc_scr.shape[1]) * acc_scr[...] + pv
        m_scr[...] = m_new

    wq = wq_scr[...]
    lf_all = jnp.concatenate([r[...] for r in lfs], axis=0)
    later = jnp.where(_iota((page, page), 0) > _iota((page, page), 1), 1.0, 0.0).astype(BF16)
    within = _mm(lf_all, later, NN, pa=3, pb=1)
    carry = suf_scr[...]
    cn = jnp.broadcast_to(cn_ref[...], (n_rows, page))
    parts = []
    for i in range(pg):
        suf = within[i * n_heads:(i + 1) * n_heads, :] + carry
        carry = carry + jnp.sum(lfs[i][...], axis=-1, keepdims=True)
        bias = jnp.broadcast_to(suf[:, None, :], (n_heads, t_new, page)).reshape(n_rows, page)
        parts.append(jnp.dot(wq, kc[i][...].astype(BF16), preferred_element_type=F32) + bias + cn)
    suf_scr[...] = carry
    update(jnp.concatenate(parts, axis=1), [r[...] for r in vc])

    @pl.when(step == pl.num_programs(1) - 1)
    def _():
        s_new = jnp.dot(wq, kn_ref[...].astype(BF16), preferred_element_type=F32)
        update(s_new + bn_ref[...], [vn_ref[...]])
        o = acc_scr[...] / _lanes(l_scr[...], acc_scr.shape[1])
        head = _div_pow2(_iota((t_new, o.shape[1]), 1), HEAD_DIM)
        out = jnp.zeros((t_new, o.shape[1]), F32)
        for h in range(n_heads):
            out = out + jnp.where(head == h, o[h * t_new:(h + 1) * t_new, :], 0.0)
        o_ref[...] = out


def _dattn(page_table, q, cn_col, cache_kt, cache_vt, cache_lf_t, kt_new, vt_new, bias_new, *, n_heads, t_new, pg):
    bsz, n_pages = page_table.shape
    _, d_b, page = cache_kt.shape
    n_rows = n_heads * t_new
    assert n_pages % pg == 0
    per_b = lambda *blk: pl.BlockSpec((None,) + blk, lambda b, s, pt: (b,) + (0,) * len(blk))

    def paged(i, *blk):
        return pl.BlockSpec((None,) + blk,
                            lambda b, s, pt: (pt[b, n_pages - 1 - (s * pg + i)],) + (0,) * len(blk))

    grid_spec = pltpu.PrefetchScalarGridSpec(
        num_scalar_prefetch=1,
        grid=(bsz, n_pages // pg),
        in_specs=([per_b(t_new, d_b), per_b(n_rows, 1)]
                  + [paged(i, d_b, page) for i in range(pg)] + [paged(i, d_b, page) for i in range(pg)]
                  + [paged(i, n_heads, page) for i in range(pg)]
                  + [per_b(d_b, page), per_b(d_b, page), per_b(n_rows, page)]),
        out_specs=per_b(t_new, d_b),
        scratch_shapes=[pltpu.VMEM((n_rows, d_b), BF16), pltpu.VMEM((n_rows, LANES), F32),
                        pltpu.VMEM((n_rows, LANES), F32), pltpu.VMEM((n_rows, d_b), F32),
                        pltpu.VMEM((n_heads, 1), F32)],
    )
    kern = functools.partial(_dattn_kernel, n_heads=n_heads, t_new=t_new, page=page, pg=pg)
    return pl.pallas_call(
        kern,
        grid_spec=grid_spec,
        out_shape=jax.ShapeDtypeStruct((bsz, t_new, d_b), F32),
        compiler_params=_params(("arbitrary", "arbitrary")),
        name="dattn",
    )(page_table, q, cn_col, *([cache_kt] * pg), *([cache_vt] * pg), *([cache_lf_t] * pg),
      kt_new, vt_new, bias_new)


def _merge_kernel(x_ref, y_ref, r_ref, k_ref, v_ref, g_ref, yb_ref, ga_ref, gb_ref,
                  lnw_ref, lnb_ref, rk_ref, bd_ref, woa_ref, wob_ref, wout_ref, o_ref):
    inv_hd = 1.0 / HEAD_DIM
    y = y_ref[...]
    mean = _head_sum(y, bd_ref, 3) * inv_hd
    yc = y - mean
    var = _head_sum(yc * yc, bd_ref, 3) * inv_hd
    yn = yc * lax.rsqrt(var + LNX_EPS) * lnw_ref[...] + lnb_ref[...]
    v = v_ref[...]
    bonus = _head_sum(r_ref[...] * k_ref[...] * rk_ref[...], bd_ref, 3) * v
    ya = ((yn + bonus) * g_ref[...]).astype(BF16)
    pa = jnp.dot(ya, woa_ref[...], preferred_element_type=F32)
    pb = jnp.dot(yb_ref[...].astype(BF16), wob_ref[...], preferred_element_type=F32)
    merged = _sigmoid(ga_ref[...]) * pa + _sigmoid(gb_ref[...]) * pb
    o_ref[...] = x_ref[...] + jnp.dot(merged.astype(BF16), wout_ref[...], preferred_element_type=F32)


def _merge(x, y, r, k, v, g, yb, ga, gb, lnw, lnb, rk, bd, woa, wob, wout):
    n, d = x.shape
    d_a = y.shape[1]
    tm = _row_tile(n)
    row = lambda w: pl.BlockSpec((tm, w), lambda i: (i, 0))
    return pl.pallas_call(
        _merge_kernel,
        grid=(n // tm,),
        in_specs=[row(d)] + [row(d_a)] * 6 + [row(d), row(d)] + [_resident((1, d_a))] * 3
                 + [_resident(bd.shape), _resident(woa.shape), _resident(wob.shape), _resident(wout.shape)],
        out_specs=row(d),
        out_shape=jax.ShapeDtypeStruct((n, d), F32),
        compiler_params=_params(("arbitrary",)),
        name="merge",
    )(x, y, r, k, v, g, yb, ga, gb, lnw, lnb, rk, bd, woa, wob, wout)


def _prep_weights(lw, n_heads_a, n_heads_b):
    (ffn1_norm, ffn1_wg, ffn1_wu, ffn1_wd, mix_norm, w_in, shift_mu, w0, w2, a0, a2, g2, k_k, k_a, r_k,
     lnx_w, lnx_b, b_f, q_norm, k_norm, w_o_a, w_o_b, w_out, ffn2_norm, ffn2_wg, ffn2_wu, ffn2_wd) = lw
    d = w_in.shape[0]
    d_a = n_heads_a * HEAD_DIM
    d_b = n_heads_b * HEAD_DIM
    n_shift = shift_mu.shape[0]
    c4 = n_shift + 3 * d_b
    row = lambda t: t.reshape(1, -1)
    w_main = jnp.concatenate([w_in[:, :c4], w_in[:, c4 + n_heads_b:]], axis=1).astype(BF16)
    w_f = jnp.pad(w_in[:, c4:c4 + n_heads_b], ((0, 0), (0, LANES - n_heads_b))).astype(BF16)
    dl, al = w2.shape[0], a2.shape[0]
    assert dl == HEAD_DIM and al == HEAD_DIM and g2.shape[0] == LANES
    lora = jnp.zeros((LANES, 2 * d_a), F32).at[:dl, :d_a].set(w2).at[dl:, d_a:].set(a2)
    bd = jnp.kron(jnp.eye(max(d_a, d_b) // HEAD_DIM, dtype=F32), jnp.ones((HEAD_DIM, HEAD_DIM), F32)).astype(BF16)
    return dict(
        d=d, d_a=d_a, d_b=d_b, n_shift=n_shift, n_heads_a=n_heads_a, n_heads_b=n_heads_b,
        ffn1=(row(ffn1_norm), ffn1_wg.astype(BF16), ffn1_wu.astype(BF16), ffn1_wd.astype(BF16)),
        ffn2=(row(ffn2_norm), ffn2_wg.astype(BF16), ffn2_wu.astype(BF16), ffn2_wd.astype(BF16)),
        mix_norm=row(mix_norm), w_main=w_main, w_f=w_f,
        b_f=jnp.pad(b_f, (0, LANES - n_heads_b)).reshape(1, LANES),
        q_norm=row(jnp.tile(q_norm, n_heads_b)), k_norm=row(jnp.tile(k_norm, n_heads_b)), bd=bd,
        mu=row(shift_mu), w0=row(w0), a0=row(a0), lora=lora, g2=g2, k_k=row(k_k), k_a=row(k_a),
        r_k=row(r_k), lnx_w=row(lnx_w), lnx_b=row(lnx_b),
        w_o_a=w_o_a.astype(BF16), w_o_b=w_o_b.astype(BF16), w_out=w_out.astype(BF16),
    )


def _layer(x3, prev_shift, s0, past, w):
    bsz, t, d = x3.shape
    d_a, d_b, ha, hb = w["d_a"], w["d_b"], w["n_heads_a"], w["n_heads_b"]
    n = bsz * t
    x = x3.reshape(n, d)
    x1 = _ffn(x, *w["ffn1"])
    p_a, q, k, v, g_a, g_b, logf, cum = _mixproj(
        x1, w["mix_norm"], w["w_main"], w["w_f"], w["b_f"], w["q_norm"], w["k_norm"], w["bd"],
        n_shift=w["n_shift"], d_b=d_b, n_heads=hb, seq_len=t)

    tm = _row_tile(n)
    init_rows = prev_shift if t >= tm else jnp.repeat(prev_shift, t, axis=0)
    r, km, va, kap, b, lw, g = _rprep(p_a, init_rows, w["mu"], w["w0"], w["a0"], w["lora"], w["g2"],
                                      w["k_k"], w["k_a"], w["bd"], d_a=d_a, seq_len=t)
    t_pad = -(-t // CHUNK) * CHUNK
    seq = lambda a: jnp.pad(a.reshape(bsz, t, d_a), ((0, 0), (0, t_pad - t), (0, 0)))
    y_raw, s_fin = _rscan(seq(r), seq(km), seq(va), seq(kap), seq(b), seq(lw), _pair_state(s0))
    y_raw = y_raw[:, :t].reshape(n, d_a)
    new_s = _unpair_state(s_fin, ha)
    new_shift = p_a.reshape(bsz, t, -1)[:, -1]

    npair = d_b // LANES
    if past is None:
        tq = min(512, t)
        cq = cum.reshape(bsz, t // tq, tq, npair, PAIR).transpose(0, 3, 1, 4, 2).reshape(bsz, npair, t // tq, PAIR * tq, 1)
        ck = cum.reshape(bsz, t, npair, PAIR).transpose(0, 2, 3, 1)
        y_b = _attn(q.reshape(bsz, t, d_b), k.reshape(bsz, t, d_b), v.reshape(bsz, t, d_b), cq, ck, tq=tq)
    else:
        cache_k, cache_v, cache_lf, page_table = past
        n_pool, page = cache_k.shape[:2]
        cn = cum.reshape(bsz, t, hb)
        cn_col = cn.transpose(0, 2, 1).reshape(bsz, hb * t, 1)
        tpos = jnp.arange(t)
        ok = tpos[None, :] <= tpos[:, None]
        bn = cn.transpose(0, 2, 1)[:, :, :, None] - cn.transpose(0, 2, 1)[:, :, None, :]
        bn = jnp.where(ok[None, None], bn, NEG_BIG).reshape(bsz, hb * t, t)
        bn = jnp.pad(bn, ((0, 0), (0, 0), (0, page - t)), constant_values=NEG_BIG)
        pages_t = lambda c: c.transpose(0, 2, 3, 1).reshape(n_pool, d_b, page)
        new_t = lambda a: jnp.pad(a.reshape(bsz, t, d_b).transpose(0, 2, 1), ((0, 0), (0, 0), (0, page - t)))
        pg = 8 if page_table.shape[1] % 8 == 0 else 1
        y_b = _dattn(page_table, q.reshape(bsz, t, d_b), cn_col, pages_t(cache_k), pages_t(cache_v),
                     cache_lf.transpose(0, 2, 1), new_t(k), new_t(v), bn, n_heads=hb, t_new=t, pg=pg)
    y_b = y_b.reshape(n, d_b)

    x2 = _merge(x1, y_raw, r, km, va, g, y_b, g_a, g_b, w["lnx_w"], w["lnx_b"], w["r_k"], w["bd"],
                w["w_o_a"], w["w_o_b"], w["w_out"])
    x3o = _ffn(x2, *w["ffn2"]).reshape(bsz, t, d)
    return x3o, (k.reshape(bsz, t, hb, HEAD_DIM), v.reshape(bsz, t, hb, HEAD_DIM),
                 logf.reshape(bsz, t, hb), new_s, new_shift)


def kernel(x_prompt, x_sample, cache_k, cache_v, cache_logf, state_wkv, state_shift, page_table,
           ffn1_norm, ffn1_wg, ffn1_wu, ffn1_wd, mix_norm, w_in, shift_mu,
           rwkv_w0, rwkv_w2, rwkv_a0, rwkv_a2, rwkv_g2, rwkv_k_k, rwkv_k_a, rwkv_r_k, rwkv_lnx_w, rwkv_lnx_b,
           fox_b_f, fox_q_norm, fox_k_norm, w_o_a, w_o_b, w_out,
           ffn2_norm, ffn2_wg, ffn2_wu, ffn2_wd):
    depth = w_in.shape[0]
    n_heads_a = state_wkv.shape[2]
    n_heads_b = cache_k.shape[3]
    bp = x_prompt.shape[0]
    xp, xs = x_prompt, x_sample
    outs_p, outs_s = [], []
    for l in range(depth):
        lw = (ffn1_norm[l], ffn1_wg[l], ffn1_wu[l], ffn1_wd[l], mix_norm[l], w_in[l], shift_mu[l],
              rwkv_w0[l], rwkv_w2[l], rwkv_a0[l], rwkv_a2[l], rwkv_g2[l], rwkv_k_k[l], rwkv_k_a[l],
              rwkv_r_k[l].reshape(-1), rwkv_lnx_w[l], rwkv_lnx_b[l], fox_b_f[l], fox_q_norm[l], fox_k_norm[l],
              w_o_a[l], w_o_b[l], w_out[l], ffn2_norm[l], ffn2_wg[l], ffn2_wu[l], ffn2_wd[l])
        w = _prep_weights(lw, n_heads_a, n_heads_b)
        prev0 = jnp.zeros((bp, w["n_shift"]), xp.dtype)
        s00 = jnp.zeros((bp, n_heads_a, HEAD_DIM, HEAD_DIM), xp.dtype)
        xp, op = _layer(xp, prev0, s00, None, w)
        xs, os_ = _layer(xs, state_shift[l], state_wkv[l], (cache_k[l], cache_v[l], cache_logf[l], page_table), w)
        outs_p.append(op)
        outs_s.append(os_)
    stack = lambda outs, i: jnp.stack([o[i] for o in outs], 0)
    return ((xp, xs) + tuple(stack(outs_p, i) for i in range(5)) + tuple(stack(outs_s, i) for i in range(5)))
```

```python
import functools

import jax
import jax.numpy as jnp
from jax import lax
from jax.experimental import pallas as pl
from jax.experimental.pallas import tpu as pltpu

F32 = jnp.float32
BF16 = jnp.bfloat16

HEAD_DIM = 64
LANES = 128
PAIR = LANES // HEAD_DIM
NORM_EPS = 1e-6
LNX_EPS = 64e-5
NEG_BIG = -1e30
LOG2E = 1.4426950408889634
CHUNK = 64
VMEM_LIMIT = 56 * 1024 * 1024

NN = (((1,), (0,)), ((), ()))
NT = (((1,), (1,)), ((), ()))
TN = (((0,), (0,)), ((), ()))


def _pieces(x, n):
    if x.dtype == BF16:
        return [x]
    out = []
    rem = x
    for i in range(n):
        p = rem.astype(BF16)
        out.append(p)
        if i + 1 < n:
            rem = rem - p.astype(F32)
    return out


def _mm(a, b, dims=NN, pa=1, pb=1):
    pa_l = _pieces(a, pa)
    pb_l = _pieces(b, pb)
    order = max(len(pa_l), len(pb_l))
    acc = None
    for i in reversed(range(len(pa_l))):
        for j in reversed(range(len(pb_l))):
            if i + j >= order:
                continue
            t = lax.dot_general(pa_l[i], pb_l[j], dims, preferred_element_type=F32)
            acc = t if acc is None else acc + t
    return acc


def _sigmoid(x):
    return 1.0 / (1.0 + jnp.exp(-x))


def _softplus(x):
    return jnp.maximum(x, 0.0) + jnp.log1p(jnp.exp(-jnp.abs(x)))


def _rms(x, g):
    return x * lax.rsqrt(jnp.mean(x * x, axis=-1, keepdims=True) + NORM_EPS) * g


def _iota(shape, dim):
    return lax.broadcasted_iota(jnp.int32, shape, dim)


def _div_pow2(x, n):
    assert n & (n - 1) == 0
    return lax.shift_right_logical(x, n.bit_length() - 1)


def _mod_pow2(x, n):
    assert n & (n - 1) == 0
    return x & (n - 1)


def _resident(shape):
    nd = len(shape)
    return pl.BlockSpec(shape, lambda *_: (0,) * nd, pipeline_mode=pl.Buffered(1))


def _params(sem):
    return pltpu.CompilerParams(dimension_semantics=sem, vmem_limit_bytes=VMEM_LIMIT)


def _row_tile(n):
    for t in (256, 128, 64, 32, 16, 8):
        if n % t == 0:
            return t
    raise ValueError(f"row count {n} is not a multiple of 8")


def _ffn_kernel(x_ref, g_ref, wg_ref, wu_ref, wd_ref, o_ref):
    x = x_ref[...]
    h = _rms(x, g_ref[...]).astype(BF16)
    gate = jnp.dot(h, wg_ref[...], preferred_element_type=F32)
    up = jnp.dot(h, wu_ref[...], preferred_element_type=F32)
    act = (gate * _sigmoid(gate) * up).astype(BF16)
    o_ref[...] = x + 0.5 * jnp.dot(act, wd_ref[...], preferred_element_type=F32)


def _ffn(x, g, wg, wu, wd):
    n, d = x.shape
    dff = wg.shape[1]
    tm = _row_tile(n)
    return pl.pallas_call(
        _ffn_kernel,
        grid=(n // tm,),
        in_specs=[pl.BlockSpec((tm, d), lambda i: (i, 0)), _resident((1, d)),
                  _resident((d, dff)), _resident((d, dff)), _resident((dff, d))],
        out_specs=pl.BlockSpec((tm, d), lambda i: (i, 0)),
        out_shape=jax.ShapeDtypeStruct((n, d), F32),
        compiler_params=_params(("arbitrary",)),
        name="ffn",
    )(x, g, wg, wu, wd)


def _head_sum(x, bd_ref, pieces):
    return _mm(x, bd_ref[...], NN, pa=pieces, pb=1)


def _mixproj_kernel(x_ref, g_ref, w_ref, wf_ref, bf_ref, qn_ref, kn_ref, bd_ref,
                    pa_ref, q_ref, k_ref, v_ref, ga_ref, gb_ref, lf_ref, cum_ref, carry_ref,
                    *, n_shift, d_b, d_model, n_heads, seq_len, tm, q_scale):
    i = pl.program_id(0)
    h = _rms(x_ref[...], g_ref[...]).astype(BF16)
    c1 = n_shift
    c2 = c1 + d_b
    c3 = c2 + d_b
    c4 = c3 + d_b
    c5 = c4 + d_model
    c6 = c5 + d_model
    pa_ref[...] = jnp.dot(h, w_ref[:, 0:c1], preferred_element_type=F32)
    q = jnp.dot(h, w_ref[:, c1:c2], preferred_element_type=F32)
    k = jnp.dot(h, w_ref[:, c2:c3], preferred_element_type=F32)
    v_ref[...] = jnp.dot(h, w_ref[:, c3:c4], preferred_element_type=F32)
    ga_ref[...] = jnp.dot(h, w_ref[:, c4:c5], preferred_element_type=F32)
    gb_ref[...] = jnp.dot(h, w_ref[:, c5:c6], preferred_element_type=F32)
    inv_hd = 1.0 / HEAD_DIM
    q = q * lax.rsqrt(_head_sum(q * q, bd_ref, 2) * inv_hd + NORM_EPS) * qn_ref[...]
    k = k * lax.rsqrt(_head_sum(k * k, bd_ref, 2) * inv_hd + NORM_EPS) * kn_ref[...]
    q_ref[...] = q * q_scale
    k_ref[...] = k
    z = jnp.dot(h, wf_ref[...], preferred_element_type=F32) + bf_ref[...]
    logf = jnp.minimum(z, 0.0) - jnp.log1p(jnp.exp(-jnp.abs(z)))
    logf = jnp.where(_iota(logf.shape, 1) < n_heads, logf, 0.0)
    row = _iota((tm, tm), 0)
    col = _iota((tm, tm), 1)
    keep = col <= row
    if seq_len < tm:
        keep = keep & (_div_pow2(row, seq_len) == _div_pow2(col, seq_len))
    tri = jnp.where(keep, 1.0, 0.0).astype(BF16)
    cum = _mm(tri, logf, NN, pa=1, pb=3)
    if seq_len > tm:
        tiles_per_seq = seq_len // tm

        @pl.when(i % tiles_per_seq == 0)
        def _():
            carry_ref[...] = jnp.zeros_like(carry_ref)

        cum = cum + carry_ref[...]
        carry_ref[...] = cum[tm - 1:tm, :]
    lf_ref[...] = logf[:, :n_heads]
    cum_ref[...] = cum[:, :n_heads]


def _mixproj(x, g, w_main, w_f, b_f, qn, kn, bd, *, n_shift, d_b, n_heads, seq_len, q_scale):
    n, d = x.shape
    tm = _row_tile(n)
    assert seq_len % tm == 0 or tm % seq_len == 0
    row = lambda w: pl.BlockSpec((tm, w), lambda i: (i, 0))
    shp = lambda w: jax.ShapeDtypeStruct((n, w), F32)
    kern = functools.partial(_mixproj_kernel, n_shift=n_shift, d_b=d_b, d_model=d, n_heads=n_heads,
                             seq_len=seq_len, tm=tm, q_scale=q_scale)
    return pl.pallas_call(
        kern,
        grid=(n // tm,),
        in_specs=[row(d), _resident((1, d)), _resident(w_main.shape), _resident(w_f.shape),
                  _resident((1, LANES)), _resident((1, d_b)), _resident((1, d_b)), _resident(bd.shape)],
        out_specs=[row(n_shift), row(d_b), row(d_b), row(d_b), row(d), row(d), row(n_heads), row(n_heads)],
        out_shape=[shp(n_shift), shp(d_b), shp(d_b), shp(d_b), shp(d), shp(d), shp(n_heads), shp(n_heads)],
        scratch_shapes=[pltpu.VMEM((1, LANES), F32)],
        compiler_params=_params(("arbitrary",)),
        name="mixproj",
    )(x, g, w_main, w_f, b_f, qn, kn, bd)


def _rprep_kernel(p_ref, init_ref, mu_ref, w0_ref, a0_ref, lora_ref, g2_ref, kk_ref, ka_ref, bd_ref,
                  r_ref, k_ref, v_ref, kap_ref, b_ref, lw_ref, g_ref, last_ref, *, d_a, seq_len, tm):
    i = pl.program_id(0)
    p = p_ref[...]
    prev = pltpu.roll(p, 1, 0)
    rowi = _iota(p.shape, 0)
    if seq_len >= tm:
        tiles_per_seq = seq_len // tm

        @pl.when(i % tiles_per_seq == 0)
        def _():
            last_ref[...] = init_ref[...]

        prev = jnp.where(rowi == 0, last_ref[...], prev)
        last_ref[...] = p[tm - 1:tm, :]
    else:
        prev = jnp.where(_mod_pow2(rowi, seq_len) == 0, init_ref[...], prev)
    xm = p + (prev - p) * mu_ref[...]
    r = xm[:, 0:d_a]
    k = xm[:, d_a:2 * d_a]
    v = xm[:, 2 * d_a:3 * d_a]
    xwa = xm[:, 3 * d_a:3 * d_a + LANES]
    xg = xm[:, 3 * d_a + LANES:3 * d_a + 2 * LANES]
    xwa = jnp.where(_iota(xwa.shape, 1) < HEAD_DIM, jnp.tanh(xwa), xwa)
    lo = _mm(xwa, lora_ref[...], NN, pa=2, pb=2)
    w_log = -_softplus(-(w0_ref[...] + lo[:, 0:d_a])) - 0.5
    lw_ref[...] = -jnp.exp(w_log)
    a = _sigmoid(a0_ref[...] + lo[:, d_a:2 * d_a])
    g_ref[...] = _mm(_sigmoid(xg), g2_ref[...], NN, pa=2, pb=2)
    kk = k * kk_ref[...]
    ss = _head_sum(kk * kk, bd_ref, 3)
    kk = kk / jnp.maximum(jnp.sqrt(ss), 1e-12)
    r_ref[...] = r
    k_ref[...] = k * (1.0 + (a - 1.0) * ka_ref[...])
    v_ref[...] = v
    kap_ref[...] = kk
    b_ref[...] = kk * a


def _rprep(p_a, init_rows, mu, w0, a0, lora, g2, k_k, k_a, bd, *, d_a, seq_len):
    n, ns = p_a.shape
    tm = _row_tile(n)
    row = lambda w: pl.BlockSpec((tm, w), lambda i: (i, 0))
    shp = jax.ShapeDtypeStruct((n, d_a), F32)
    if seq_len >= tm:
        tps = seq_len // tm
        init_spec = pl.BlockSpec((None, 1, ns), lambda i: (i // tps, 0, 0))
        init_rows = init_rows.reshape(-1, 1, ns)
    else:
        init_spec = row(ns)
    kern = functools.partial(_rprep_kernel, d_a=d_a, seq_len=seq_len, tm=tm)
    return pl.pallas_call(
        kern,
        grid=(n // tm,),
        in_specs=[row(ns), init_spec, _resident((1, ns)), _resident((1, d_a)), _resident((1, d_a)),
                  _resident(lora.shape), _resident(g2.shape), _resident((1, d_a)), _resident((1, d_a)),
                  _resident(bd.shape)],
        out_specs=[row(d_a)] * 7,
        out_shape=[shp] * 7,
        scratch_shapes=[pltpu.VMEM((1, ns), F32)],
        compiler_params=_params(("arbitrary",)),
        name="rprep",
    )(p_a, init_rows, mu, w0, a0, lora, g2, k_k, k_a, bd)


def _stack(x, m0):
    return jnp.concatenate([jnp.where(m0, x, 0.0), jnp.where(m0, 0.0, x)], axis=0)


def _rscan_kernel(r_ref, k_ref, v_ref, kap_ref, b_ref, lw_ref, s0_ref, y_ref, st_ref, s_scr, *, bb, npair, hs, hc, hy):
    c = CHUNK
    c2 = PAIR * c
    ci_ = pl.program_id(1)
    probs = [(i, j) for i in range(bb) for j in range(npair)]
    each = lambda f, *ls: [f(*xs) for xs in zip(*ls)]

    @pl.when(ci_ == 0)
    def _():
        for p, (i, j) in enumerate(probs):
            s_scr[p] = s0_ref[i, j]

    m0 = _iota((c, LANES), 1) < HEAD_DIM
    ri = _iota((c2, c2), 0)
    ci = _iota((c2, c2), 1)
    same = _div_pow2(ri, c) == _div_pow2(ci, c)
    strict = same & (ci < ri)
    incl = same & (ci <= ri)
    eye = ri == ci
    tri_c = jnp.where(_iota((c, c), 1) <= _iota((c, c), 0), 1.0, 0.0).astype(BF16)
    ld = lambda ref: [ref[i, :, j * LANES:(j + 1) * LANES] for (i, j) in probs]

    lw = ld(lw_ref)
    cum = each(lambda x: _mm(tri_c, x, NN, pa=1, pb=3), lw)
    cum_end = each(lambda x: x[c - 1:c, :], cum)
    e_neg = each(lambda x: jnp.exp(-x), cum)
    e_rem = each(lambda x, e: jnp.exp(e - x), cum, cum_end)
    kap, b, k = ld(kap_ref), ld(b_ref), ld(k_ref)
    a_t = each(lambda kp, x, w: _stack(-kp * jnp.exp(x - w), m0), kap, cum, lw)
    r_t = each(lambda r, x: _stack(r * jnp.exp(x), m0), ld(r_ref), cum)
    b_t = each(lambda x, e: _stack(x * e, m0), b, e_neg)
    k_t = each(lambda x, e: _stack(x * e, m0), k, e_neg)
    b_p = each(lambda x, e: _stack(x * e, m0), b, e_rem)
    k_p = each(lambda x, e: _stack(x * e, m0), k, e_rem)
    v_s = each(lambda x: _stack(x, m0), ld(v_ref))

    l_ab = each(lambda x, y: jnp.where(strict, _mm(x, y, NT, hs, hs), 0.0), a_t, b_t)
    a_ak = each(lambda x, y: jnp.where(strict, _mm(x, y, NT, hs, hs), 0.0), a_t, k_t)
    a_rb = each(lambda x, y: jnp.where(incl, _mm(x, y, NT, hy, hy), 0.0), r_t, b_t)
    a_rk = each(lambda x, y: jnp.where(incl, _mm(x, y, NT, hy, hy), 0.0), r_t, k_t)

    t_inv = each(lambda x: jnp.where(eye, 1.0, 0.0) + x, l_ab)
    l_pow = l_ab
    for _ in range(c.bit_length() - 2):
        l_pow = each(lambda x: _mm(x, x, NN, hc, hc), l_pow)
        t_inv = each(lambda t, x: t + _mm(t, x, NN, hc, hc), t_inv, l_pow)

    a_p = each(lambda t, x: _mm(t, x, NN, hs, hs), t_inv, a_t)
    akv = each(lambda x, y: _mm(x, y, NN, hs, hs), a_ak, v_s)
    u0 = each(lambda t, x: _mm(t, x, NN, hs, hs), t_inv, akv)
    r_p = each(lambda r, x, y: r + _mm(x, y, NN, hy, hy), r_t, a_rb, a_p)
    y0 = each(lambda x, y, z, w: _mm(x, y, NN, hy, hy) + _mm(z, w, NN, hy, hy), a_rk, v_s, a_rb, u0)
    m_c = each(lambda e, x, y: jnp.where(eye, jnp.exp(e), 0.0) + _mm(x, y, TN, hs, hs), cum_end, a_p, b_p)
    n_c = each(lambda x, y, z, w: _mm(x, y, TN, hs, hs) + _mm(z, w, TN, hs, hs), u0, b_p, v_s, k_p)

    s_in = [s_scr[p] for p in range(len(probs))]
    y_st = each(lambda r, s, y: _mm(r, s, NT, hy, hy) + y, r_p, s_in, y0)
    s_out = each(lambda s, m, n: _mm(s, m, NN, hs, hs) + n, s_in, m_c, n_c)
    for p, (i, j) in enumerate(probs):
        y_ref[i, :, j * LANES:(j + 1) * LANES] = y_st[p][0:c, :] + y_st[p][c:c2, :]
        s_scr[p] = s_out[p]

    @pl.when(ci_ == pl.num_programs(1) - 1)
    def _():
        for p, (i, j) in enumerate(probs):
            st_ref[i, j] = s_out[p]


def _rscan(r, k, v, kap, b, lw, s0, *, hs=2, hc=1, hy=1):
    bsz, t, d_a = r.shape
    npair = d_a // LANES
    bb = 2 if bsz % 2 == 0 else 1
    seq = pl.BlockSpec((bb, CHUNK, d_a), lambda g, ci: (g, ci, 0))
    st = pl.BlockSpec((bb, npair, LANES, LANES), lambda g, ci: (g, 0, 0, 0))
    kern = functools.partial(_rscan_kernel, bb=bb, npair=npair, hs=hs, hc=hc, hy=hy)
    return pl.pallas_call(
        kern,
        grid=(bsz // bb, t // CHUNK),
        in_specs=[seq] * 6 + [st],
        out_specs=[seq, st],
        out_shape=[jax.ShapeDtypeStruct((bsz, t, d_a), F32),
                   jax.ShapeDtypeStruct((bsz, npair, LANES, LANES), F32)],
        scratch_shapes=[pltpu.VMEM((bb * npair, LANES, LANES), F32)],
        compiler_params=_params(("arbitrary", "arbitrary")),
        name="rscan",
    )(r, k, v, kap, b, lw, s0)


def _pair_state(s):
    bsz, h, dv, dk = s.shape
    s = s.reshape(bsz, h // PAIR, PAIR, dv, dk)
    eye = jnp.eye(PAIR, dtype=s.dtype)
    out = jnp.einsum("bjpvk,pq->bjpvqk", s, eye)
    return out.reshape(bsz, h // PAIR, PAIR * dv, PAIR * dk)


def _unpair_state(sp, h):
    bsz, npair = sp.shape[:2]
    s = sp.reshape(bsz, npair, PAIR, HEAD_DIM, PAIR, HEAD_DIM)
    s = jnp.stack([s[:, :, p, :, p, :] for p in range(PAIR)], axis=2)
    return s.reshape(bsz, h, HEAD_DIM, HEAD_DIM)


def _lanes(x, n):
    return x if n == LANES else jnp.concatenate([x] * (n // LANES), axis=1)


def _attn_kernel(q_ref, k_ref, v_ref, kb_ref, o_ref, kaug_scr, vb_scr, s_scr, p_scr, m_scr, l_scr, acc_scr,
                 *, tq, strip):
    qi = pl.program_id(2)
    tk = tq
    rows2 = 2 * tq

    @pl.when(qi == 0)
    def _():
        kaug_scr[:, 0:LANES] = k_ref[...].astype(BF16)
        kaug_scr[:, LANES:2 * LANES] = kb_ref[...]
        vb_scr[...] = v_ref[...].astype(BF16)

    q = q_ref[...]
    m0 = _iota(q.shape, 1) < HEAD_DIM
    lane = _iota((rows2, LANES), 1)
    head1 = _iota((rows2, LANES), 0) >= tq
    pick = (lane >= jnp.where(head1, 3, 0)) & (lane < jnp.where(head1, 6, 3))
    q_aug = jnp.concatenate([_stack(q, m0), jnp.where(pick, 1.0, 0.0)], axis=1).astype(BF16)
    m_scr[...] = jnp.full_like(m_scr, NEG_BIG)
    l_scr[...] = jnp.zeros_like(l_scr)
    acc_scr[...] = jnp.zeros_like(acc_scr)

    def keys(ki):
        return pl.ds(pl.multiple_of(ki * tk, tk), tk)

    def tile(ki, diagonal):
        s_scr[...] = lax.dot_general(q_aug, kaug_scr[keys(ki), :], NT, preferred_element_type=F32)
        for r0 in range(0, rows2, strip):
            rs = slice(r0, r0 + strip)
            s = s_scr[rs, :]
            if diagonal:
                s = jnp.where(_iota((strip, tk), 1) <= _iota((strip, tk), 0) + r0 % tq, s, NEG_BIG)
            m_old = m_scr[rs, :]
            m_new = jnp.maximum(m_old, jnp.max(s, axis=-1, keepdims=True))
            alpha = jnp.exp2(m_old - m_new)
            p = jnp.exp2(s - _lanes(m_new, tk))
            l_scr[rs, :] = alpha * l_scr[rs, :] + jnp.sum(p, axis=-1, keepdims=True)
            acc_scr[rs, :] = alpha * acc_scr[rs, :]
            p_scr[rs, :] = p.astype(BF16)
            m_scr[rs, :] = m_new
        acc_scr[...] += jnp.dot(p_scr[...], vb_scr[keys(ki), :], preferred_element_type=F32)

    def off_diagonal(ki, carry):
        tile(ki, False)
        return carry

    lax.fori_loop(0, qi, off_diagonal, 0)
    tile(qi, True)
    o = acc_scr[...] / l_scr[...]
    o_ref[...] = jnp.where(m0, o[0:tq, :], o[tq:rows2, :])


def _attn(q, k, v, kb, *, tq):
    bsz, t, d_b = q.shape
    npair = d_b // LANES
    nq = t // tq
    qspec = pl.BlockSpec((None, tq, LANES), lambda b, j, qi: (b, qi, j))
    kspec = pl.BlockSpec((None, t, LANES), lambda b, j, qi: (b, 0, j))
    kbspec = pl.BlockSpec((None, None, t, LANES), lambda b, j, qi: (b, j, 0, 0))
    return pl.pallas_call(
        functools.partial(_attn_kernel, tq=tq, strip=min(64, tq)),
        grid=(bsz, npair, nq),
        in_specs=[qspec, kspec, kspec, kbspec],
        out_specs=qspec,
        out_shape=jax.ShapeDtypeStruct((bsz, t, d_b), F32),
        scratch_shapes=[pltpu.VMEM((t, 2 * LANES), BF16), pltpu.VMEM((t, LANES), BF16),
                        pltpu.VMEM((2 * tq, tq), F32), pltpu.VMEM((2 * tq, tq), BF16),
                        pltpu.VMEM((2 * tq, LANES), F32), pltpu.VMEM((2 * tq, LANES), F32),
                        pltpu.VMEM((2 * tq, LANES), F32)],
        compiler_params=_params(("arbitrary",) * 3),
        name="attn",
    )(q, k, v, kb)


def _bf16_pieces3(x):
    rp = lambda y: lax.reduce_precision(y, exponent_bits=8, mantissa_bits=7)
    p0 = rp(x)
    p1 = rp(x - p0)
    return p0, p1, rp(x - p0 - p1)


def _dattn_kernel(pt_ref, q_ref, cn_ref, *refs, n_heads, t_new, page, pg):
    del pt_ref
    kc = refs[0:pg]
    vc = refs[pg:2 * pg]
    lfs = refs[2 * pg:3 * pg]
    kn_ref, vn_ref, bn_ref, o_ref, wq_scr, m_scr, l_scr, acc_scr, suf_scr = refs[3 * pg:]
    step = pl.program_id(1)
    n_rows = n_heads * t_new

    @pl.when(step == 0)
    def _():
        q = q_ref[...]
        head = _div_pow2(_iota(q.shape, 1), HEAD_DIM)
        wq_scr[...] = jnp.concatenate([jnp.where(head == h, q, 0.0) for h in range(n_heads)],
                                      axis=0).astype(BF16)
        m_scr[...] = jnp.full_like(m_scr, NEG_BIG)
        l_scr[...] = jnp.zeros_like(l_scr)
        acc_scr[...] = jnp.zeros_like(acc_scr)
        suf_scr[...] = jnp.zeros_like(suf_scr)

    def update(s, vals_t):
        n = len(vals_t)
        m_old = m_scr[...]
        m_new = jnp.maximum(m_old, jnp.max(s, axis=-1, keepdims=True))
        alpha = jnp.exp(m_old - m_new)
        p = jnp.exp(s - _lanes(m_new, n * page))
        l_scr[...] = alpha * l_scr[...] + jnp.sum(p, axis=-1, keepdims=True)
        p = p.astype(BF16)
        pv = None
        for i in range(n):
            t = lax.dot_general(p[:, i * page:(i + 1) * page], vals_t[i].astype(BF16), NT,
                                preferred_element_type=F32)
            pv = t if pv is None else pv + t
        acc_scr[...] = _lanes(alpha, acc_scr.shape[1]) * acc_scr[...] + pv
        m_scr[...] = m_new

    wq = wq_scr[...]
    lf_all = jnp.concatenate([r[...] for r in lfs], axis=0)
    later = jnp.where(_iota((page, page), 0) > _iota((page, page), 1), 1.0, 0.0).astype(BF16)
    within = _mm(lf_all, later, NN, pa=3, pb=1)
    carry = suf_scr[...]
    cn = jnp.broadcast_to(cn_ref[...], (n_rows, page))
    parts = []
    for i in range(pg):
        suf = within[i * n_heads:(i + 1) * n_heads, :] + carry
        carry = carry + jnp.sum(lfs[i][...], axis=-1, keepdims=True)
        bias = jnp.broadcast_to(suf[:, None, :], (n_heads, t_new, page)).reshape(n_rows, page)
        parts.append(jnp.dot(wq, kc[i][...].astype(BF16), preferred_element_type=F32) + bias + cn)
    suf_scr[...] = carry
    update(jnp.concatenate(parts, axis=1), [r[...] for r in vc])

    @pl.when(step == pl.num_programs(1) - 1)
    def _():
        s_new = jnp.dot(wq, kn_ref[...].astype(BF16), preferred_element_type=F32)
        update(s_new + bn_ref[...], [vn_ref[...]])
        o = acc_scr[...] / _lanes(l_scr[...], acc_scr.shape[1])
        head = _div_pow2(_iota((t_new, o.shape[1]), 1), HEAD_DIM)
        out = jnp.zeros((t_new, o.shape[1]), F32)
        for h in range(n_heads):
            out = out + jnp.where(head == h, o[h * t_new:(h + 1) * t_new, :], 0.0)
        o_ref[...] = out


def _dattn(page_table, q, cn_col, cache_kt, cache_vt, cache_lf_t, kt_new, vt_new, bias_new, *, n_heads, t_new, pg):
    bsz, n_pages = page_table.shape
    _, d_b, page = cache_kt.shape
    n_rows = n_heads * t_new
    assert n_pages % pg == 0
    per_b = lambda *blk: pl.BlockSpec((None,) + blk, lambda b, s, pt: (b,) + (0,) * len(blk))

    def paged(i, *blk):
        return pl.BlockSpec((None,) + blk,
                            lambda b, s, pt: (pt[b, n_pages - 1 - (s * pg + i)],) + (0,) * len(blk))

    grid_spec = pltpu.PrefetchScalarGridSpec(
        num_scalar_prefetch=1,
        grid=(bsz, n_pages // pg),
        in_specs=([per_b(t_new, d_b), per_b(n_rows, 1)]
                  + [paged(i, d_b, page) for i in range(pg)] + [paged(i, d_b, page) for i in range(pg)]
                  + [paged(i, n_heads, page) for i in range(pg)]
                  + [per_b(d_b, page), per_b(d_b, page), per_b(n_rows, page)]),
        out_specs=per_b(t_new, d_b),
        scratch_shapes=[pltpu.VMEM((n_rows, d_b), BF16), pltpu.VMEM((n_rows, LANES), F32),
                        pltpu.VMEM((n_rows, LANES), F32), pltpu.VMEM((n_rows, d_b), F32),
                        pltpu.VMEM((n_heads, 1), F32)],
    )
    kern = functools.partial(_dattn_kernel, n_heads=n_heads, t_new=t_new, page=page, pg=pg)
    return pl.pallas_call(
        kern,
        grid_spec=grid_spec,
        out_shape=jax.ShapeDtypeStruct((bsz, t_new, d_b), F32),
        compiler_params=_params(("arbitrary", "arbitrary")),
        name="dattn",
    )(page_table, q, cn_col, *([cache_kt] * pg), *([cache_vt] * pg), *([cache_lf_t] * pg),
      kt_new, vt_new, bias_new)


def _merge_kernel(x_ref, y_ref, r_ref, k_ref, v_ref, g_ref, yb_ref, ga_ref, gb_ref,
                  lnw_ref, lnb_ref, rk_ref, bd_ref, woa_ref, wob_ref, wout_ref, o_ref):
    inv_hd = 1.0 / HEAD_DIM
    y = y_ref[...]
    mean = _head_sum(y, bd_ref, 3) * inv_hd
    yc = y - mean
    var = _head_sum(yc * yc, bd_ref, 3) * inv_hd
    yn = yc * lax.rsqrt(var + LNX_EPS) * lnw_ref[...] + lnb_ref[...]
    v = v_ref[...]
    bonus = _head_sum(r_ref[...] * k_ref[...] * rk_ref[...], bd_ref, 3) * v
    ya = ((yn + bonus) * g_ref[...]).astype(BF16)
    pa = jnp.dot(ya, woa_ref[...], preferred_element_type=F32)
    pb = jnp.dot(yb_ref[...].astype(BF16), wob_ref[...], preferred_element_type=F32)
    merged = _sigmoid(ga_ref[...]) * pa + _sigmoid(gb_ref[...]) * pb
    o_ref[...] = x_ref[...] + jnp.dot(merged.astype(BF16), wout_ref[...], preferred_element_type=F32)


def _merge(x, y, r, k, v, g, yb, ga, gb, lnw, lnb, rk, bd, woa, wob, wout):
    n, d = x.shape
    d_a = y.shape[1]
    tm = _row_tile(n)
    row = lambda w: pl.BlockSpec((tm, w), lambda i: (i, 0))
    return pl.pallas_call(
        _merge_kernel,
        grid=(n // tm,),
        in_specs=[row(d)] + [row(d_a)] * 6 + [row(d), row(d)] + [_resident((1, d_a))] * 3
                 + [_resident(bd.shape), _resident(woa.shape), _resident(wob.shape), _resident(wout.shape)],
        out_specs=row(d),
        out_shape=jax.ShapeDtypeStruct((n, d), F32),
        compiler_params=_params(("arbitrary",)),
        name="merge",
    )(x, y, r, k, v, g, yb, ga, gb, lnw, lnb, rk, bd, woa, wob, wout)


def _prep_weights(lw, n_heads_a, n_heads_b):
    (ffn1_norm, ffn1_wg, ffn1_wu, ffn1_wd, mix_norm, w_in, shift_mu, w0, w2, a0, a2, g2, k_k, k_a, r_k,
     lnx_w, lnx_b, b_f, q_norm, k_norm, w_o_a, w_o_b, w_out, ffn2_norm, ffn2_wg, ffn2_wu, ffn2_wd) = lw
    d = w_in.shape[0]
    d_a = n_heads_a * HEAD_DIM
    d_b = n_heads_b * HEAD_DIM
    n_shift = shift_mu.shape[0]
    c4 = n_shift + 3 * d_b
    row = lambda t: t.reshape(1, -1)
    w_main = jnp.concatenate([w_in[:, :c4], w_in[:, c4 + n_heads_b:]], axis=1).astype(BF16)
    w_f = jnp.pad(w_in[:, c4:c4 + n_heads_b], ((0, 0), (0, LANES - n_heads_b))).astype(BF16)
    dl, al = w2.shape[0], a2.shape[0]
    assert dl == HEAD_DIM and al == HEAD_DIM and g2.shape[0] == LANES
    lora = jnp.zeros((LANES, 2 * d_a), F32).at[:dl, :d_a].set(w2).at[dl:, d_a:].set(a2)
    bd = jnp.kron(jnp.eye(max(d_a, d_b) // HEAD_DIM, dtype=F32), jnp.ones((HEAD_DIM, HEAD_DIM), F32)).astype(BF16)
    return dict(
        d=d, d_a=d_a, d_b=d_b, n_shift=n_shift, n_heads_a=n_heads_a, n_heads_b=n_heads_b,
        ffn1=(row(ffn1_norm), ffn1_wg.astype(BF16), ffn1_wu.astype(BF16), ffn1_wd.astype(BF16)),
        ffn2=(row(ffn2_norm), ffn2_wg.astype(BF16), ffn2_wu.astype(BF16), ffn2_wd.astype(BF16)),
        mix_norm=row(mix_norm), w_main=w_main, w_f=w_f,
        b_f=jnp.pad(b_f, (0, LANES - n_heads_b)).reshape(1, LANES),
        q_norm=row(jnp.tile(q_norm, n_heads_b)), k_norm=row(jnp.tile(k_norm, n_heads_b)), bd=bd,
        mu=row(shift_mu), w0=row(w0), a0=row(a0), lora=lora, g2=g2, k_k=row(k_k), k_a=row(k_a),
        r_k=row(r_k), lnx_w=row(lnx_w), lnx_b=row(lnx_b),
        w_o_a=w_o_a.astype(BF16), w_o_b=w_o_b.astype(BF16), w_out=w_out.astype(BF16),
    )


def _layer(x3, prev_shift, s0, past, w):
    bsz, t, d = x3.shape
    d_a, d_b, ha, hb = w["d_a"], w["d_b"], w["n_heads_a"], w["n_heads_b"]
    n = bsz * t
    x = x3.reshape(n, d)
    x1 = _ffn(x, *w["ffn1"])
    p_a, q, k, v, g_a, g_b, logf, cum = _mixproj(
        x1, w["mix_norm"], w["w_main"], w["w_f"], w["b_f"], w["q_norm"], w["k_norm"], w["bd"],
        n_shift=w["n_shift"], d_b=d_b, n_heads=hb, seq_len=t,
        q_scale=(LOG2E if past is None else 1.0) / float(HEAD_DIM) ** 0.5)

    tm = _row_tile(n)
    init_rows = prev_shift if t >= tm else jnp.repeat(prev_shift, t, axis=0)
    r, km, va, kap, b, lw, g = _rprep(p_a, init_rows, w["mu"], w["w0"], w["a0"], w["lora"], w["g2"],
                                      w["k_k"], w["k_a"], w["bd"], d_a=d_a, seq_len=t)
    t_pad = -(-t // CHUNK) * CHUNK
    seq = lambda a: jnp.pad(a.reshape(bsz, t, d_a), ((0, 0), (0, t_pad - t), (0, 0)))
    y_raw, s_fin = _rscan(seq(r), seq(km), seq(va), seq(kap), seq(b), seq(lw), _pair_state(s0))
    y_raw = y_raw[:, :t].reshape(n, d_a)
    new_s = _unpair_state(s_fin, ha)
    new_shift = p_a.reshape(bsz, t, -1)[:, -1]

    npair = d_b // LANES
    if past is None:
        tq = min(512, t)
        kb = jnp.stack(_bf16_pieces3(-LOG2E * cum.reshape(bsz, t, npair, PAIR)), axis=-1)
        kb = kb.transpose(0, 2, 1, 3, 4).reshape(bsz, npair, t, PAIR * 3)
        kb = jnp.pad(kb, ((0, 0), (0, 0), (0, 0), (0, LANES - PAIR * 3))).astype(BF16)
        y_b = _attn(q.reshape(bsz, t, d_b), k.reshape(bsz, t, d_b), v.reshape(bsz, t, d_b), kb, tq=tq)
    else:
        cache_k, cache_v, cache_lf, page_table = past
        n_pool, page = cache_k.shape[:2]
        cn = cum.reshape(bsz, t, hb)
        cn_col = cn.transpose(0, 2, 1).reshape(bsz, hb * t, 1)
        tpos = jnp.arange(t)
        ok = tpos[None, :] <= tpos[:, None]
        bn = cn.transpose(0, 2, 1)[:, :, :, None] - cn.transpose(0, 2, 1)[:, :, None, :]
        bn = jnp.where(ok[None, None], bn, NEG_BIG).reshape(bsz, hb * t, t)
        bn = jnp.pad(bn, ((0, 0), (0, 0), (0, page - t)), constant_values=NEG_BIG)
        pages_t = lambda c: c.transpose(0, 2, 3, 1).reshape(n_pool, d_b, page)
        new_t = lambda a: jnp.pad(a.reshape(bsz, t, d_b).transpose(0, 2, 1), ((0, 0), (0, 0), (0, page - t)))
        pg = next(g for g in (16, 8, 4, 2, 1) if page_table.shape[1] % g == 0)
        y_b = _dattn(page_table, q.reshape(bsz, t, d_b), cn_col, pages_t(cache_k), pages_t(cache_v),
                     cache_lf.transpose(0, 2, 1), new_t(k), new_t(v), bn, n_heads=hb, t_new=t, pg=pg)
    y_b = y_b.reshape(n, d_b)

    x2 = _merge(x1, y_raw, r, km, va, g, y_b, g_a, g_b, w["lnx_w"], w["lnx_b"], w["r_k"], w["bd"],
                w["w_o_a"], w["w_o_b"], w["w_out"])
    x3o = _ffn(x2, *w["ffn2"]).reshape(bsz, t, d)
    return x3o, (k.reshape(bsz, t, hb, HEAD_DIM), v.reshape(bsz, t, hb, HEAD_DIM),
                 logf.reshape(bsz, t, hb), new_s, new_shift)


def kernel(x_prompt, x_sample, cache_k, cache_v, cache_logf, state_wkv, state_shift, page_table,
           ffn1_norm, ffn1_wg, ffn1_wu, ffn1_wd, mix_norm, w_in, shift_mu,
           rwkv_w0, rwkv_w2, rwkv_a0, rwkv_a2, rwkv_g2, rwkv_k_k, rwkv_k_a, rwkv_r_k, rwkv_lnx_w, rwkv_lnx_b,
           fox_b_f, fox_q_norm, fox_k_norm, w_o_a, w_o_b, w_out,
           ffn2_norm, ffn2_wg, ffn2_wu, ffn2_wd):
    depth = w_in.shape[0]
    n_heads_a = state_wkv.shape[2]
    n_heads_b = cache_k.shape[3]
    bp = x_prompt.shape[0]
    xp, xs = x_prompt, x_sample
    outs_p, outs_s = [], []
    for l in range(depth):
        lw = (ffn1_norm[l], ffn1_wg[l], ffn1_wu[l], ffn1_wd[l], mix_norm[l], w_in[l], shift_mu[l],
              rwkv_w0[l], rwkv_w2[l], rwkv_a0[l], rwkv_a2[l], rwkv_g2[l], rwkv_k_k[l], rwkv_k_a[l],
              rwkv_r_k[l].reshape(-1), rwkv_lnx_w[l], rwkv_lnx_b[l], fox_b_f[l], fox_q_norm[l], fox_k_norm[l],
              w_o_a[l], w_o_b[l], w_out[l], ffn2_norm[l], ffn2_wg[l], ffn2_wu[l], ffn2_wd[l])
        w = _prep_weights(lw, n_heads_a, n_heads_b)
        prev0 = jnp.zeros((bp, w["n_shift"]), xp.dtype)
        s00 = jnp.zeros((bp, n_heads_a, HEAD_DIM, HEAD_DIM), xp.dtype)
        xp, op = _layer(xp, prev0, s00, None, w)
        xs, os_ = _layer(xs, state_shift[l], state_wkv[l], (cache_k[l], cache_v[l], cache_logf[l], page_table), w)
        outs_p.append(op)
        outs_s.append(os_)
    stack = lambda outs, i: jnp.stack([o[i] for o in outs], 0)
    return ((xp, xs) + tuple(stack(outs_p, i) for i in range(5)) + tuple(stack(outs_s, i) for i in range(5)))
```

```python
import functools

import jax
import jax.numpy as jnp
from jax import lax
from jax.experimental import pallas as pl
from jax.experimental.pallas import tpu as pltpu

F32 = jnp.float32
BF16 = jnp.bfloat16

HEAD_DIM = 64
LANES = 128
PAIR = LANES // HEAD_DIM
NORM_EPS = 1e-6
LNX_EPS = 64e-5
NEG_BIG = -1e30
CHUNK = 64
VMEM_LIMIT = 56 * 1024 * 1024

NN = (((1,), (0,)), ((), ()))
NT = (((1,), (1,)), ((), ()))
TN = (((0,), (0,)), ((), ()))


def _pieces(x, n):
    if x.dtype == BF16:
        return [x]
    out = []
    rem = x
    for i in range(n):
        p = rem.astype(BF16)
        out.append(p)
        if i + 1 < n:
            rem = rem - p.astype(F32)
    return out


def _mm(a, b, dims=NN, pa=1, pb=1):
    pa_l = _pieces(a, pa)
    pb_l = _pieces(b, pb)
    order = max(len(pa_l), len(pb_l))
    acc = None
    for i in reversed(range(len(pa_l))):
        for j in reversed(range(len(pb_l))):
            if i + j >= order:
                continue
            t = lax.dot_general(pa_l[i], pb_l[j], dims, preferred_element_type=F32)
            acc = t if acc is None else acc + t
    return acc


def _sigmoid(x):
    return 1.0 / (1.0 + jnp.exp(-x))


def _softplus(x):
    return jnp.maximum(x, 0.0) + jnp.log1p(jnp.exp(-jnp.abs(x)))


def _rms(x, g):
    return x * lax.rsqrt(jnp.mean(x * x, axis=-1, keepdims=True) + NORM_EPS) * g


def _iota(shape, dim):
    return lax.broadcasted_iota(jnp.int32, shape, dim)


def _div_pow2(x, n):
    assert n & (n - 1) == 0
    return lax.shift_right_logical(x, n.bit_length() - 1)


def _mod_pow2(x, n):
    assert n & (n - 1) == 0
    return x & (n - 1)


def _resident(shape):
    nd = len(shape)
    return pl.BlockSpec(shape, lambda *_: (0,) * nd, pipeline_mode=pl.Buffered(1))


def _params(sem):
    return pltpu.CompilerParams(dimension_semantics=sem, vmem_limit_bytes=VMEM_LIMIT)


def _row_tile(n):
    for t in (256, 128, 64, 32, 16, 8):
        if n % t == 0:
            return t
    raise ValueError(f"row count {n} is not a multiple of 8")


def _ffn_kernel(x_ref, g_ref, wg_ref, wu_ref, wd_ref, o_ref):
    x = x_ref[...]
    h = _rms(x, g_ref[...]).astype(BF16)
    gate = jnp.dot(h, wg_ref[...], preferred_element_type=F32)
    up = jnp.dot(h, wu_ref[...], preferred_element_type=F32)
    act = (gate * _sigmoid(gate) * up).astype(BF16)
    o_ref[...] = x + 0.5 * jnp.dot(act, wd_ref[...], preferred_element_type=F32)


def _ffn(x, g, wg, wu, wd):
    n, d = x.shape
    dff = wg.shape[1]
    tm = _row_tile(n)
    return pl.pallas_call(
        _ffn_kernel,
        grid=(n // tm,),
        in_specs=[pl.BlockSpec((tm, d), lambda i: (i, 0)), _resident((1, d)),
                  _resident((d, dff)), _resident((d, dff)), _resident((dff, d))],
        out_specs=pl.BlockSpec((tm, d), lambda i: (i, 0)),
        out_shape=jax.ShapeDtypeStruct((n, d), F32),
        compiler_params=_params(("arbitrary",)),
        name="ffn",
    )(x, g, wg, wu, wd)


def _head_sum(x, bd_ref, pieces):
    return _mm(x, bd_ref[...], NN, pa=pieces, pb=1)


def _mixproj_kernel(x_ref, g_ref, w_ref, wf_ref, bf_ref, qn_ref, kn_ref, bd_ref,
                    pa_ref, q_ref, k_ref, v_ref, ga_ref, gb_ref, lf_ref, cum_ref, carry_ref,
                    *, n_shift, d_b, d_model, n_heads, seq_len, tm):
    i = pl.program_id(0)
    h = _rms(x_ref[...], g_ref[...]).astype(BF16)
    c1 = n_shift
    c2 = c1 + d_b
    c3 = c2 + d_b
    c4 = c3 + d_b
    c5 = c4 + d_model
    c6 = c5 + d_model
    pa_ref[...] = jnp.dot(h, w_ref[:, 0:c1], preferred_element_type=F32)
    q = jnp.dot(h, w_ref[:, c1:c2], preferred_element_type=F32)
    k = jnp.dot(h, w_ref[:, c2:c3], preferred_element_type=F32)
    v_ref[...] = jnp.dot(h, w_ref[:, c3:c4], preferred_element_type=F32)
    ga_ref[...] = jnp.dot(h, w_ref[:, c4:c5], preferred_element_type=F32)
    gb_ref[...] = jnp.dot(h, w_ref[:, c5:c6], preferred_element_type=F32)
    inv_hd = 1.0 / HEAD_DIM
    q = q * lax.rsqrt(_head_sum(q * q, bd_ref, 2) * inv_hd + NORM_EPS) * qn_ref[...]
    k = k * lax.rsqrt(_head_sum(k * k, bd_ref, 2) * inv_hd + NORM_EPS) * kn_ref[...]
    q_ref[...] = q * (1.0 / float(HEAD_DIM) ** 0.5)
    k_ref[...] = k
    z = jnp.dot(h, wf_ref[...], preferred_element_type=F32) + bf_ref[...]
    logf = jnp.minimum(z, 0.0) - jnp.log1p(jnp.exp(-jnp.abs(z)))
    logf = jnp.where(_iota(logf.shape, 1) < n_heads, logf, 0.0)
    row = _iota((tm, tm), 0)
    col = _iota((tm, tm), 1)
    keep = col <= row
    if seq_len < tm:
        keep = keep & (_div_pow2(row, seq_len) == _div_pow2(col, seq_len))
    tri = jnp.where(keep, 1.0, 0.0).astype(BF16)
    cum = _mm(tri, logf, NN, pa=1, pb=3)
    if seq_len > tm:
        tiles_per_seq = seq_len // tm

        @pl.when(i % tiles_per_seq == 0)
        def _():
            carry_ref[...] = jnp.zeros_like(carry_ref)

        cum = cum + carry_ref[...]
        carry_ref[...] = cum[tm - 1:tm, :]
    lf_ref[...] = logf[:, :n_heads]
    cum_ref[...] = cum[:, :n_heads]


def _mixproj(x, g, w_main, w_f, b_f, qn, kn, bd, *, n_shift, d_b, n_heads, seq_len):
    n, d = x.shape
    tm = _row_tile(n)
    assert seq_len % tm == 0 or tm % seq_len == 0
    row = lambda w: pl.BlockSpec((tm, w), lambda i: (i, 0))
    shp = lambda w: jax.ShapeDtypeStruct((n, w), F32)
    kern = functools.partial(_mixproj_kernel, n_shift=n_shift, d_b=d_b, d_model=d, n_heads=n_heads,
                             seq_len=seq_len, tm=tm)
    return pl.pallas_call(
        kern,
        grid=(n // tm,),
        in_specs=[row(d), _resident((1, d)), _resident(w_main.shape), _resident(w_f.shape),
                  _resident((1, LANES)), _resident((1, d_b)), _resident((1, d_b)), _resident(bd.shape)],
        out_specs=[row(n_shift), row(d_b), row(d_b), row(d_b), row(d), row(d), row(n_heads), row(n_heads)],
        out_shape=[shp(n_shift), shp(d_b), shp(d_b), shp(d_b), shp(d), shp(d), shp(n_heads), shp(n_heads)],
        scratch_shapes=[pltpu.VMEM((1, LANES), F32)],
        compiler_params=_params(("arbitrary",)),
        name="mixproj",
    )(x, g, w_main, w_f, b_f, qn, kn, bd)


def _rprep_kernel(p_ref, init_ref, mu_ref, w0_ref, a0_ref, lora_ref, g2_ref, kk_ref, ka_ref, bd_ref,
                  r_ref, k_ref, v_ref, kap_ref, b_ref, lw_ref, g_ref, last_ref, *, d_a, seq_len, tm):
    i = pl.program_id(0)
    p = p_ref[...]
    prev = pltpu.roll(p, 1, 0)
    rowi = _iota(p.shape, 0)
    if seq_len >= tm:
        tiles_per_seq = seq_len // tm

        @pl.when(i % tiles_per_seq == 0)
        def _():
            last_ref[...] = init_ref[...]

        prev = jnp.where(rowi == 0, last_ref[...], prev)
        last_ref[...] = p[tm - 1:tm, :]
    else:
        prev = jnp.where(_mod_pow2(rowi, seq_len) == 0, init_ref[...], prev)
    xm = p + (prev - p) * mu_ref[...]
    r = xm[:, 0:d_a]
    k = xm[:, d_a:2 * d_a]
    v = xm[:, 2 * d_a:3 * d_a]
    xwa = xm[:, 3 * d_a:3 * d_a + LANES]
    xg = xm[:, 3 * d_a + LANES:3 * d_a + 2 * LANES]
    xwa = jnp.where(_iota(xwa.shape, 1) < HEAD_DIM, jnp.tanh(xwa), xwa)
    lo = _mm(xwa, lora_ref[...], NN, pa=2, pb=2)
    w_log = -_softplus(-(w0_ref[...] + lo[:, 0:d_a])) - 0.5
    lw_ref[...] = -jnp.exp(w_log)
    a = _sigmoid(a0_ref[...] + lo[:, d_a:2 * d_a])
    g_ref[...] = _mm(_sigmoid(xg), g2_ref[...], NN, pa=2, pb=2)
    kk = k * kk_ref[...]
    ss = _head_sum(kk * kk, bd_ref, 2)
    kk = kk / jnp.maximum(jnp.sqrt(ss), 1e-12)
    r_ref[...] = r
    k_ref[...] = k * (1.0 + (a - 1.0) * ka_ref[...])
    v_ref[...] = v
    kap_ref[...] = kk
    b_ref[...] = kk * a


def _rprep(p_a, init_rows, mu, w0, a0, lora, g2, k_k, k_a, bd, *, d_a, seq_len):
    n, ns = p_a.shape
    tm = _row_tile(n)
    row = lambda w: pl.BlockSpec((tm, w), lambda i: (i, 0))
    shp = jax.ShapeDtypeStruct((n, d_a), F32)
    if seq_len >= tm:
        tps = seq_len // tm
        init_spec = pl.BlockSpec((None, 1, ns), lambda i: (i // tps, 0, 0))
        init_rows = init_rows.reshape(-1, 1, ns)
    else:
        init_spec = row(ns)
    kern = functools.partial(_rprep_kernel, d_a=d_a, seq_len=seq_len, tm=tm)
    return pl.pallas_call(
        kern,
        grid=(n // tm,),
        in_specs=[row(ns), init_spec, _resident((1, ns)), _resident((1, d_a)), _resident((1, d_a)),
                  _resident(lora.shape), _resident(g2.shape), _resident((1, d_a)), _resident((1, d_a)),
                  _resident(bd.shape)],
        out_specs=[row(d_a)] * 7,
        out_shape=[shp] * 7,
        scratch_shapes=[pltpu.VMEM((1, ns), F32)],
        compiler_params=_params(("arbitrary",)),
        name="rprep",
    )(p_a, init_rows, mu, w0, a0, lora, g2, k_k, k_a, bd)


def _stack(x, m0):
    return jnp.concatenate([jnp.where(m0, x, 0.0), jnp.where(m0, 0.0, x)], axis=0)


def _rscan_kernel(r_ref, k_ref, v_ref, kap_ref, b_ref, lw_ref, s0_ref, y_ref, st_ref, s_scr, *, bb, npair, hs, hc, hy):
    c = CHUNK
    c2 = PAIR * c
    ci_ = pl.program_id(1)
    probs = [(i, j) for i in range(bb) for j in range(npair)]
    each = lambda f, *ls: [f(*xs) for xs in zip(*ls)]

    @pl.when(ci_ == 0)
    def _():
        zero = jnp.zeros((HEAD_DIM, HEAD_DIM), F32)
        for p, (i, j) in enumerate(probs):
            top = jnp.concatenate([s0_ref[i, PAIR * j], zero], axis=1)
            bot = jnp.concatenate([zero, s0_ref[i, PAIR * j + 1]], axis=1)
            s_scr[p] = jnp.concatenate([top, bot], axis=0)

    m0 = _iota((c, LANES), 1) < HEAD_DIM
    ri = _iota((c2, c2), 0)
    ci = _iota((c2, c2), 1)
    same = _div_pow2(ri, c) == _div_pow2(ci, c)
    strict = same & (ci < ri)
    incl = same & (ci <= ri)
    eye = ri == ci
    tri_c = jnp.where(_iota((c, c), 1) <= _iota((c, c), 0), 1.0, 0.0).astype(BF16)
    ld = lambda ref: [ref[i, :, j * LANES:(j + 1) * LANES] for (i, j) in probs]

    lw = ld(lw_ref)
    cum = each(lambda x: _mm(tri_c, x, NN, pa=1, pb=3), lw)
    cum_end = each(lambda x: x[c - 1:c, :], cum)
    e_neg = each(lambda x: jnp.exp(-x), cum)
    e_rem = each(lambda x, e: jnp.exp(e - x), cum, cum_end)
    kap, b, k = ld(kap_ref), ld(b_ref), ld(k_ref)
    a_t = each(lambda kp, x, w: _stack(-kp * jnp.exp(x - w), m0), kap, cum, lw)
    r_t = each(lambda r, x: _stack(r * jnp.exp(x), m0), ld(r_ref), cum)
    b_t = each(lambda x, e: _stack(x * e, m0), b, e_neg)
    k_t = each(lambda x, e: _stack(x * e, m0), k, e_neg)
    b_p = each(lambda x, e: _stack(x * e, m0), b, e_rem)
    k_p = each(lambda x, e: _stack(x * e, m0), k, e_rem)
    v_s = each(lambda x: _stack(x, m0), ld(v_ref))

    side = lambda x, y: jnp.concatenate([x, y], axis=1)
    bk_t = each(lambda x, y: jnp.concatenate([x, y], axis=0), b_t, k_t)
    sc_a = each(lambda x, y: _mm(x, y, NT, hs, hs), a_t, bk_t)
    sc_r = each(lambda x, y: _mm(x, y, NT, hy, hy), r_t, bk_t)
    l_ab = each(lambda x: jnp.where(strict, x[:, 0:c2], 0.0), sc_a)
    a_ak = each(lambda x: jnp.where(strict, x[:, c2:], 0.0), sc_a)
    a_rb = each(lambda x: jnp.where(incl, x[:, 0:c2], 0.0), sc_r)
    a_rk = each(lambda x: jnp.where(incl, x[:, c2:], 0.0), sc_r)

    t_inv = each(lambda x: jnp.where(eye, 1.0, 0.0) + x, l_ab)
    l_pow = l_ab
    for _ in range(c.bit_length() - 2):
        l_pow = each(lambda x: _mm(x, x, NN, hc, hc), l_pow)
        t_inv = each(lambda t, x: t + _mm(t, x, NN, hc, hc), t_inv, l_pow)

    akv = each(lambda x, y: _mm(x, y, NN, hs, hs), a_ak, v_s)
    au = each(lambda t, x, y: _mm(t, side(x, y), NN, hs, hs), t_inv, a_t, akv)
    a_p = each(lambda x: x[:, 0:LANES], au)
    u0 = each(lambda x: x[:, LANES:], au)
    ry = each(lambda x, y: _mm(x, y, NN, hy, hy), a_rb, au)
    r_p = each(lambda r, x: r + x[:, 0:LANES], r_t, ry)
    y0 = each(lambda x, y, z: _mm(x, y, NN, hy, hy) + z[:, LANES:], a_rk, v_s, ry)
    m_c = each(lambda e, x, y: jnp.where(eye, jnp.exp(e), 0.0) + _mm(x, y, TN, hs, hs), cum_end, a_p, b_p)
    n_c = each(lambda x, y, z, w: _mm(x, y, TN, hs, hs) + _mm(z, w, TN, hs, hs), u0, b_p, v_s, k_p)

    s_in = [s_scr[p] for p in range(len(probs))]
    y_st = each(lambda r, s, y: _mm(r, s, NT, hy, hy) + y, r_p, s_in, y0)
    s_out = each(lambda s, m, n: _mm(s, m, NN, hs, hs) + n, s_in, m_c, n_c)
    for p, (i, j) in enumerate(probs):
        y_ref[i, :, j * LANES:(j + 1) * LANES] = y_st[p][0:c, :] + y_st[p][c:c2, :]
        s_scr[p] = s_out[p]

    @pl.when(ci_ == pl.num_programs(1) - 1)
    def _():
        for p, (i, j) in enumerate(probs):
            st_ref[i, PAIR * j] = s_out[p][0:HEAD_DIM, 0:HEAD_DIM]
            st_ref[i, PAIR * j + 1] = s_out[p][HEAD_DIM:, HEAD_DIM:]


def _rscan(r, k, v, kap, b, lw, s0, *, hs=2, hc=1, hy=1):
    bsz, t, d_a = r.shape
    npair = d_a // LANES
    bb = 2 if bsz % 2 == 0 else 1
    seq = pl.BlockSpec((bb, CHUNK, d_a), lambda g, ci: (g, ci, 0))
    st = pl.BlockSpec((bb, PAIR * npair, HEAD_DIM, HEAD_DIM), lambda g, ci: (g, 0, 0, 0))
    kern = functools.partial(_rscan_kernel, bb=bb, npair=npair, hs=hs, hc=hc, hy=hy)
    return pl.pallas_call(
        kern,
        grid=(bsz // bb, t // CHUNK),
        in_specs=[seq] * 6 + [st],
        out_specs=[seq, st],
        out_shape=[jax.ShapeDtypeStruct((bsz, t, d_a), F32),
                   jax.ShapeDtypeStruct(s0.shape, F32)],
        scratch_shapes=[pltpu.VMEM((bb * npair, LANES, LANES), F32)],
        compiler_params=_params(("arbitrary", "arbitrary")),
        name="rscan",
    )(r, k, v, kap, b, lw, s0)


def _lanes(x, n):
    return x if n == LANES else jnp.concatenate([x] * (n // LANES), axis=1)


def _attn_kernel(q_ref, k_ref, v_ref, kb_ref, o_ref, kaug_scr, vb_scr, s_scr, p_scr, m_scr, l_scr, acc_scr,
                 *, tq, strip):
    qi = pl.program_id(2)
    tk = tq
    rows2 = 2 * tq

    @pl.when(qi == 0)
    def _():
        kaug_scr[:, 0:LANES] = k_ref[...].astype(BF16)
        kaug_scr[:, LANES:2 * LANES] = jnp.zeros((kaug_scr.shape[0], LANES), BF16)
        nck = kb_ref[...]
        p0, p1, p2 = _pieces(nck, 3)
        which = _iota(nck.shape, 1)
        kaug_scr[:, LANES:LANES + nck.shape[1]] = jnp.where((which == 0) | (which == 3), p0,
                                                            jnp.where((which == 1) | (which == 4), p1, p2))
        vb_scr[...] = v_ref[...].astype(BF16)

    q = q_ref[...]
    m0 = _iota(q.shape, 1) < HEAD_DIM
    lane = _iota((rows2, LANES), 1)
    head1 = _iota((rows2, LANES), 0) >= tq
    pick = (lane >= jnp.where(head1, 3, 0)) & (lane < jnp.where(head1, 6, 3))
    q_aug = jnp.concatenate([_stack(q, m0), jnp.where(pick, 1.0, 0.0)], axis=1).astype(BF16)
    m_scr[...] = jnp.full_like(m_scr, NEG_BIG)
    l_scr[...] = jnp.zeros_like(l_scr)
    acc_scr[...] = jnp.zeros_like(acc_scr)

    def keys(ki):
        return pl.ds(pl.multiple_of(ki * tk, tk), tk)

    def tile(ki, diagonal):
        s_scr[...] = lax.dot_general(q_aug, kaug_scr[keys(ki), :], NT, preferred_element_type=F32)
        for r0 in range(0, rows2, strip):
            rs = slice(r0, r0 + strip)
            s = s_scr[rs, :]
            if diagonal:
                s = jnp.where(_iota((strip, tk), 1) <= _iota((strip, tk), 0) + r0 % tq, s, NEG_BIG)
            m_old = m_scr[rs, :]
            m_new = jnp.maximum(m_old, jnp.max(s, axis=-1, keepdims=True))
            alpha = jnp.exp(m_old - m_new)
            p = jnp.exp(s - _lanes(m_new, tk))
            l_scr[rs, :] = alpha * l_scr[rs, :] + jnp.sum(p, axis=-1, keepdims=True)
            acc_scr[rs, :] = alpha * acc_scr[rs, :]
            p_scr[rs, :] = p.astype(BF16)
            m_scr[rs, :] = m_new
        acc_scr[...] += jnp.dot(p_scr[...], vb_scr[keys(ki), :], preferred_element_type=F32)

    def off_diagonal(ki, carry):
        tile(ki, False)
        return carry

    lax.fori_loop(0, qi, off_diagonal, 0)
    tile(qi, True)
    o = acc_scr[...] / l_scr[...]
    o_ref[...] = jnp.where(m0, o[0:tq, :], o[tq:rows2, :])


def _attn(q, k, v, kb, *, tq):
    bsz, t, d_b = q.shape
    npair = d_b // LANES
    nq = t // tq
    qspec = pl.BlockSpec((None, tq, LANES), lambda b, j, qi: (b, qi, j))
    kspec = pl.BlockSpec((None, t, LANES), lambda b, j, qi: (b, 0, j))
    kbspec = pl.BlockSpec((None, None, t, kb.shape[-1]), lambda b, j, qi: (b, j, 0, 0))
    return pl.pallas_call(
        functools.partial(_attn_kernel, tq=tq, strip=min(64, tq)),
        grid=(bsz, npair, nq),
        in_specs=[qspec, kspec, kspec, kbspec],
        out_specs=qspec,
        out_shape=jax.ShapeDtypeStruct((bsz, t, d_b), F32),
        scratch_shapes=[pltpu.VMEM((t, 2 * LANES), BF16), pltpu.VMEM((t, LANES), BF16),
                        pltpu.VMEM((2 * tq, tq), F32), pltpu.VMEM((2 * tq, tq), BF16),
                        pltpu.VMEM((2 * tq, LANES), F32), pltpu.VMEM((2 * tq, LANES), F32),
                        pltpu.VMEM((2 * tq, LANES), F32)],
        compiler_params=_params(("arbitrary",) * 3),
        name="attn",
    )(q, k, v, kb)


def _dattn_kernel(pt_ref, q_ref, cn_ref, kn_ref, vn_ref, bn_ref, kc_hbm, vc_hbm, lf_hbm, o_ref,
                  kbuf, vbuf, lbuf, sems, wq_scr, m_scr, l_scr, acc_scr, suf_scr,
                  *, n_heads, t_new, page, pg, n_pages):
    b = pl.program_id(0)
    step = pl.program_id(1)
    n_b = pl.num_programs(0)
    n_s = pl.num_programs(1)
    n_rows = n_heads * t_new
    slot = (b * n_s + step) & 1

    def page_copies(bi, si, sl):
        out = []
        for i in range(pg):
            src = pt_ref[bi, n_pages - 1 - (si * pg + i)]
            out.append(pltpu.make_async_copy(kc_hbm.at[src], kbuf.at[sl, i], sems.at[0, sl]))
            out.append(pltpu.make_async_copy(vc_hbm.at[src], vbuf.at[sl, i], sems.at[1, sl]))
            out.append(pltpu.make_async_copy(lf_hbm.at[src], lbuf.at[sl, i], sems.at[2, sl]))
        return out

    @pl.when((b == 0) & (step == 0))
    def _():
        for c in page_copies(0, 0, 0):
            c.start()

    wraps = step + 1 == n_s

    @pl.when(jnp.logical_not(wraps & (b + 1 == n_b)))
    def _():
        for c in page_copies(jnp.where(wraps, b + 1, b), jnp.where(wraps, 0, step + 1), 1 - slot):
            c.start()

    @pl.when(step == 0)
    def _():
        q = q_ref[...]
        head = _div_pow2(_iota(q.shape, 1), HEAD_DIM)
        wq_scr[...] = jnp.concatenate([jnp.where(head == h, q, 0.0) for h in range(n_heads)],
                                      axis=0).astype(BF16)
        m_scr[...] = jnp.full_like(m_scr, NEG_BIG)
        l_scr[...] = jnp.zeros_like(l_scr)
        acc_scr[...] = jnp.zeros_like(acc_scr)
        suf_scr[...] = jnp.zeros_like(suf_scr)

    for c in page_copies(b, step, slot):
        c.wait()

    def update(s, vals_t):
        n = len(vals_t)
        m_old = m_scr[...]
        m_new = jnp.maximum(m_old, jnp.max(s, axis=-1, keepdims=True))
        alpha = jnp.exp(m_old - m_new)
        p = jnp.exp(s - _lanes(m_new, n * page))
        l_scr[...] = alpha * l_scr[...] + jnp.sum(p, axis=-1, keepdims=True)
        p = p.astype(BF16)
        pv = None
        for i in range(n):
            t = lax.dot_general(p[:, i * page:(i + 1) * page], vals_t[i].astype(BF16), NT,
                                preferred_element_type=F32)
            pv = t if pv is None else pv + t
        acc_scr[...] = _lanes(alpha, acc_scr.shape[1]) * acc_scr[...] + pv
        m_scr[...] = m_new

    wq = wq_scr[...]
    lfs = [lbuf[slot, i] for i in range(pg)]
    lf_all = jnp.concatenate(lfs, axis=0)
    later = jnp.where(_iota((page, page), 0) > _iota((page, page), 1), 1.0, 0.0).astype(BF16)
    within = _mm(lf_all, later, NN, pa=3, pb=1)
    carry = suf_scr[...]
    cn = jnp.broadcast_to(cn_ref[...], (n_rows, page))
    parts = []
    for i in range(pg):
        suf = within[i * n_heads:(i + 1) * n_heads, :] + carry
        carry = carry + jnp.sum(lfs[i], axis=-1, keepdims=True)
        bias = jnp.broadcast_to(suf[:, None, :], (n_heads, t_new, page)).reshape(n_rows, page)
        parts.append(jnp.dot(wq, kbuf[slot, i].astype(BF16), preferred_element_type=F32) + bias + cn)
    suf_scr[...] = carry
    update(jnp.concatenate(parts, axis=1), [vbuf[slot, i] for i in range(pg)])

    @pl.when(step == n_s - 1)
    def _():
        s_new = jnp.dot(wq, kn_ref[...].astype(BF16), preferred_element_type=F32)
        update(s_new + bn_ref[...], [vn_ref[...]])
        o = acc_scr[...] / _lanes(l_scr[...], acc_scr.shape[1])
        head = _div_pow2(_iota((t_new, o.shape[1]), 1), HEAD_DIM)
        out = jnp.zeros((t_new, o.shape[1]), F32)
        for h in range(n_heads):
            out = out + jnp.where(head == h, o[h * t_new:(h + 1) * t_new, :], 0.0)
        o_ref[...] = out


def _dattn(page_table, q, cn_col, cache_kt, cache_vt, cache_lf_t, kt_new, vt_new, bias_new, *, n_heads, t_new, pg):
    bsz, n_pages = page_table.shape
    _, d_b, page = cache_kt.shape
    n_rows = n_heads * t_new
    assert n_pages % pg == 0
    per_b = lambda *blk: pl.BlockSpec((None,) + blk, lambda b, s, pt: (b,) + (0,) * len(blk))
    hbm = pl.BlockSpec(memory_space=pl.ANY)
    grid_spec = pltpu.PrefetchScalarGridSpec(
        num_scalar_prefetch=1,
        grid=(bsz, n_pages // pg),
        in_specs=[per_b(t_new, d_b), per_b(n_rows, 1), per_b(d_b, page), per_b(d_b, page), per_b(n_rows, page),
                  hbm, hbm, hbm],
        out_specs=per_b(t_new, d_b),
        scratch_shapes=[pltpu.VMEM((2, pg, d_b, page), F32), pltpu.VMEM((2, pg, d_b, page), F32),
                        pltpu.VMEM((2, pg, n_heads, page), F32), pltpu.SemaphoreType.DMA((3, 2)),
                        pltpu.VMEM((n_rows, d_b), BF16), pltpu.VMEM((n_rows, LANES), F32),
                        pltpu.VMEM((n_rows, LANES), F32), pltpu.VMEM((n_rows, d_b), F32),
                        pltpu.VMEM((n_heads, 1), F32)],
    )
    kern = functools.partial(_dattn_kernel, n_heads=n_heads, t_new=t_new, page=page, pg=pg, n_pages=n_pages)
    return pl.pallas_call(
        kern,
        grid_spec=grid_spec,
        out_shape=jax.ShapeDtypeStruct((bsz, t_new, d_b), F32),
        compiler_params=_params(("arbitrary", "arbitrary")),
        name="dattn",
    )(page_table, q, cn_col, kt_new, vt_new, bias_new, cache_kt, cache_vt, cache_lf_t)


def _merge_kernel(x_ref, y_ref, r_ref, k_ref, v_ref, g_ref, yb_ref, ga_ref, gb_ref,
                  lnw_ref, lnb_ref, rk_ref, bd_ref, woa_ref, wob_ref, wout_ref, o_ref):
    inv_hd = 1.0 / HEAD_DIM
    y = y_ref[...]
    mean = _head_sum(y, bd_ref, 2) * inv_hd
    yc = y - mean
    var = _head_sum(yc * yc, bd_ref, 2) * inv_hd
    yn = yc * lax.rsqrt(var + LNX_EPS) * lnw_ref[...] + lnb_ref[...]
    v = v_ref[...]
    bonus = _head_sum(r_ref[...] * k_ref[...] * rk_ref[...], bd_ref, 2) * v
    ya = ((yn + bonus) * g_ref[...]).astype(BF16)
    pa = jnp.dot(ya, woa_ref[...], preferred_element_type=F32)
    pb = jnp.dot(yb_ref[...].astype(BF16), wob_ref[...], preferred_element_type=F32)
    merged = _sigmoid(ga_ref[...]) * pa + _sigmoid(gb_ref[...]) * pb
    o_ref[...] = x_ref[...] + jnp.dot(merged.astype(BF16), wout_ref[...], preferred_element_type=F32)


def _merge(x, y, r, k, v, g, yb, ga, gb, lnw, lnb, rk, bd, woa, wob, wout):
    n, d = x.shape
    d_a = y.shape[1]
    tm = _row_tile(n)
    row = lambda w: pl.BlockSpec((tm, w), lambda i: (i, 0))
    return pl.pallas_call(
        _merge_kernel,
        grid=(n // tm,),
        in_specs=[row(d)] + [row(d_a)] * 6 + [row(d), row(d)] + [_resident((1, d_a))] * 3
                 + [_resident(bd.shape), _resident(woa.shape), _resident(wob.shape), _resident(wout.shape)],
        out_specs=row(d),
        out_shape=jax.ShapeDtypeStruct((n, d), F32),
        compiler_params=_params(("arbitrary",)),
        name="merge",
    )(x, y, r, k, v, g, yb, ga, gb, lnw, lnb, rk, bd, woa, wob, wout)


def _prep_weights(lw, n_heads_a, n_heads_b):
    (ffn1_norm, ffn1_wg, ffn1_wu, ffn1_wd, mix_norm, w_in, shift_mu, w0, w2, a0, a2, g2, k_k, k_a, r_k,
     lnx_w, lnx_b, b_f, q_norm, k_norm, w_o_a, w_o_b, w_out, ffn2_norm, ffn2_wg, ffn2_wu, ffn2_wd) = lw
    d = w_in.shape[0]
    d_a = n_heads_a * HEAD_DIM
    d_b = n_heads_b * HEAD_DIM
    n_shift = shift_mu.shape[0]
    c4 = n_shift + 3 * d_b
    row = lambda t: t.reshape(1, -1)
    w_main = jnp.concatenate([w_in[:, :c4], w_in[:, c4 + n_heads_b:]], axis=1).astype(BF16)
    w_f = jnp.pad(w_in[:, c4:c4 + n_heads_b], ((0, 0), (0, LANES - n_heads_b))).astype(BF16)
    dl, al = w2.shape[0], a2.shape[0]
    assert dl == HEAD_DIM and al == HEAD_DIM and g2.shape[0] == LANES
    lora = jnp.zeros((LANES, 2 * d_a), F32).at[:dl, :d_a].set(w2).at[dl:, d_a:].set(a2)
    bd = jnp.kron(jnp.eye(max(d_a, d_b) // HEAD_DIM, dtype=F32), jnp.ones((HEAD_DIM, HEAD_DIM), F32)).astype(BF16)
    return dict(
        d=d, d_a=d_a, d_b=d_b, n_shift=n_shift, n_heads_a=n_heads_a, n_heads_b=n_heads_b,
        ffn1=(row(ffn1_norm), ffn1_wg.astype(BF16), ffn1_wu.astype(BF16), ffn1_wd.astype(BF16)),
        ffn2=(row(ffn2_norm), ffn2_wg.astype(BF16), ffn2_wu.astype(BF16), ffn2_wd.astype(BF16)),
        mix_norm=row(mix_norm), w_main=w_main, w_f=w_f,
        b_f=jnp.pad(b_f, (0, LANES - n_heads_b)).reshape(1, LANES),
        q_norm=row(jnp.tile(q_norm, n_heads_b)), k_norm=row(jnp.tile(k_norm, n_heads_b)), bd=bd,
        mu=row(shift_mu), w0=row(w0), a0=row(a0), lora=lora, g2=g2, k_k=row(k_k), k_a=row(k_a),
        r_k=row(r_k), lnx_w=row(lnx_w), lnx_b=row(lnx_b),
        w_o_a=w_o_a.astype(BF16), w_o_b=w_o_b.astype(BF16), w_out=w_out.astype(BF16),
    )


def _layer(x3, prev_shift, s0, past, w):
    bsz, t, d = x3.shape
    d_a, d_b, ha, hb = w["d_a"], w["d_b"], w["n_heads_a"], w["n_heads_b"]
    n = bsz * t
    x = x3.reshape(n, d)
    x1 = _ffn(x, *w["ffn1"])
    p_a, q, k, v, g_a, g_b, logf, cum = _mixproj(
        x1, w["mix_norm"], w["w_main"], w["w_f"], w["b_f"], w["q_norm"], w["k_norm"], w["bd"],
        n_shift=w["n_shift"], d_b=d_b, n_heads=hb, seq_len=t)

    tm = _row_tile(n)
    init_rows = prev_shift if t >= tm else jnp.repeat(prev_shift, t, axis=0)
    r, km, va, kap, b, lw, g = _rprep(p_a, init_rows, w["mu"], w["w0"], w["a0"], w["lora"], w["g2"],
                                      w["k_k"], w["k_a"], w["bd"], d_a=d_a, seq_len=t)
    t_pad = -(-t // CHUNK) * CHUNK
    seq = lambda a: jnp.pad(a.reshape(bsz, t, d_a), ((0, 0), (0, t_pad - t), (0, 0)))
    y_raw, new_s = _rscan(seq(r), seq(km), seq(va), seq(kap), seq(b), seq(lw), s0)
    y_raw = y_raw[:, :t].reshape(n, d_a)
    new_shift = p_a.reshape(bsz, t, -1)[:, -1]

    npair = d_b // LANES
    if past is None:
        tq = min(512, t)
        kb = jnp.repeat(-cum.reshape(bsz, t, npair, PAIR).transpose(0, 2, 1, 3), 3, axis=-1)
        kb = jnp.pad(kb, ((0, 0), (0, 0), (0, 0), (0, 8 - PAIR * 3)))
        y_b = _attn(q.reshape(bsz, t, d_b), k.reshape(bsz, t, d_b), v.reshape(bsz, t, d_b), kb, tq=tq)
    else:
        cache_k, cache_v, cache_lf, page_table = past
        n_pool, page = cache_k.shape[:2]
        cn = cum.reshape(bsz, t, hb)
        cn_col = cn.transpose(0, 2, 1).reshape(bsz, hb * t, 1)
        tpos = jnp.arange(t)
        ok = tpos[None, :] <= tpos[:, None]
        bn = cn.transpose(0, 2, 1)[:, :, :, None] - cn.transpose(0, 2, 1)[:, :, None, :]
        bn = jnp.where(ok[None, None], bn, NEG_BIG).reshape(bsz, hb * t, t)
        bn = jnp.pad(bn, ((0, 0), (0, 0), (0, page - t)), constant_values=NEG_BIG)
        pages_t = lambda c: c.transpose(0, 2, 3, 1).reshape(n_pool, d_b, page)
        new_t = lambda a: jnp.pad(a.reshape(bsz, t, d_b).transpose(0, 2, 1), ((0, 0), (0, 0), (0, page - t)))
        pg = next(g for g in (16, 8, 4, 2, 1) if page_table.shape[1] % g == 0)
        y_b = _dattn(page_table, q.reshape(bsz, t, d_b), cn_col, pages_t(cache_k), pages_t(cache_v),
                     cache_lf.transpose(0, 2, 1), new_t(k), new_t(v), bn, n_heads=hb, t_new=t, pg=pg)
    y_b = y_b.reshape(n, d_b)

    x2 = _merge(x1, y_raw, r, km, va, g, y_b, g_a, g_b, w["lnx_w"], w["lnx_b"], w["r_k"], w["bd"],
                w["w_o_a"], w["w_o_b"], w["w_out"])
    x3o = _ffn(x2, *w["ffn2"]).reshape(bsz, t, d)
    return x3o, (k.reshape(bsz, t, hb, HEAD_DIM), v.reshape(bsz, t, hb, HEAD_DIM),
                 logf.reshape(bsz, t, hb), new_s, new_shift)


def kernel(x_prompt, x_sample, cache_k, cache_v, cache_logf, state_wkv, state_shift, page_table,
           ffn1_norm, ffn1_wg, ffn1_wu, ffn1_wd, mix_norm, w_in, shift_mu,
           rwkv_w0, rwkv_w2, rwkv_a0, rwkv_a2, rwkv_g2, rwkv_k_k, rwkv_k_a, rwkv_r_k, rwkv_lnx_w, rwkv_lnx_b,
           fox_b_f, fox_q_norm, fox_k_norm, w_o_a, w_o_b, w_out,
           ffn2_norm, ffn2_wg, ffn2_wu, ffn2_wd):
    depth = w_in.shape[0]
    n_heads_a = state_wkv.shape[2]
    n_heads_b = cache_k.shape[3]
    bp = x_prompt.shape[0]
    xp, xs = x_prompt, x_sample
    outs_p, outs_s = [], []
    for l in range(depth):
        lw = (ffn1_norm[l], ffn1_wg[l], ffn1_wu[l], ffn1_wd[l], mix_norm[l], w_in[l], shift_mu[l],
              rwkv_w0[l], rwkv_w2[l], rwkv_a0[l], rwkv_a2[l], rwkv_g2[l], rwkv_k_k[l], rwkv_k_a[l],
              rwkv_r_k[l].reshape(-1), rwkv_lnx_w[l], rwkv_lnx_b[l], fox_b_f[l], fox_q_norm[l], fox_k_norm[l],
              w_o_a[l], w_o_b[l], w_out[l], ffn2_norm[l], ffn2_wg[l], ffn2_wu[l], ffn2_wd[l])
        w = _prep_weights(lw, n_heads_a, n_heads_b)
        prev0 = jnp.zeros((bp, w["n_shift"]), xp.dtype)
        s00 = jnp.zeros((bp, n_heads_a, HEAD_DIM, HEAD_DIM), xp.dtype)
        xp, op = _layer(xp, prev0, s00, None, w)
        xs, os_ = _layer(xs, state_shift[l], state_wkv[l], (cache_k[l], cache_v[l], cache_logf[l], page_table), w)
        outs_p.append(op)
        outs_s.append(os_)
    stack = lambda outs, i: jnp.stack([o[i] for o in outs], 0)
    return ((xp, xs) + tuple(stack(outs_p, i) for i in range(5)) + tuple(stack(outs_s, i) for i in range(5)))
```

```python
import functools

import jax
import jax.numpy as jnp
from jax import lax
from jax.experimental import pallas as pl
from jax.experimental.pallas import tpu as pltpu

F32 = jnp.float32
BF16 = jnp.bfloat16

HEAD_DIM = 64
LANES = 128
PAIR = LANES // HEAD_DIM
NORM_EPS = 1e-6
LNX_EPS = 64e-5
NEG_BIG = -1e30
BIAS_GROUP = 16
CHUNK = 64
VMEM_LIMIT = 56 * 1024 * 1024

NN = (((1,), (0,)), ((), ()))
NT = (((1,), (1,)), ((), ()))
TN = (((0,), (0,)), ((), ()))


def _pieces(x, n):
    if x.dtype == BF16:
        return [x]
    out = []
    rem = x
    for i in range(n):
        p = rem.astype(BF16)
        out.append(p)
        if i + 1 < n:
            rem = rem - p.astype(F32)
    return out


def _mm(a, b, dims=NN, pa=1, pb=1):
    pa_l = _pieces(a, pa)
    pb_l = _pieces(b, pb)
    if len(pa_l) == 2 and len(pb_l) == 2:
        m_axis = 1 - dims[0][0][0]
        m = a.shape[m_axis]
        both = lax.dot_general(jnp.concatenate(pa_l, axis=m_axis), pb_l[0], dims, preferred_element_type=F32)
        return both[0:m] + both[m:] + lax.dot_general(pa_l[0], pb_l[1], dims, preferred_element_type=F32)
    order = max(len(pa_l), len(pb_l))
    acc = None
    for i in reversed(range(len(pa_l))):
        for j in reversed(range(len(pb_l))):
            if i + j >= order:
                continue
            t = lax.dot_general(pa_l[i], pb_l[j], dims, preferred_element_type=F32)
            acc = t if acc is None else acc + t
    return acc


def _sigmoid(x):
    return 1.0 / (1.0 + jnp.exp(-x))


def _softplus(x):
    return jnp.maximum(x, 0.0) + jnp.log1p(jnp.exp(-jnp.abs(x)))


def _rms(x, g):
    return x * lax.rsqrt(jnp.mean(x * x, axis=-1, keepdims=True) + NORM_EPS) * g


def _iota(shape, dim):
    return lax.broadcasted_iota(jnp.int32, shape, dim)


def _div_pow2(x, n):
    assert n & (n - 1) == 0
    return lax.shift_right_logical(x, n.bit_length() - 1)


def _mod_pow2(x, n):
    assert n & (n - 1) == 0
    return x & (n - 1)


def _resident(shape):
    nd = len(shape)
    return pl.BlockSpec(shape, lambda *_: (0,) * nd, pipeline_mode=pl.Buffered(1))


def _params(sem):
    return pltpu.CompilerParams(dimension_semantics=sem, vmem_limit_bytes=VMEM_LIMIT)


def _row_tile(n):
    for t in (512, 256, 128, 64, 32, 16, 8):
        if n % t == 0:
            return t
    raise ValueError(f"row count {n} is not a multiple of 8")


def _ffn_kernel(x_ref, g_ref, wg_ref, wu_ref, wd_ref, o_ref):
    x = x_ref[...]
    h = _rms(x, g_ref[...]).astype(BF16)
    gate = jnp.dot(h, wg_ref[...], preferred_element_type=F32)
    up = jnp.dot(h, wu_ref[...], preferred_element_type=F32)
    act = (gate * _sigmoid(gate) * up).astype(BF16)
    o_ref[...] = x + 0.5 * jnp.dot(act, wd_ref[...], preferred_element_type=F32)


def _ffn(x, g, wg, wu, wd):
    n, d = x.shape
    dff = wg.shape[1]
    tm = _row_tile(n)
    return pl.pallas_call(
        _ffn_kernel,
        grid=(n // tm,),
        in_specs=[pl.BlockSpec((tm, d), lambda i: (i, 0)), _resident((1, d)),
                  _resident((d, dff)), _resident((d, dff)), _resident((dff, d))],
        out_specs=pl.BlockSpec((tm, d), lambda i: (i, 0)),
        out_shape=jax.ShapeDtypeStruct((n, d), F32),
        compiler_params=_params(("arbitrary",)),
        name="ffn",
    )(x, g, wg, wu, wd)


def _head_sum(x, bd_ref, pieces):
    return _mm(x, bd_ref[...], NN, pa=pieces, pb=1)


def _mixproj_kernel(x_ref, g_ref, w_ref, wf_ref, bf_ref, qn_ref, kn_ref, bd_ref,
                    pa_ref, q_ref, k_ref, v_ref, ga_ref, gb_ref, lf_ref, cum_ref, nck_ref, carry_ref,
                    *, n_shift, d_b, d_model, n_heads, seq_len, tm):
    i = pl.program_id(0)
    h = _rms(x_ref[...], g_ref[...]).astype(BF16)
    c1 = n_shift
    c2 = c1 + d_b
    c3 = c2 + d_b
    c4 = c3 + d_b
    c5 = c4 + d_model
    c6 = c5 + d_model
    pa_ref[...] = jnp.dot(h, w_ref[:, 0:c1], preferred_element_type=F32)
    q = jnp.dot(h, w_ref[:, c1:c2], preferred_element_type=F32)
    k = jnp.dot(h, w_ref[:, c2:c3], preferred_element_type=F32)
    v_ref[...] = jnp.dot(h, w_ref[:, c3:c4], preferred_element_type=F32)
    ga_ref[...] = jnp.dot(h, w_ref[:, c4:c5], preferred_element_type=F32)
    gb_ref[...] = jnp.dot(h, w_ref[:, c5:c6], preferred_element_type=F32)
    inv_hd = 1.0 / HEAD_DIM
    q = q * lax.rsqrt(_head_sum(q * q, bd_ref, 2) * inv_hd + NORM_EPS) * qn_ref[...]
    k = k * lax.rsqrt(_head_sum(k * k, bd_ref, 2) * inv_hd + NORM_EPS) * kn_ref[...]
    q_ref[...] = q * (1.0 / float(HEAD_DIM) ** 0.5)
    k_ref[...] = k
    z = jnp.dot(h, wf_ref[...], preferred_element_type=F32) + bf_ref[...]
    logf = jnp.minimum(z, 0.0) - jnp.log1p(jnp.exp(-jnp.abs(z)))
    logf = jnp.where(_iota(logf.shape, 1) < n_heads, logf, 0.0)
    row = _iota((tm, tm), 0)
    col = _iota((tm, tm), 1)
    keep = col <= row
    if seq_len < tm:
        keep = keep & (_div_pow2(row, seq_len) == _div_pow2(col, seq_len))
    tri = jnp.where(keep, 1.0, 0.0).astype(BF16)
    cum = _mm(tri, logf, NN, pa=1, pb=3)
    if seq_len > tm:
        tiles_per_seq = seq_len // tm

        @pl.when(i % tiles_per_seq == 0)
        def _():
            carry_ref[...] = jnp.zeros_like(carry_ref)

        cum = cum + carry_ref[...]
        carry_ref[...] = cum[tm - 1:tm, :]
    lf_ref[...] = logf[:, :n_heads]
    cum_ref[...] = cum[:, :n_heads]
    src = _iota((LANES, LANES), 0)
    dst = _iota((LANES, LANES), 1)
    sub = _mod_pow2(dst, BIAS_GROUP)
    hit = (_div_pow2(dst, BIAS_GROUP) == _div_pow2(src, PAIR)) & (sub < 3 * PAIR) & (src < n_heads) \
        & ((sub >= 3) == (_mod_pow2(src, PAIR) == 1))
    nck_ref[...] = _mm(cum, jnp.where(hit, -1.0, 0.0).astype(BF16), NN, pa=3, pb=1)


def _mixproj(x, g, w_main, w_f, b_f, qn, kn, bd, *, n_shift, d_b, n_heads, seq_len):
    n, d = x.shape
    tm = _row_tile(n)
    assert seq_len % tm == 0 or tm % seq_len == 0
    row = lambda w: pl.BlockSpec((tm, w), lambda i: (i, 0))
    shp = lambda w: jax.ShapeDtypeStruct((n, w), F32)
    kern = functools.partial(_mixproj_kernel, n_shift=n_shift, d_b=d_b, d_model=d, n_heads=n_heads,
                             seq_len=seq_len, tm=tm)
    return pl.pallas_call(
        kern,
        grid=(n // tm,),
        in_specs=[row(d), _resident((1, d)), _resident(w_main.shape), _resident(w_f.shape),
                  _resident((1, LANES)), _resident((1, d_b)), _resident((1, d_b)), _resident(bd.shape)],
        out_specs=[row(n_shift), row(d_b), row(d_b), row(d_b), row(d), row(d), row(n_heads), row(n_heads), row(LANES)],
        out_shape=[shp(n_shift), shp(d_b), shp(d_b), shp(d_b), shp(d), shp(d), shp(n_heads), shp(n_heads), shp(LANES)],
        scratch_shapes=[pltpu.VMEM((1, LANES), F32)],
        compiler_params=_params(("arbitrary",)),
        name="mixproj",
    )(x, g, w_main, w_f, b_f, qn, kn, bd)


def _rprep_kernel(p_ref, init_ref, mu_ref, w0_ref, a0_ref, lora_ref, g2_ref, kk_ref, ka_ref, bd_ref,
                  r_ref, k_ref, v_ref, kap_ref, b_ref, lw_ref, g_ref, last_ref, *, d_a, seq_len, tm):
    i = pl.program_id(0)
    p = p_ref[...]
    prev = pltpu.roll(p, 1, 0)
    rowi = _iota(p.shape, 0)
    if seq_len >= tm:
        tiles_per_seq = seq_len // tm

        @pl.when(i % tiles_per_seq == 0)
        def _():
            last_ref[...] = init_ref[...]

        prev = jnp.where(rowi == 0, last_ref[...], prev)
        last_ref[...] = p[tm - 1:tm, :]
    else:
        prev = jnp.where(_mod_pow2(rowi, seq_len) == 0, init_ref[...], prev)
    xm = p + (prev - p) * mu_ref[...]
    r = xm[:, 0:d_a]
    k = xm[:, d_a:2 * d_a]
    v = xm[:, 2 * d_a:3 * d_a]
    xwa = xm[:, 3 * d_a:3 * d_a + LANES]
    xg = xm[:, 3 * d_a + LANES:3 * d_a + 2 * LANES]
    xwa = jnp.where(_iota(xwa.shape, 1) < HEAD_DIM, jnp.tanh(xwa), xwa)
    lo = _mm(xwa, lora_ref[...], NN, pa=2, pb=2)
    w_log = -_softplus(-(w0_ref[...] + lo[:, 0:d_a])) - 0.5
    lw_ref[...] = -jnp.exp(w_log)
    a = _sigmoid(a0_ref[...] + lo[:, d_a:2 * d_a])
    g_ref[...] = _mm(_sigmoid(xg), g2_ref[...], NN, pa=2, pb=2)
    kk = k * kk_ref[...]
    ss = _head_sum(kk * kk, bd_ref, 2)
    kk = kk / jnp.maximum(jnp.sqrt(ss), 1e-12)
    r_ref[...] = r
    k_ref[...] = k * (1.0 + (a - 1.0) * ka_ref[...])
    v_ref[...] = v
    kap_ref[...] = kk
    b_ref[...] = kk * a


def _rprep(p_a, init_rows, mu, w0, a0, lora, g2, k_k, k_a, bd, *, d_a, seq_len):
    n, ns = p_a.shape
    tm = _row_tile(n)
    row = lambda w: pl.BlockSpec((tm, w), lambda i: (i, 0))
    shp = jax.ShapeDtypeStruct((n, d_a), F32)
    if seq_len >= tm:
        tps = seq_len // tm
        init_spec = pl.BlockSpec((None, 1, ns), lambda i: (i // tps, 0, 0))
        init_rows = init_rows.reshape(-1, 1, ns)
    else:
        init_spec = row(ns)
    kern = functools.partial(_rprep_kernel, d_a=d_a, seq_len=seq_len, tm=tm)
    return pl.pallas_call(
        kern,
        grid=(n // tm,),
        in_specs=[row(ns), init_spec, _resident((1, ns)), _resident((1, d_a)), _resident((1, d_a)),
                  _resident(lora.shape), _resident(g2.shape), _resident((1, d_a)), _resident((1, d_a)),
                  _resident(bd.shape)],
        out_specs=[row(d_a)] * 7,
        out_shape=[shp] * 7,
        scratch_shapes=[pltpu.VMEM((1, ns), F32)],
        compiler_params=_params(("arbitrary",)),
        name="rprep",
    )(p_a, init_rows, mu, w0, a0, lora, g2, k_k, k_a, bd)


def _stack(x, m0):
    return jnp.concatenate([jnp.where(m0, x, 0.0), jnp.where(m0, 0.0, x)], axis=0)


def _rscan_kernel(r_ref, k_ref, v_ref, kap_ref, b_ref, lw_ref, s0_ref, y_ref, st_ref, s_scr, *, bb, npair, hs, hc, hy):
    c = CHUNK
    c2 = PAIR * c
    ci_ = pl.program_id(1)
    probs = [(i, j) for i in range(bb) for j in range(npair)]
    each = lambda f, *ls: [f(*xs) for xs in zip(*ls)]

    @pl.when(ci_ == 0)
    def _():
        zero = jnp.zeros((HEAD_DIM, HEAD_DIM), F32)
        for p, (i, j) in enumerate(probs):
            top = jnp.concatenate([s0_ref[i, PAIR * j], zero], axis=1)
            bot = jnp.concatenate([zero, s0_ref[i, PAIR * j + 1]], axis=1)
            s_scr[p] = jnp.concatenate([top, bot], axis=0)

    m0 = _iota((c, LANES), 1) < HEAD_DIM
    ri = _iota((c2, c2), 0)
    ci = _iota((c2, c2), 1)
    same = _div_pow2(ri, c) == _div_pow2(ci, c)
    strict = same & (ci < ri)
    incl = same & (ci <= ri)
    eye = ri == ci
    tri_c = jnp.where(_iota((c, c), 1) <= _iota((c, c), 0), 1.0, 0.0).astype(BF16)
    ld = lambda ref: [ref[i, :, j * LANES:(j + 1) * LANES] for (i, j) in probs]

    lw = ld(lw_ref)
    cum = each(lambda x: _mm(tri_c, x, NN, pa=1, pb=3), lw)
    cum_end = each(lambda x: x[c - 1:c, :], cum)
    e_neg = each(lambda x: jnp.exp(-x), cum)
    e_rem = each(lambda x, e: jnp.exp(e - x), cum, cum_end)
    kap, b, k = ld(kap_ref), ld(b_ref), ld(k_ref)
    a_t = each(lambda kp, x, w: _stack(-kp * jnp.exp(x - w), m0), kap, cum, lw)
    r_t = each(lambda r, x: _stack(r * jnp.exp(x), m0), ld(r_ref), cum)
    b_t = each(lambda x, e: _stack(x * e, m0), b, e_neg)
    k_t = each(lambda x, e: _stack(x * e, m0), k, e_neg)
    b_p = each(lambda x, e: _stack(x * e, m0), b, e_rem)
    k_p = each(lambda x, e: _stack(x * e, m0), k, e_rem)
    v_s = each(lambda x: _stack(x, m0), ld(v_ref))

    side = lambda x, y: jnp.concatenate([x, y], axis=1)
    bk_t = each(lambda x, y: jnp.concatenate([x, y], axis=0), b_t, k_t)
    sc_a = each(lambda x, y: _mm(x, y, NT, hs, hs), a_t, bk_t)
    sc_r = each(lambda x, y: _mm(x, y, NT, hy, hy), r_t, bk_t)
    l_ab = each(lambda x: jnp.where(strict, x[:, 0:c2], 0.0), sc_a)
    a_ak = each(lambda x: jnp.where(strict, x[:, c2:], 0.0), sc_a)
    a_rb = each(lambda x: jnp.where(incl, x[:, 0:c2], 0.0), sc_r)
    a_rk = each(lambda x: jnp.where(incl, x[:, c2:], 0.0), sc_r)

    t_inv = each(lambda x: jnp.where(eye, 1.0, 0.0) + x, l_ab)
    l_pow = each(lambda x: _mm(x, x, NN, hc, hc), l_ab)
    n_sq = c.bit_length() - 2
    for i in range(1, n_sq + 1):
        if i < n_sq:
            both = each(lambda x, t: _mm(jnp.concatenate([x, t], axis=0), x, NN, hc, hc), l_pow, t_inv)
            l_pow = each(lambda x: x[0:c2], both)
            t_inv = each(lambda t, x: t + x[c2:], t_inv, both)
        else:
            t_inv = each(lambda t, x: t + _mm(t, x, NN, hc, hc), t_inv, l_pow)

    akv = each(lambda x, y: _mm(x, y, NN, hs, hs), a_ak, v_s)
    au = each(lambda t, x, y: _mm(t, side(x, y), NN, hs, hs), t_inv, a_t, akv)
    a_p = each(lambda x: x[:, 0:LANES], au)
    u0 = each(lambda x: x[:, LANES:], au)
    ry = each(lambda x, y: _mm(x, y, NN, hy, hy), a_rb, au)
    r_p = each(lambda r, x: r + x[:, 0:LANES], r_t, ry)
    y0 = each(lambda x, y, z: _mm(x, y, NN, hy, hy) + z[:, LANES:], a_rk, v_s, ry)
    m_c = each(lambda e, x, y: jnp.where(eye, jnp.exp(e), 0.0) + _mm(x, y, TN, hs, hs), cum_end, a_p, b_p)
    n_c = each(lambda x, y, z, w: _mm(x, y, TN, hs, hs) + _mm(z, w, TN, hs, hs), u0, b_p, v_s, k_p)

    s_in = [s_scr[p] for p in range(len(probs))]
    y_st = each(lambda r, s, y: _mm(r, s, NT, hy, hy) + y, r_p, s_in, y0)
    s_out = each(lambda s, m, n: _mm(s, m, NN, hs, hs) + n, s_in, m_c, n_c)
    for p, (i, j) in enumerate(probs):
        y_ref[i, :, j * LANES:(j + 1) * LANES] = y_st[p][0:c, :] + y_st[p][c:c2, :]
        s_scr[p] = s_out[p]

    @pl.when(ci_ == pl.num_programs(1) - 1)
    def _():
        for p, (i, j) in enumerate(probs):
            st_ref[i, PAIR * j] = s_out[p][0:HEAD_DIM, 0:HEAD_DIM]
            st_ref[i, PAIR * j + 1] = s_out[p][HEAD_DIM:, HEAD_DIM:]


def _rscan(r, k, v, kap, b, lw, s0, *, hs=2, hc=1, hy=1):
    bsz, t, d_a = r.shape
    npair = d_a // LANES
    bb = 2 if bsz % 2 == 0 else 1
    seq = pl.BlockSpec((bb, CHUNK, d_a), lambda g, ci: (g, ci, 0))
    st = pl.BlockSpec((bb, PAIR * npair, HEAD_DIM, HEAD_DIM), lambda g, ci: (g, 0, 0, 0))
    kern = functools.partial(_rscan_kernel, bb=bb, npair=npair, hs=hs, hc=hc, hy=hy)
    return pl.pallas_call(
        kern,
        grid=(bsz // bb, t // CHUNK),
        in_specs=[seq] * 6 + [st],
        out_specs=[seq, st],
        out_shape=[jax.ShapeDtypeStruct((bsz, t, d_a), F32),
                   jax.ShapeDtypeStruct(s0.shape, F32)],
        scratch_shapes=[pltpu.VMEM((bb * npair, LANES, LANES), F32)],
        compiler_params=_params(("arbitrary", "arbitrary")),
        name="rscan",
    )(r, k, v, kap, b, lw, s0)


def _lanes(x, n):
    return x if n == LANES else jnp.concatenate([x] * (n // LANES), axis=1)


def _attn_kernel(q_ref, k_ref, v_ref, kb_ref, o_ref, kaug_scr, vb_scr, s_scr, p_scr, m_scr, l_scr, acc_scr,
                 *, tq, strip):
    qi = pl.program_id(2)
    tk = tq
    rows2 = 2 * tq

    @pl.when(qi == 0)
    def _():
        kaug_scr[:, 0:LANES] = k_ref[...].astype(BF16)
        p0, p1, p2 = _pieces(kb_ref[...], 3)
        which = _mod_pow2(_iota(p0.shape, 1), BIAS_GROUP)
        kaug_scr[:, LANES:2 * LANES] = jnp.where((which == 0) | (which == 3), p0,
                                                 jnp.where((which == 1) | (which == 4), p1, p2))
        vb_scr[...] = v_ref[...].astype(BF16)

    q = q_ref[...]
    m0 = _iota(q.shape, 1) < HEAD_DIM
    lane = _iota((rows2, LANES), 1) - BIAS_GROUP * pl.program_id(1)
    head1 = _iota((rows2, LANES), 0) >= tq
    pick = (lane >= jnp.where(head1, 3, 0)) & (lane < jnp.where(head1, 6, 3))
    q_aug = jnp.concatenate([_stack(q, m0), jnp.where(pick, 1.0, 0.0)], axis=1).astype(BF16)
    m_scr[...] = jnp.full_like(m_scr, NEG_BIG)
    l_scr[...] = jnp.zeros_like(l_scr)
    acc_scr[...] = jnp.zeros_like(acc_scr)

    def keys(ki):
        return pl.ds(pl.multiple_of(ki * tk, tk), tk)

    def tile(ki, diagonal):
        s_scr[...] = lax.dot_general(q_aug, kaug_scr[keys(ki), :], NT, preferred_element_type=F32)
        for r0 in range(0, rows2, strip):
            rs = slice(r0, r0 + strip)
            s = s_scr[rs, :]
            if diagonal:
                s = jnp.where(_iota((strip, tk), 1) <= _iota((strip, tk), 0) + r0 % tq, s, NEG_BIG)
            m_old = m_scr[rs, :]
            m_new = jnp.maximum(m_old, jnp.max(s, axis=-1, keepdims=True))
            alpha = jnp.exp(m_old - m_new)
            p = jnp.exp(s - _lanes(m_new, tk))
            l_scr[rs, :] = alpha * l_scr[rs, :] + jnp.sum(p, axis=-1, keepdims=True)
            acc_scr[rs, :] = alpha * acc_scr[rs, :]
            p_scr[rs, :] = p.astype(BF16)
            m_scr[rs, :] = m_new
        acc_scr[...] += jnp.dot(p_scr[...], vb_scr[keys(ki), :], preferred_element_type=F32)

    def off_diagonal(ki, carry):
        tile(ki, False)
        return carry

    lax.fori_loop(0, qi, off_diagonal, 0)
    tile(qi, True)
    o = acc_scr[...] / l_scr[...]
    o_ref[...] = jnp.where(m0, o[0:tq, :], o[tq:rows2, :])


def _attn(q, k, v, kb, *, tq):
    bsz, t, d_b = q.shape
    npair = d_b // LANES
    nq = t // tq
    qspec = pl.BlockSpec((None, tq, LANES), lambda b, j, qi: (b, qi, j))
    kspec = pl.BlockSpec((None, t, LANES), lambda b, j, qi: (b, 0, j))
    kbspec = pl.BlockSpec((None, t, LANES), lambda b, j, qi: (b, 0, 0))
    return pl.pallas_call(
        functools.partial(_attn_kernel, tq=tq, strip=min(64, tq)),
        grid=(bsz, npair, nq),
        in_specs=[qspec, kspec, kspec, kbspec],
        out_specs=qspec,
        out_shape=jax.ShapeDtypeStruct((bsz, t, d_b), F32),
        scratch_shapes=[pltpu.VMEM((t, 2 * LANES), BF16), pltpu.VMEM((t, LANES), BF16),
                        pltpu.VMEM((2 * tq, tq), F32), pltpu.VMEM((2 * tq, tq), BF16),
                        pltpu.VMEM((2 * tq, LANES), F32), pltpu.VMEM((2 * tq, LANES), F32),
                        pltpu.VMEM((2 * tq, LANES), F32)],
        compiler_params=_params(("arbitrary",) * 3),
        name="attn",
    )(q, k, v, kb)


def _dattn_kernel(pt_ref, q_ref, cn_ref, kn_ref, vn_ref, bn_ref, kc_hbm, vc_hbm, lf_hbm, o_ref,
                  kbuf, vbuf, lbuf, sems, wq_scr, m_scr, l_scr, acc_scr, suf_scr,
                  *, n_heads, t_new, page, pg, n_pages):
    b = pl.program_id(0)
    step = pl.program_id(1)
    n_b = pl.num_programs(0)
    n_s = pl.num_programs(1)
    n_rows = n_heads * t_new
    slot = (b * n_s + step) & 1

    def page_copies(bi, si, sl):
        out = []
        for i in range(pg):
            src = pt_ref[bi, n_pages - 1 - (si * pg + i)]
            out.append(pltpu.make_async_copy(kc_hbm.at[src], kbuf.at[sl, i], sems.at[0, sl]))
            out.append(pltpu.make_async_copy(vc_hbm.at[src], vbuf.at[sl, i], sems.at[1, sl]))
            out.append(pltpu.make_async_copy(lf_hbm.at[src], lbuf.at[sl, i], sems.at[2, sl]))
        return out

    @pl.when((b == 0) & (step == 0))
    def _():
        for c in page_copies(0, 0, 0):
            c.start()

    wraps = step + 1 == n_s

    @pl.when(jnp.logical_not(wraps & (b + 1 == n_b)))
    def _():
        for c in page_copies(jnp.where(wraps, b + 1, b), jnp.where(wraps, 0, step + 1), 1 - slot):
            c.start()

    @pl.when(step == 0)
    def _():
        q = q_ref[...]
        head = _div_pow2(_iota(q.shape, 1), HEAD_DIM)
        wq_scr[...] = jnp.concatenate([jnp.where(head == h, q, 0.0) for h in range(n_heads)],
                                      axis=0).astype(BF16)
        m_scr[...] = jnp.full_like(m_scr, NEG_BIG)
        l_scr[...] = jnp.zeros_like(l_scr)
        acc_scr[...] = jnp.zeros_like(acc_scr)
        suf_scr[...] = jnp.zeros_like(suf_scr)

    for c in page_copies(b, step, slot):
        c.wait()

    def update(s, vals_t):
        n = len(vals_t)
        m_old = m_scr[...]
        m_new = jnp.maximum(m_old, jnp.max(s, axis=-1, keepdims=True))
        alpha = jnp.exp(m_old - m_new)
        p = jnp.exp(s - _lanes(m_new, n * page))
        l_scr[...] = alpha * l_scr[...] + jnp.sum(p, axis=-1, keepdims=True)
        p = p.astype(BF16)
        pv = None
        for i in range(n):
            t = lax.dot_general(p[:, i * page:(i + 1) * page], vals_t[i].astype(BF16), NT,
                                preferred_element_type=F32)
            pv = t if pv is None else pv + t
        acc_scr[...] = _lanes(alpha, acc_scr.shape[1]) * acc_scr[...] + pv
        m_scr[...] = m_new

    wq = wq_scr[...]
    lfs = [lbuf[slot, i] for i in range(pg)]
    lf_all = jnp.concatenate(lfs, axis=0)
    later = jnp.where(_iota((page, page), 0) > _iota((page, page), 1), 1.0, 0.0).astype(BF16)
    within = _mm(lf_all, later, NN, pa=3, pb=1)
    carry = suf_scr[...]
    cn = jnp.broadcast_to(cn_ref[...], (n_rows, page))
    parts = []
    for i in range(pg):
        suf = within[i * n_heads:(i + 1) * n_heads, :] + carry
        carry = carry + jnp.sum(lfs[i], axis=-1, keepdims=True)
        bias = jnp.broadcast_to(suf[:, None, :], (n_heads, t_new, page)).reshape(n_rows, page)
        parts.append(jnp.dot(wq, kbuf[slot, i].astype(BF16), preferred_element_type=F32) + bias + cn)
    suf_scr[...] = carry
    update(jnp.concatenate(parts, axis=1), [vbuf[slot, i] for i in range(pg)])

    @pl.when(step == n_s - 1)
    def _():
        s_new = jnp.dot(wq, kn_ref[...].astype(BF16), preferred_element_type=F32)
        update(s_new + bn_ref[...], [vn_ref[...]])
        o = acc_scr[...] / _lanes(l_scr[...], acc_scr.shape[1])
        head = _div_pow2(_iota((t_new, o.shape[1]), 1), HEAD_DIM)
        out = jnp.zeros((t_new, o.shape[1]), F32)
        for h in range(n_heads):
            out = out + jnp.where(head == h, o[h * t_new:(h + 1) * t_new, :], 0.0)
        o_ref[...] = out


def _dattn(page_table, q, cn_col, cache_kt, cache_vt, cache_lf_t, kt_new, vt_new, bias_new, *, n_heads, t_new, pg):
    bsz, n_pages = page_table.shape
    _, d_b, page = cache_kt.shape
    n_rows = n_heads * t_new
    assert n_pages % pg == 0
    per_b = lambda *blk: pl.BlockSpec((None,) + blk, lambda b, s, pt: (b,) + (0,) * len(blk))
    hbm = pl.BlockSpec(memory_space=pl.ANY)
    grid_spec = pltpu.PrefetchScalarGridSpec(
        num_scalar_prefetch=1,
        grid=(bsz, n_pages // pg),
        in_specs=[per_b(t_new, d_b), per_b(n_rows, 1), per_b(d_b, page), per_b(d_b, page), per_b(n_rows, page),
                  hbm, hbm, hbm],
        out_specs=per_b(t_new, d_b),
        scratch_shapes=[pltpu.VMEM((2, pg, d_b, page), F32), pltpu.VMEM((2, pg, d_b, page), F32),
                        pltpu.VMEM((2, pg, n_heads, page), F32), pltpu.SemaphoreType.DMA((3, 2)),
                        pltpu.VMEM((n_rows, d_b), BF16), pltpu.VMEM((n_rows, LANES), F32),
                        pltpu.VMEM((n_rows, LANES), F32), pltpu.VMEM((n_rows, d_b), F32),
                        pltpu.VMEM((n_heads, 1), F32)],
    )
    kern = functools.partial(_dattn_kernel, n_heads=n_heads, t_new=t_new, page=page, pg=pg, n_pages=n_pages)
    return pl.pallas_call(
        kern,
        grid_spec=grid_spec,
        out_shape=jax.ShapeDtypeStruct((bsz, t_new, d_b), F32),
        compiler_params=_params(("arbitrary", "arbitrary")),
        name="dattn",
    )(page_table, q, cn_col, kt_new, vt_new, bias_new, cache_kt, cache_vt, cache_lf_t)


def _merge_kernel(x_ref, y_ref, r_ref, k_ref, v_ref, g_ref, yb_ref, ga_ref, gb_ref,
                  lnw_ref, lnb_ref, rk_ref, bd_ref, woa_ref, wob_ref, wout_ref, o_ref):
    inv_hd = 1.0 / HEAD_DIM
    y = y_ref[...]
    mean = _head_sum(y, bd_ref, 2) * inv_hd
    yc = y - mean
    var = _head_sum(yc * yc, bd_ref, 2) * inv_hd
    yn = yc * lax.rsqrt(var + LNX_EPS) * lnw_ref[...] + lnb_ref[...]
    v = v_ref[...]
    bonus = _head_sum(r_ref[...] * k_ref[...] * rk_ref[...], bd_ref, 2) * v
    ya = ((yn + bonus) * g_ref[...]).astype(BF16)
    pa = jnp.dot(ya, woa_ref[...], preferred_element_type=F32)
    pb = jnp.dot(yb_ref[...].astype(BF16), wob_ref[...], preferred_element_type=F32)
    merged = _sigmoid(ga_ref[...]) * pa + _sigmoid(gb_ref[...]) * pb
    o_ref[...] = x_ref[...] + jnp.dot(merged.astype(BF16), wout_ref[...], preferred_element_type=F32)


def _merge(x, y, r, k, v, g, yb, ga, gb, lnw, lnb, rk, bd, woa, wob, wout):
    n, d = x.shape
    d_a = y.shape[1]
    tm = _row_tile(n)
    row = lambda w: pl.BlockSpec((tm, w), lambda i: (i, 0))
    return pl.pallas_call(
        _merge_kernel,
        grid=(n // tm,),
        in_specs=[row(d)] + [row(d_a)] * 6 + [row(d), row(d)] + [_resident((1, d_a))] * 3
                 + [_resident(bd.shape), _resident(woa.shape), _resident(wob.shape), _resident(wout.shape)],
        out_specs=row(d),
        out_shape=jax.ShapeDtypeStruct((n, d), F32),
        compiler_params=_params(("arbitrary",)),
        name="merge",
    )(x, y, r, k, v, g, yb, ga, gb, lnw, lnb, rk, bd, woa, wob, wout)


def _prep_weights(lw, n_heads_a, n_heads_b):
    (ffn1_norm, ffn1_wg, ffn1_wu, ffn1_wd, mix_norm, w_in, shift_mu, w0, w2, a0, a2, g2, k_k, k_a, r_k,
     lnx_w, lnx_b, b_f, q_norm, k_norm, w_o_a, w_o_b, w_out, ffn2_norm, ffn2_wg, ffn2_wu, ffn2_wd) = lw
    d = w_in.shape[0]
    d_a = n_heads_a * HEAD_DIM
    d_b = n_heads_b * HEAD_DIM
    n_shift = shift_mu.shape[0]
    c4 = n_shift + 3 * d_b
    row = lambda t: t.reshape(1, -1)
    w_main = jnp.concatenate([w_in[:, :c4], w_in[:, c4 + n_heads_b:]], axis=1).astype(BF16)
    w_f = jnp.pad(w_in[:, c4:c4 + n_heads_b], ((0, 0), (0, LANES - n_heads_b))).astype(BF16)
    dl, al = w2.shape[0], a2.shape[0]
    assert dl == HEAD_DIM and al == HEAD_DIM and g2.shape[0] == LANES
    lora = jnp.zeros((LANES, 2 * d_a), F32).at[:dl, :d_a].set(w2).at[dl:, d_a:].set(a2)
    bd = jnp.kron(jnp.eye(max(d_a, d_b) // HEAD_DIM, dtype=F32), jnp.ones((HEAD_DIM, HEAD_DIM), F32)).astype(BF16)
    return dict(
        d=d, d_a=d_a, d_b=d_b, n_shift=n_shift, n_heads_a=n_heads_a, n_heads_b=n_heads_b,
        ffn1=(row(ffn1_norm), ffn1_wg.astype(BF16), ffn1_wu.astype(BF16), ffn1_wd.astype(BF16)),
        ffn2=(row(ffn2_norm), ffn2_wg.astype(BF16), ffn2_wu.astype(BF16), ffn2_wd.astype(BF16)),
        mix_norm=row(mix_norm), w_main=w_main, w_f=w_f,
        b_f=jnp.pad(b_f, (0, LANES - n_heads_b)).reshape(1, LANES),
        q_norm=row(jnp.tile(q_norm, n_heads_b)), k_norm=row(jnp.tile(k_norm, n_heads_b)), bd=bd,
        mu=row(shift_mu), w0=row(w0), a0=row(a0), lora=lora, g2=g2, k_k=row(k_k), k_a=row(k_a),
        r_k=row(r_k), lnx_w=row(lnx_w), lnx_b=row(lnx_b),
        w_o_a=w_o_a.astype(BF16), w_o_b=w_o_b.astype(BF16), w_out=w_out.astype(BF16),
    )


def _layer(x3, prev_shift, s0, past, w):
    bsz, t, d = x3.shape
    d_a, d_b, ha, hb = w["d_a"], w["d_b"], w["n_heads_a"], w["n_heads_b"]
    n = bsz * t
    x = x3.reshape(n, d)
    x1 = _ffn(x, *w["ffn1"])
    p_a, q, k, v, g_a, g_b, logf, cum, nck = _mixproj(
        x1, w["mix_norm"], w["w_main"], w["w_f"], w["b_f"], w["q_norm"], w["k_norm"], w["bd"],
        n_shift=w["n_shift"], d_b=d_b, n_heads=hb, seq_len=t)

    tm = _row_tile(n)
    init_rows = prev_shift if t >= tm else jnp.repeat(prev_shift, t, axis=0)
    r, km, va, kap, b, lw, g = _rprep(p_a, init_rows, w["mu"], w["w0"], w["a0"], w["lora"], w["g2"],
                                      w["k_k"], w["k_a"], w["bd"], d_a=d_a, seq_len=t)
    t_pad = -(-t // CHUNK) * CHUNK
    seq = lambda a: jnp.pad(a.reshape(bsz, t, d_a), ((0, 0), (0, t_pad - t), (0, 0)))
    y_raw, new_s = _rscan(seq(r), seq(km), seq(va), seq(kap), seq(b), seq(lw), s0)
    y_raw = y_raw[:, :t].reshape(n, d_a)
    new_shift = p_a.reshape(bsz, t, -1)[:, -1]

    npair = d_b // LANES
    if past is None:
        tq = min(512, t)
        y_b = _attn(q.reshape(bsz, t, d_b), k.reshape(bsz, t, d_b), v.reshape(bsz, t, d_b),
                    nck.reshape(bsz, t, LANES), tq=tq)
    else:
        cache_k, cache_v, cache_lf, page_table = past
        n_pool, page = cache_k.shape[:2]
        cn = cum.reshape(bsz, t, hb)
        cn_col = cn.transpose(0, 2, 1).reshape(bsz, hb * t, 1)
        tpos = jnp.arange(t)
        ok = tpos[None, :] <= tpos[:, None]
        bn = cn.transpose(0, 2, 1)[:, :, :, None] - cn.transpose(0, 2, 1)[:, :, None, :]
        bn = jnp.where(ok[None, None], bn, NEG_BIG).reshape(bsz, hb * t, t)
        bn = jnp.pad(bn, ((0, 0), (0, 0), (0, page - t)), constant_values=NEG_BIG)
        pages_t = lambda c: c.transpose(0, 2, 3, 1).reshape(n_pool, d_b, page)
        new_t = lambda a: jnp.pad(a.reshape(bsz, t, d_b).transpose(0, 2, 1), ((0, 0), (0, 0), (0, page - t)))
        pg = next(g for g in (16, 8, 4, 2, 1) if page_table.shape[1] % g == 0)
        y_b = _dattn(page_table, q.reshape(bsz, t, d_b), cn_col, pages_t(cache_k), pages_t(cache_v),
                     cache_lf.transpose(0, 2, 1), new_t(k), new_t(v), bn, n_heads=hb, t_new=t, pg=pg)
    y_b = y_b.reshape(n, d_b)

    x2 = _merge(x1, y_raw, r, km, va, g, y_b, g_a, g_b, w["lnx_w"], w["lnx_b"], w["r_k"], w["bd"],
                w["w_o_a"], w["w_o_b"], w["w_out"])
    x3o = _ffn(x2, *w["ffn2"]).reshape(bsz, t, d)
    return x3o, (k.reshape(bsz, t, hb, HEAD_DIM), v.reshape(bsz, t, hb, HEAD_DIM),
                 logf.reshape(bsz, t, hb), new_s, new_shift)


def kernel(x_prompt, x_sample, cache_k, cache_v, cache_logf, state_wkv, state_shift, page_table,
           ffn1_norm, ffn1_wg, ffn1_wu, ffn1_wd, mix_norm, w_in, shift_mu,
           rwkv_w0, rwkv_w2, rwkv_a0, rwkv_a2, rwkv_g2, rwkv_k_k, rwkv_k_a, rwkv_r_k, rwkv_lnx_w, rwkv_lnx_b,
           fox_b_f, fox_q_norm, fox_k_norm, w_o_a, w_o_b, w_out,
           ffn2_norm, ffn2_wg, ffn2_wu, ffn2_wd):
    depth = w_in.shape[0]
    n_heads_a = state_wkv.shape[2]
    n_heads_b = cache_k.shape[3]
    bp = x_prompt.shape[0]
    xp, xs = x_prompt, x_sample
    outs_p, outs_s = [], []
    for l in range(depth):
        lw = (ffn1_norm[l], ffn1_wg[l], ffn1_wu[l], ffn1_wd[l], mix_norm[l], w_in[l], shift_mu[l],
              rwkv_w0[l], rwkv_w2[l], rwkv_a0[l], rwkv_a2[l], rwkv_g2[l], rwkv_k_k[l], rwkv_k_a[l],
              rwkv_r_k[l].reshape(-1), rwkv_lnx_w[l], rwkv_lnx_b[l], fox_b_f[l], fox_q_norm[l], fox_k_norm[l],
              w_o_a[l], w_o_b[l], w_out[l], ffn2_norm[l], ffn2_wg[l], ffn2_wu[l], ffn2_wd[l])
        w = _prep_weights(lw, n_heads_a, n_heads_b)
        prev0 = jnp.zeros((bp, w["n_shift"]), xp.dtype)
        s00 = jnp.zeros((bp, n_heads_a, HEAD_DIM, HEAD_DIM), xp.dtype)
        xp, op = _layer(xp, prev0, s00, None, w)
        xs, os_ = _layer(xs, state_shift[l], state_wkv[l], (cache_k[l], cache_v[l], cache_logf[l], page_table), w)
        outs_p.append(op)
        outs_s.append(os_)
    stack = lambda outs, i: jnp.stack([o[i] for o in outs], 0)
    return ((xp, xs) + tuple(stack(outs_p, i) for i in range(5)) + tuple(stack(outs_s, i) for i in range(5)))
```

```python
import functools

import jax
import jax.numpy as jnp
from jax import lax
from jax.experimental import pallas as pl
from jax.experimental.pallas import tpu as pltpu

F32 = jnp.float32
BF16 = jnp.bfloat16

HEAD_DIM = 64
LANES = 128
PAIR = LANES // HEAD_DIM
NORM_EPS = 1e-6
LNX_EPS = 64e-5
NEG_BIG = -1e30
BIAS_GROUP = 16
CHUNK = 64
VMEM_LIMIT = 56 * 1024 * 1024

NN = (((1,), (0,)), ((), ()))
NT = (((1,), (1,)), ((), ()))
TN = (((0,), (0,)), ((), ()))


def _pieces(x, n):
    if x.dtype == BF16:
        return [x]
    out = []
    rem = x
    for i in range(n):
        p = rem.astype(BF16)
        out.append(p)
        if i + 1 < n:
            rem = rem - p.astype(F32)
    return out


def _mm(a, b, dims=NN, pa=1, pb=1):
    pa_l = _pieces(a, pa)
    pb_l = _pieces(b, pb)
    if len(pa_l) == 2 and len(pb_l) == 2:
        m_axis = 1 - dims[0][0][0]
        m = a.shape[m_axis]
        both = lax.dot_general(jnp.concatenate(pa_l, axis=m_axis), pb_l[0], dims, preferred_element_type=F32)
        return both[0:m] + both[m:] + lax.dot_general(pa_l[0], pb_l[1], dims, preferred_element_type=F32)
    order = max(len(pa_l), len(pb_l))
    acc = None
    for i in reversed(range(len(pa_l))):
        for j in reversed(range(len(pb_l))):
            if i + j >= order:
                continue
            t = lax.dot_general(pa_l[i], pb_l[j], dims, preferred_element_type=F32)
            acc = t if acc is None else acc + t
    return acc


def _sigmoid(x):
    return 1.0 / (1.0 + jnp.exp(-x))


def _softplus(x):
    return jnp.maximum(x, 0.0) + jnp.log1p(jnp.exp(-jnp.abs(x)))


def _rms(x, g):
    return x * lax.rsqrt(jnp.mean(x * x, axis=-1, keepdims=True) + NORM_EPS) * g


def _iota(shape, dim):
    return lax.broadcasted_iota(jnp.int32, shape, dim)


def _div_pow2(x, n):
    assert n & (n - 1) == 0
    return lax.shift_right_logical(x, n.bit_length() - 1)


def _mod_pow2(x, n):
    assert n & (n - 1) == 0
    return x & (n - 1)


def _resident(shape):
    nd = len(shape)
    return pl.BlockSpec(shape, lambda *_: (0,) * nd, pipeline_mode=pl.Buffered(1))


def _params(sem):
    return pltpu.CompilerParams(dimension_semantics=sem, vmem_limit_bytes=VMEM_LIMIT)


def _row_tile(n):
    for t in (512, 256, 128, 64, 32, 16, 8):
        if n % t == 0:
            return t
    raise ValueError(f"row count {n} is not a multiple of 8")


def _ffn_kernel(x_ref, g_ref, wg_ref, wu_ref, wd_ref, o_ref):
    x = x_ref[...]
    h = _rms(x, g_ref[...]).astype(BF16)
    gate = jnp.dot(h, wg_ref[...], preferred_element_type=F32)
    up = jnp.dot(h, wu_ref[...], preferred_element_type=F32)
    act = (gate * _sigmoid(gate) * up).astype(BF16)
    o_ref[...] = x + 0.5 * jnp.dot(act, wd_ref[...], preferred_element_type=F32)


def _ffn(x, g, wg, wu, wd):
    n, d = x.shape
    dff = wg.shape[1]
    tm = _row_tile(n)
    return pl.pallas_call(
        _ffn_kernel,
        grid=(n // tm,),
        in_specs=[pl.BlockSpec((tm, d), lambda i: (i, 0)), _resident((1, d)),
                  _resident((d, dff)), _resident((d, dff)), _resident((dff, d))],
        out_specs=pl.BlockSpec((tm, d), lambda i: (i, 0)),
        out_shape=jax.ShapeDtypeStruct((n, d), F32),
        compiler_params=_params(("arbitrary",)),
        name="ffn",
    )(x, g, wg, wu, wd)


def _head_sum(x, bd_ref, pieces):
    return _mm(x, bd_ref[...], NN, pa=pieces, pb=1)


def _rwkv_prep(p, i, init_ref, mu_ref, w0_ref, a0_ref, lora_ref, g2_ref, kk_ref, ka_ref, bd_ref,
               r_ref, k_ref, v_ref, kap_ref, b_ref, lw_ref, g_ref, last_ref, *, d_a, seq_len, tm):
    prev = pltpu.roll(p, 1, 0)
    rowi = _iota(p.shape, 0)
    if seq_len >= tm:
        tiles_per_seq = seq_len // tm

        @pl.when(i % tiles_per_seq == 0)
        def _():
            last_ref[...] = init_ref[...]

        prev = jnp.where(rowi == 0, last_ref[...], prev)
        last_ref[...] = p[tm - 1:tm, :]
    else:
        prev = jnp.where(_mod_pow2(rowi, seq_len) == 0, init_ref[...], prev)
    xm = p + (prev - p) * mu_ref[...]
    r = xm[:, 0:d_a]
    k = xm[:, d_a:2 * d_a]
    v = xm[:, 2 * d_a:3 * d_a]
    xwa = xm[:, 3 * d_a:3 * d_a + LANES]
    xg = xm[:, 3 * d_a + LANES:3 * d_a + 2 * LANES]
    xwa = jnp.where(_iota(xwa.shape, 1) < HEAD_DIM, jnp.tanh(xwa), xwa)
    lo = _mm(xwa, lora_ref[...], NN, pa=2, pb=2)
    w_log = -_softplus(-(w0_ref[...] + lo[:, 0:d_a])) - 0.5
    lw_ref[...] = -jnp.exp(w_log)
    a = _sigmoid(a0_ref[...] + lo[:, d_a:2 * d_a])
    g_ref[...] = _mm(_sigmoid(xg), g2_ref[...], NN, pa=2, pb=2)
    kk = k * kk_ref[...]
    ss = _head_sum(kk * kk, bd_ref, 2)
    kk = kk / jnp.maximum(jnp.sqrt(ss), 1e-12)
    r_ref[...] = r
    k_ref[...] = k * (1.0 + (a - 1.0) * ka_ref[...])
    v_ref[...] = v
    kap_ref[...] = kk
    b_ref[...] = kk * a


def _mixproj_kernel(x_ref, g_ref, w_ref, wf_ref, bf_ref, qn_ref, kn_ref, bd_ref, *rest,
                    n_shift, d_a, d_b, d_model, n_heads, seq_len, tm):
    prep_in = rest[0:8]
    q_ref, k_ref, v_ref, ga_ref, gb_ref, lf_ref, cum_ref, nck_ref = rest[8:16]
    prep_out = rest[16:23]
    sh_ref, carry_ref, last_ref = rest[23:]
    i = pl.program_id(0)
    h = _rms(x_ref[...], g_ref[...]).astype(BF16)
    c1 = n_shift
    c2 = c1 + d_b
    c3 = c2 + d_b
    c4 = c3 + d_b
    c5 = c4 + d_model
    c6 = c5 + d_model
    p_a = jnp.dot(h, w_ref[:, 0:c1], preferred_element_type=F32)
    q = jnp.dot(h, w_ref[:, c1:c2], preferred_element_type=F32)
    k = jnp.dot(h, w_ref[:, c2:c3], preferred_element_type=F32)
    v_ref[...] = jnp.dot(h, w_ref[:, c3:c4], preferred_element_type=F32)
    ga_ref[...] = jnp.dot(h, w_ref[:, c4:c5], preferred_element_type=F32)
    gb_ref[...] = jnp.dot(h, w_ref[:, c5:c6], preferred_element_type=F32)
    inv_hd = 1.0 / HEAD_DIM
    q = q * lax.rsqrt(_head_sum(q * q, bd_ref, 2) * inv_hd + NORM_EPS) * qn_ref[...]
    k = k * lax.rsqrt(_head_sum(k * k, bd_ref, 2) * inv_hd + NORM_EPS) * kn_ref[...]
    q_ref[...] = q * (1.0 / float(HEAD_DIM) ** 0.5)
    k_ref[...] = k
    z = jnp.dot(h, wf_ref[...], preferred_element_type=F32) + bf_ref[...]
    logf = jnp.minimum(z, 0.0) - jnp.log1p(jnp.exp(-jnp.abs(z)))
    logf = jnp.where(_iota(logf.shape, 1) < n_heads, logf, 0.0)
    row = _iota((tm, tm), 0)
    col = _iota((tm, tm), 1)
    keep = col <= row
    if seq_len < tm:
        keep = keep & (_div_pow2(row, seq_len) == _div_pow2(col, seq_len))
    tri = jnp.where(keep, 1.0, 0.0).astype(BF16)
    cum = _mm(tri, logf, NN, pa=1, pb=3)
    if seq_len > tm:
        tiles_per_seq = seq_len // tm

        @pl.when(i % tiles_per_seq == 0)
        def _():
            carry_ref[...] = jnp.zeros_like(carry_ref)

        cum = cum + carry_ref[...]
        carry_ref[...] = cum[tm - 1:tm, :]
    lf_ref[...] = logf[:, :n_heads]
    cum_ref[...] = cum[:, :n_heads]
    src = _iota((LANES, LANES), 0)
    dst = _iota((LANES, LANES), 1)
    sub = _mod_pow2(dst, BIAS_GROUP)
    hit = (_div_pow2(dst, BIAS_GROUP) == _div_pow2(src, PAIR)) & (sub < 3 * PAIR) & (src < n_heads) \
        & ((sub >= 3) == (_mod_pow2(src, PAIR) == 1))
    nck_ref[...] = _mm(cum, jnp.where(hit, -1.0, 0.0).astype(BF16), NN, pa=3, pb=1)
    _rwkv_prep(p_a, i, *prep_in, bd_ref, *prep_out, last_ref, d_a=d_a, seq_len=seq_len, tm=tm)
    if seq_len >= tm:
        sh_ref[...] = p_a[tm - 1:tm, :]
    else:
        n_seq = tm // seq_len
        last = _iota((n_seq, tm), 1) == _iota((n_seq, tm), 0) * seq_len + (seq_len - 1)
        sh_ref[...] = _mm(jnp.where(last, 1.0, 0.0).astype(BF16), p_a, NN, pa=1, pb=3)


def _mixproj(x, g, w_main, w_f, b_f, qn, kn, bd, init_rows, mu, w0, a0, lora, g2, k_k, k_a,
             *, n_shift, d_a, d_b, n_heads, seq_len):
    n, d = x.shape
    tm = min(_row_tile(n), 256)
    assert seq_len % tm == 0 or tm % seq_len == 0
    row = lambda w: pl.BlockSpec((tm, w), lambda i: (i, 0))
    shp = lambda w: jax.ShapeDtypeStruct((n, w), F32)
    n_seq = n // seq_len
    if seq_len >= tm:
        tps = seq_len // tm
        init_spec = pl.BlockSpec((None, 1, n_shift), lambda i: (i // tps, 0, 0))
        init_rows = init_rows.reshape(n_seq, 1, n_shift)
        sh_spec = pl.BlockSpec((None, 1, n_shift), lambda i: (i // tps, 0, 0))
        sh_shape = jax.ShapeDtypeStruct((n_seq, 1, n_shift), F32)
    else:
        init_spec = row(n_shift)
        sh_spec = pl.BlockSpec((tm // seq_len, n_shift), lambda i: (i, 0))
        sh_shape = jax.ShapeDtypeStruct((n_seq, n_shift), F32)
    kern = functools.partial(_mixproj_kernel, n_shift=n_shift, d_a=d_a, d_b=d_b, d_model=d, n_heads=n_heads,
                             seq_len=seq_len, tm=tm)
    outs = pl.pallas_call(
        kern,
        grid=(n // tm,),
        in_specs=[row(d), _resident((1, d)), _resident(w_main.shape), _resident(w_f.shape),
                  _resident((1, LANES)), _resident((1, d_b)), _resident((1, d_b)), _resident(bd.shape),
                  init_spec, _resident((1, n_shift)), _resident((1, d_a)), _resident((1, d_a)),
                  _resident(lora.shape), _resident(g2.shape), _resident((1, d_a)), _resident((1, d_a))],
        out_specs=[row(d_b), row(d_b), row(d_b), row(d), row(d), row(n_heads), row(n_heads), row(LANES)]
                  + [row(d_a)] * 7 + [sh_spec],
        out_shape=[shp(d_b), shp(d_b), shp(d_b), shp(d), shp(d), shp(n_heads), shp(n_heads), shp(LANES)]
                  + [shp(d_a)] * 7 + [sh_shape],
        scratch_shapes=[pltpu.VMEM((1, LANES), F32), pltpu.VMEM((1, n_shift), F32)],
        compiler_params=_params(("arbitrary",)),
        name="mixproj",
    )(x, g, w_main, w_f, b_f, qn, kn, bd, init_rows, mu, w0, a0, lora, g2, k_k, k_a)
    return tuple(outs[:-1]) + (outs[-1].reshape(n_seq, n_shift),)


def _stack(x, m0):
    return jnp.concatenate([jnp.where(m0, x, 0.0), jnp.where(m0, 0.0, x)], axis=0)


def _rscan_kernel(r_ref, k_ref, v_ref, kap_ref, b_ref, lw_ref, s0_ref, y_ref, st_ref, s_scr, *, bb, npair, hs, hc, hy):
    c = CHUNK
    c2 = PAIR * c
    ci_ = pl.program_id(1)
    probs = [(i, j) for i in range(bb) for j in range(npair)]
    each = lambda f, *ls: [f(*xs) for xs in zip(*ls)]

    @pl.when(ci_ == 0)
    def _():
        zero = jnp.zeros((HEAD_DIM, HEAD_DIM), F32)
        for p, (i, j) in enumerate(probs):
            top = jnp.concatenate([s0_ref[i, PAIR * j], zero], axis=1)
            bot = jnp.concatenate([zero, s0_ref[i, PAIR * j + 1]], axis=1)
            s_scr[p] = jnp.concatenate([top, bot], axis=0)

    m0 = _iota((c, LANES), 1) < HEAD_DIM
    ri = _iota((c2, c2), 0)
    ci = _iota((c2, c2), 1)
    same = _div_pow2(ri, c) == _div_pow2(ci, c)
    strict = same & (ci < ri)
    incl = same & (ci <= ri)
    eye = ri == ci
    tri_c = jnp.where(_iota((c, c), 1) <= _iota((c, c), 0), 1.0, 0.0).astype(BF16)
    ld = lambda ref: [ref[i, :, j * LANES:(j + 1) * LANES] for (i, j) in probs]

    lw = ld(lw_ref)
    cum = each(lambda x: _mm(tri_c, x, NN, pa=1, pb=3), lw)
    cum_end = each(lambda x: x[c - 1:c, :], cum)
    e_neg = each(lambda x: jnp.exp(-x), cum)
    e_rem = each(lambda x, e: jnp.exp(e - x), cum, cum_end)
    kap, b, k = ld(kap_ref), ld(b_ref), ld(k_ref)
    a_t = each(lambda kp, x, w: _stack(-kp * jnp.exp(x - w), m0), kap, cum, lw)
    r_t = each(lambda r, x: _stack(r * jnp.exp(x), m0), ld(r_ref), cum)
    b_t = each(lambda x, e: _stack(x * e, m0), b, e_neg)
    k_t = each(lambda x, e: _stack(x * e, m0), k, e_neg)
    b_p = each(lambda x, e: _stack(x * e, m0), b, e_rem)
    k_p = each(lambda x, e: _stack(x * e, m0), k, e_rem)
    v_s = each(lambda x: _stack(x, m0), ld(v_ref))

    side = lambda x, y: jnp.concatenate([x, y], axis=1)
    bk_t = each(lambda x, y: jnp.concatenate([x, y], axis=0), b_t, k_t)
    sc_a = each(lambda x, y: _mm(x, y, NT, hs, hs), a_t, bk_t)
    sc_r = each(lambda x, y: _mm(x, y, NT, hy, hy), r_t, bk_t)
    l_ab = each(lambda x: jnp.where(strict, x[:, 0:c2], 0.0), sc_a)
    a_ak = each(lambda x: jnp.where(strict, x[:, c2:], 0.0), sc_a)
    a_rb = each(lambda x: jnp.where(incl, x[:, 0:c2], 0.0), sc_r)
    a_rk = each(lambda x: jnp.where(incl, x[:, c2:], 0.0), sc_r)

    t_inv = each(lambda x: jnp.where(eye, 1.0, 0.0) + x, l_ab)
    l_pow = each(lambda x: _mm(x, x, NN, hc, hc), l_ab)
    n_sq = c.bit_length() - 2
    for i in range(1, n_sq + 1):
        if i < n_sq:
            both = each(lambda x, t: _mm(jnp.concatenate([x, t], axis=0), x, NN, hc, hc), l_pow, t_inv)
            l_pow = each(lambda x: x[0:c2], both)
            t_inv = each(lambda t, x: t + x[c2:], t_inv, both)
        else:
            t_inv = each(lambda t, x: t + _mm(t, x, NN, hc, hc), t_inv, l_pow)

    akv = each(lambda x, y: _mm(x, y, NN, hs, hs), a_ak, v_s)
    au = each(lambda t, x, y: _mm(t, side(x, y), NN, hs, hs), t_inv, a_t, akv)
    a_p = each(lambda x: x[:, 0:LANES], au)
    u0 = each(lambda x: x[:, LANES:], au)
    ry = each(lambda x, y: _mm(x, y, NN, hy, hy), a_rb, au)
    r_p = each(lambda r, x: r + x[:, 0:LANES], r_t, ry)
    y0 = each(lambda x, y, z: _mm(x, y, NN, hy, hy) + z[:, LANES:], a_rk, v_s, ry)
    m_c = each(lambda e, x, y: jnp.where(eye, jnp.exp(e), 0.0) + _mm(x, y, TN, hs, hs), cum_end, a_p, b_p)
    n_c = each(lambda x, y, z, w: _mm(x, y, TN, hs, hs) + _mm(z, w, TN, hs, hs), u0, b_p, v_s, k_p)

    s_in = [s_scr[p] for p in range(len(probs))]
    y_st = each(lambda r, s, y: _mm(r, s, NT, hy, hy) + y, r_p, s_in, y0)
    s_out = each(lambda s, m, n: _mm(s, m, NN, hs, hs) + n, s_in, m_c, n_c)
    for p, (i, j) in enumerate(probs):
        y_ref[i, :, j * LANES:(j + 1) * LANES] = y_st[p][0:c, :] + y_st[p][c:c2, :]
        s_scr[p] = s_out[p]

    @pl.when(ci_ == pl.num_programs(1) - 1)
    def _():
        for p, (i, j) in enumerate(probs):
            st_ref[i, PAIR * j] = s_out[p][0:HEAD_DIM, 0:HEAD_DIM]
            st_ref[i, PAIR * j + 1] = s_out[p][HEAD_DIM:, HEAD_DIM:]


def _rscan(r, k, v, kap, b, lw, s0, *, hs=2, hc=1, hy=1):
    bsz, t, d_a = r.shape
    npair = d_a // LANES
    bb = 2 if bsz % 2 == 0 else 1
    seq = pl.BlockSpec((bb, CHUNK, d_a), lambda g, ci: (g, ci, 0))
    st = pl.BlockSpec((bb, PAIR * npair, HEAD_DIM, HEAD_DIM), lambda g, ci: (g, 0, 0, 0))
    kern = functools.partial(_rscan_kernel, bb=bb, npair=npair, hs=hs, hc=hc, hy=hy)
    return pl.pallas_call(
        kern,
        grid=(bsz // bb, t // CHUNK),
        in_specs=[seq] * 6 + [st],
        out_specs=[seq, st],
        out_shape=[jax.ShapeDtypeStruct((bsz, t, d_a), F32),
                   jax.ShapeDtypeStruct(s0.shape, F32)],
        scratch_shapes=[pltpu.VMEM((bb * npair, LANES, LANES), F32)],
        compiler_params=_params(("arbitrary", "arbitrary")),
        name="rscan",
    )(r, k, v, kap, b, lw, s0)


def _lanes(x, n):
    return x if n == LANES else jnp.concatenate([x] * (n // LANES), axis=1)


def _attn_kernel(q_ref, k_ref, v_ref, kb_ref, o_ref, kaug_scr, vb_scr, s_scr, p_scr, m_scr, l_scr, acc_scr,
                 *, tq, strip, ns):
    qi = pl.program_id(2)
    tk = tq
    rows2 = 2 * tq
    streams = range(ns)
    lanes_of = lambda st: slice(st * LANES, (st + 1) * LANES)

    @pl.when(qi == 0)
    def _():
        p0, p1, p2 = _pieces(kb_ref[...], 3)
        which = _mod_pow2(_iota(p0.shape, 1), BIAS_GROUP)
        bias = jnp.where((which == 0) | (which == 3), p0, jnp.where((which == 1) | (which == 4), p1, p2))
        for st in streams:
            kaug_scr[st, :, 0:LANES] = k_ref[:, lanes_of(st)].astype(BF16)
            kaug_scr[st, :, LANES:2 * LANES] = bias
            vb_scr[st] = v_ref[:, lanes_of(st)].astype(BF16)

    m0 = _iota((tq, LANES), 1) < HEAD_DIM
    head1 = _iota((rows2, LANES), 0) >= tq
    q_aug = []
    for st in streams:
        lane = _iota((rows2, LANES), 1) - BIAS_GROUP * (ns * pl.program_id(1) + st)
        pick = (lane >= jnp.where(head1, 3, 0)) & (lane < jnp.where(head1, 6, 3))
        q_aug.append(jnp.concatenate([_stack(q_ref[:, lanes_of(st)], m0), jnp.where(pick, 1.0, 0.0)],
                                     axis=1).astype(BF16))
    m_scr[...] = jnp.full_like(m_scr, NEG_BIG)
    l_scr[...] = jnp.zeros_like(l_scr)
    acc_scr[...] = jnp.zeros_like(acc_scr)

    def keys(ki):
        return pl.ds(pl.multiple_of(ki * tk, tk), tk)

    def tile(ki, diagonal):
        for st in streams:
            s_scr[st] = lax.dot_general(q_aug[st], kaug_scr[st, keys(ki), :], NT, preferred_element_type=F32)
        for st in streams:
            for r0 in range(0, rows2, strip):
                rs = slice(r0, r0 + strip)
                s = s_scr[st, rs, :]
                if diagonal:
                    s = jnp.where(_iota((strip, tk), 1) <= _iota((strip, tk), 0) + r0 % tq, s, NEG_BIG)
                m_old = m_scr[st, rs, :]
                m_new = jnp.maximum(m_old, jnp.max(s, axis=-1, keepdims=True))
                alpha = jnp.exp(m_old - m_new)
                p = jnp.exp(s - _lanes(m_new, tk))
                l_scr[st, rs, :] = alpha * l_scr[st, rs, :] + jnp.sum(p, axis=-1, keepdims=True)
                acc_scr[st, rs, :] = alpha * acc_scr[st, rs, :]
                p_scr[st, rs, :] = p.astype(BF16)
                m_scr[st, rs, :] = m_new
            acc_scr[st] += jnp.dot(p_scr[st], vb_scr[st, keys(ki), :], preferred_element_type=F32)

    def off_diagonal(ki, carry):
        tile(ki, False)
        return carry

    lax.fori_loop(0, qi, off_diagonal, 0)
    tile(qi, True)
    for st in streams:
        o = acc_scr[st] / l_scr[st]
        o_ref[:, lanes_of(st)] = jnp.where(m0, o[0:tq, :], o[tq:rows2, :])


def _attn(q, k, v, kb, *, tq):
    bsz, t, d_b = q.shape
    npair = d_b // LANES
    ns = 2 if npair % 2 == 0 else 1
    nq = t // tq
    qspec = pl.BlockSpec((None, tq, ns * LANES), lambda b, j, qi: (b, qi, j))
    kspec = pl.BlockSpec((None, t, ns * LANES), lambda b, j, qi: (b, 0, j))
    kbspec = pl.BlockSpec((None, t, LANES), lambda b, j, qi: (b, 0, 0))
    return pl.pallas_call(
        functools.partial(_attn_kernel, tq=tq, strip=min(64, tq), ns=ns),
        grid=(bsz, npair // ns, nq),
        in_specs=[qspec, kspec, kspec, kbspec],
        out_specs=qspec,
        out_shape=jax.ShapeDtypeStruct((bsz, t, d_b), F32),
        scratch_shapes=[pltpu.VMEM((ns, t, 2 * LANES), BF16), pltpu.VMEM((ns, t, LANES), BF16),
                        pltpu.VMEM((ns, 2 * tq, tq), F32), pltpu.VMEM((ns, 2 * tq, tq), BF16),
                        pltpu.VMEM((ns, 2 * tq, LANES), F32), pltpu.VMEM((ns, 2 * tq, LANES), F32),
                        pltpu.VMEM((ns, 2 * tq, LANES), F32)],
        compiler_params=_params(("arbitrary",) * 3),
        name="attn",
    )(q, k, v, kb)


def _dattn_kernel(pt_ref, q_ref, cn_ref, kn_ref, vn_ref, bn_ref, kc_hbm, vc_hbm, lf_hbm, o_ref,
                  kbuf, vbuf, lbuf, sems, wq_scr, m_scr, l_scr, acc_scr, suf_scr,
                  *, n_heads, t_new, page, pg, n_pages):
    b = pl.program_id(0)
    step = pl.program_id(1)
    n_b = pl.num_programs(0)
    n_s = pl.num_programs(1)
    n_rows = n_heads * t_new
    slot = (b * n_s + step) & 1

    def page_copies(bi, si, sl):
        out = []
        for i in range(pg):
            src = pt_ref[bi, n_pages - 1 - (si * pg + i)]
            out.append(pltpu.make_async_copy(kc_hbm.at[src], kbuf.at[sl, i], sems.at[0, sl]))
            out.append(pltpu.make_async_copy(vc_hbm.at[src], vbuf.at[sl, i], sems.at[1, sl]))
            out.append(pltpu.make_async_copy(lf_hbm.at[src], lbuf.at[sl, i], sems.at[2, sl]))
        return out

    @pl.when((b == 0) & (step == 0))
    def _():
        for c in page_copies(0, 0, 0):
            c.start()

    wraps = step + 1 == n_s

    @pl.when(jnp.logical_not(wraps & (b + 1 == n_b)))
    def _():
        for c in page_copies(jnp.where(wraps, b + 1, b), jnp.where(wraps, 0, step + 1), 1 - slot):
            c.start()

    @pl.when(step == 0)
    def _():
        q = q_ref[...]
        head = _div_pow2(_iota(q.shape, 1), HEAD_DIM)
        wq_scr[...] = jnp.concatenate([jnp.where(head == h, q, 0.0) for h in range(n_heads)],
                                      axis=0).astype(BF16)
        m_scr[...] = jnp.full_like(m_scr, NEG_BIG)
        l_scr[...] = jnp.zeros_like(l_scr)
        acc_scr[...] = jnp.zeros_like(acc_scr)
        suf_scr[...] = jnp.zeros_like(suf_scr)

    for c in page_copies(b, step, slot):
        c.wait()

    def update(s, vals_t):
        n = len(vals_t)
        m_old = m_scr[...]
        m_new = jnp.maximum(m_old, jnp.max(s, axis=-1, keepdims=True))
        alpha = jnp.exp(m_old - m_new)
        p = jnp.exp(s - _lanes(m_new, n * page))
        l_scr[...] = alpha * l_scr[...] + jnp.sum(p, axis=-1, keepdims=True)
        p = p.astype(BF16)
        pv = None
        for i in range(n):
            t = lax.dot_general(p[:, i * page:(i + 1) * page], vals_t[i].astype(BF16), NT,
                                preferred_element_type=F32)
            pv = t if pv is None else pv + t
        acc_scr[...] = _lanes(alpha, acc_scr.shape[1]) * acc_scr[...] + pv
        m_scr[...] = m_new

    wq = wq_scr[...]
    lfs = [lbuf[slot, i] for i in range(pg)]
    lf_all = jnp.concatenate(lfs, axis=0)
    later = jnp.where(_iota((page, page), 0) > _iota((page, page), 1), 1.0, 0.0).astype(BF16)
    within = _mm(lf_all, later, NN, pa=3, pb=1)
    carry = suf_scr[...]
    cn = jnp.broadcast_to(cn_ref[...], (n_rows, page))
    parts = []
    for i in range(pg):
        suf = within[i * n_heads:(i + 1) * n_heads, :] + carry
        carry = carry + jnp.sum(lfs[i], axis=-1, keepdims=True)
        bias = jnp.broadcast_to(suf[:, None, :], (n_heads, t_new, page)).reshape(n_rows, page)
        parts.append(jnp.dot(wq, kbuf[slot, i].astype(BF16), preferred_element_type=F32) + bias + cn)
    suf_scr[...] = carry
    update(jnp.concatenate(parts, axis=1), [vbuf[slot, i] for i in range(pg)])

    @pl.when(step == n_s - 1)
    def _():
        s_new = jnp.dot(wq, kn_ref[...].astype(BF16), preferred_element_type=F32)
        update(s_new + bn_ref[...], [vn_ref[...]])
        o = acc_scr[...] / _lanes(l_scr[...], acc_scr.shape[1])
        head = _div_pow2(_iota((t_new, o.shape[1]), 1), HEAD_DIM)
        out = jnp.zeros((t_new, o.shape[1]), F32)
        for h in range(n_heads):
            out = out + jnp.where(head == h, o[h * t_new:(h + 1) * t_new, :], 0.0)
        o_ref[...] = out


def _dattn(page_table, q, cn_col, cache_kt, cache_vt, cache_lf_t, kt_new, vt_new, bias_new, *, n_heads, t_new, pg):
    bsz, n_pages = page_table.shape
    _, d_b, page = cache_kt.shape
    n_rows = n_heads * t_new
    assert n_pages % pg == 0
    per_b = lambda *blk: pl.BlockSpec((None,) + blk, lambda b, s, pt: (b,) + (0,) * len(blk))
    hbm = pl.BlockSpec(memory_space=pl.ANY)
    grid_spec = pltpu.PrefetchScalarGridSpec(
        num_scalar_prefetch=1,
        grid=(bsz, n_pages // pg),
        in_specs=[per_b(t_new, d_b), per_b(n_rows, 1), per_b(d_b, page), per_b(d_b, page), per_b(n_rows, page),
                  hbm, hbm, hbm],
        out_specs=per_b(t_new, d_b),
        scratch_shapes=[pltpu.VMEM((2, pg, d_b, page), F32), pltpu.VMEM((2, pg, d_b, page), F32),
                        pltpu.VMEM((2, pg, n_heads, page), F32), pltpu.SemaphoreType.DMA((3, 2)),
                        pltpu.VMEM((n_rows, d_b), BF16), pltpu.VMEM((n_rows, LANES), F32),
                        pltpu.VMEM((n_rows, LANES), F32), pltpu.VMEM((n_rows, d_b), F32),
                        pltpu.VMEM((n_heads, 1), F32)],
    )
    kern = functools.partial(_dattn_kernel, n_heads=n_heads, t_new=t_new, page=page, pg=pg, n_pages=n_pages)
    return pl.pallas_call(
        kern,
        grid_spec=grid_spec,
        out_shape=jax.ShapeDtypeStruct((bsz, t_new, d_b), F32),
        compiler_params=_params(("arbitrary", "arbitrary")),
        name="dattn",
    )(page_table, q, cn_col, kt_new, vt_new, bias_new, cache_kt, cache_vt, cache_lf_t)


def _merge_kernel(x_ref, y_ref, r_ref, k_ref, v_ref, g_ref, yb_ref, ga_ref, gb_ref,
                  lnw_ref, lnb_ref, rk_ref, bd_ref, woa_ref, wob_ref, wout_ref, o_ref):
    inv_hd = 1.0 / HEAD_DIM
    y = y_ref[...]
    mean = _head_sum(y, bd_ref, 2) * inv_hd
    yc = y - mean
    var = _head_sum(yc * yc, bd_ref, 2) * inv_hd
    yn = yc * lax.rsqrt(var + LNX_EPS) * lnw_ref[...] + lnb_ref[...]
    v = v_ref[...]
    bonus = _head_sum(r_ref[...] * k_ref[...] * rk_ref[...], bd_ref, 2) * v
    ya = ((yn + bonus) * g_ref[...]).astype(BF16)
    pa = jnp.dot(ya, woa_ref[...], preferred_element_type=F32)
    pb = jnp.dot(yb_ref[...].astype(BF16), wob_ref[...], preferred_element_type=F32)
    merged = _sigmoid(ga_ref[...]) * pa + _sigmoid(gb_ref[...]) * pb
    o_ref[...] = x_ref[...] + jnp.dot(merged.astype(BF16), wout_ref[...], preferred_element_type=F32)


def _merge(x, y, r, k, v, g, yb, ga, gb, lnw, lnb, rk, bd, woa, wob, wout):
    n, d = x.shape
    d_a = y.shape[1]
    tm = _row_tile(n)
    row = lambda w: pl.BlockSpec((tm, w), lambda i: (i, 0))
    return pl.pallas_call(
        _merge_kernel,
        grid=(n // tm,),
        in_specs=[row(d)] + [row(d_a)] * 6 + [row(d), row(d)] + [_resident((1, d_a))] * 3
                 + [_resident(bd.shape), _resident(woa.shape), _resident(wob.shape), _resident(wout.shape)],
        out_specs=row(d),
        out_shape=jax.ShapeDtypeStruct((n, d), F32),
        compiler_params=_params(("arbitrary",)),
        name="merge",
    )(x, y, r, k, v, g, yb, ga, gb, lnw, lnb, rk, bd, woa, wob, wout)


def _prep_weights(lw, n_heads_a, n_heads_b):
    (ffn1_norm, ffn1_wg, ffn1_wu, ffn1_wd, mix_norm, w_in, shift_mu, w0, w2, a0, a2, g2, k_k, k_a, r_k,
     lnx_w, lnx_b, b_f, q_norm, k_norm, w_o_a, w_o_b, w_out, ffn2_norm, ffn2_wg, ffn2_wu, ffn2_wd) = lw
    d = w_in.shape[0]
    d_a = n_heads_a * HEAD_DIM
    d_b = n_heads_b * HEAD_DIM
    n_shift = shift_mu.shape[0]
    c4 = n_shift + 3 * d_b
    row = lambda t: t.reshape(1, -1)
    w_main = jnp.concatenate([w_in[:, :c4], w_in[:, c4 + n_heads_b:]], axis=1).astype(BF16)
    w_f = jnp.pad(w_in[:, c4:c4 + n_heads_b], ((0, 0), (0, LANES - n_heads_b))).astype(BF16)
    dl, al = w2.shape[0], a2.shape[0]
    assert dl == HEAD_DIM and al == HEAD_DIM and g2.shape[0] == LANES
    lora = jnp.zeros((LANES, 2 * d_a), F32).at[:dl, :d_a].set(w2).at[dl:, d_a:].set(a2)
    bd = jnp.kron(jnp.eye(max(d_a, d_b) // HEAD_DIM, dtype=F32), jnp.ones((HEAD_DIM, HEAD_DIM), F32)).astype(BF16)
    return dict(
        d=d, d_a=d_a, d_b=d_b, n_shift=n_shift, n_heads_a=n_heads_a, n_heads_b=n_heads_b,
        ffn1=(row(ffn1_norm), ffn1_wg.astype(BF16), ffn1_wu.astype(BF16), ffn1_wd.astype(BF16)),
        ffn2=(row(ffn2_norm), ffn2_wg.astype(BF16), ffn2_wu.astype(BF16), ffn2_wd.astype(BF16)),
        mix_norm=row(mix_norm), w_main=w_main, w_f=w_f,
        b_f=jnp.pad(b_f, (0, LANES - n_heads_b)).reshape(1, LANES),
        q_norm=row(jnp.tile(q_norm, n_heads_b)), k_norm=row(jnp.tile(k_norm, n_heads_b)), bd=bd,
        mu=row(shift_mu), w0=row(w0), a0=row(a0), lora=lora, g2=g2, k_k=row(k_k), k_a=row(k_a),
        r_k=row(r_k), lnx_w=row(lnx_w), lnx_b=row(lnx_b),
        w_o_a=w_o_a.astype(BF16), w_o_b=w_o_b.astype(BF16), w_out=w_out.astype(BF16),
    )


def _layer(x3, prev_shift, s0, past, w):
    bsz, t, d = x3.shape
    d_a, d_b, ha, hb = w["d_a"], w["d_b"], w["n_heads_a"], w["n_heads_b"]
    n = bsz * t
    x = x3.reshape(n, d)
    x1 = _ffn(x, *w["ffn1"])
    init_rows = prev_shift if t >= min(_row_tile(n), 256) else jnp.repeat(prev_shift, t, axis=0)
    q, k, v, g_a, g_b, logf, cum, nck, r, km, va, kap, b, lw, g, new_shift = _mixproj(
        x1, w["mix_norm"], w["w_main"], w["w_f"], w["b_f"], w["q_norm"], w["k_norm"], w["bd"],
        init_rows, w["mu"], w["w0"], w["a0"], w["lora"], w["g2"], w["k_k"], w["k_a"],
        n_shift=w["n_shift"], d_a=d_a, d_b=d_b, n_heads=hb, seq_len=t)

    t_pad = -(-t // CHUNK) * CHUNK
    seq = lambda a: jnp.pad(a.reshape(bsz, t, d_a), ((0, 0), (0, t_pad - t), (0, 0)))
    y_raw, new_s = _rscan(seq(r), seq(km), seq(va), seq(kap), seq(b), seq(lw), s0)
    y_raw = y_raw[:, :t].reshape(n, d_a)

    npair = d_b // LANES
    if past is None:
        tq = min(512, t)
        y_b = _attn(q.reshape(bsz, t, d_b), k.reshape(bsz, t, d_b), v.reshape(bsz, t, d_b),
                    nck.reshape(bsz, t, LANES), tq=tq)
    else:
        cache_k, cache_v, cache_lf, page_table = past
        n_pool, page = cache_k.shape[:2]
        cn = cum.reshape(bsz, t, hb)
        cn_col = cn.transpose(0, 2, 1).reshape(bsz, hb * t, 1)
        tpos = jnp.arange(t)
        ok = tpos[None, :] <= tpos[:, None]
        bn = cn.transpose(0, 2, 1)[:, :, :, None] - cn.transpose(0, 2, 1)[:, :, None, :]
        bn = jnp.where(ok[None, None], bn, NEG_BIG).reshape(bsz, hb * t, t)
        bn = jnp.pad(bn, ((0, 0), (0, 0), (0, page - t)), constant_values=NEG_BIG)
        pages_t = lambda c: c.transpose(0, 2, 3, 1).reshape(n_pool, d_b, page)
        new_t = lambda a: jnp.pad(a.reshape(bsz, t, d_b).transpose(0, 2, 1), ((0, 0), (0, 0), (0, page - t)))
        pg = next(g for g in (32, 16, 8, 4, 2, 1) if page_table.shape[1] % g == 0)
        y_b = _dattn(page_table, q.reshape(bsz, t, d_b), cn_col, pages_t(cache_k), pages_t(cache_v),
                     cache_lf.transpose(0, 2, 1), new_t(k), new_t(v), bn, n_heads=hb, t_new=t, pg=pg)
    y_b = y_b.reshape(n, d_b)

    x2 = _merge(x1, y_raw, r, km, va, g, y_b, g_a, g_b, w["lnx_w"], w["lnx_b"], w["r_k"], w["bd"],
                w["w_o_a"], w["w_o_b"], w["w_out"])
    x3o = _ffn(x2, *w["ffn2"]).reshape(bsz, t, d)
    return x3o, (k.reshape(bsz, t, hb, HEAD_DIM), v.reshape(bsz, t, hb, HEAD_DIM),
                 logf.reshape(bsz, t, hb), new_s, new_shift)


def kernel(x_prompt, x_sample, cache_k, cache_v, cache_logf, state_wkv, state_shift, page_table,
           ffn1_norm, ffn1_wg, ffn1_wu, ffn1_wd, mix_norm, w_in, shift_mu,
           rwkv_w0, rwkv_w2, rwkv_a0, rwkv_a2, rwkv_g2, rwkv_k_k, rwkv_k_a, rwkv_r_k, rwkv_lnx_w, rwkv_lnx_b,
           fox_b_f, fox_q_norm, fox_k_norm, w_o_a, w_o_b, w_out,
           ffn2_norm, ffn2_wg, ffn2_wu, ffn2_wd):
    depth = w_in.shape[0]
    n_heads_a = state_wkv.shape[2]
    n_heads_b = cache_k.shape[3]
    bp = x_prompt.shape[0]
    xp, xs = x_prompt, x_sample
    outs_p, outs_s = [], []
    for l in range(depth):
        lw = (ffn1_norm[l], ffn1_wg[l], ffn1_wu[l], ffn1_wd[l], mix_norm[l], w_in[l], shift_mu[l],
              rwkv_w0[l], rwkv_w2[l], rwkv_a0[l], rwkv_a2[l], rwkv_g2[l], rwkv_k_k[l], rwkv_k_a[l],
              rwkv_r_k[l].reshape(-1), rwkv_lnx_w[l], rwkv_lnx_b[l], fox_b_f[l], fox_q_norm[l], fox_k_norm[l],
              w_o_a[l], w_o_b[l], w_out[l], ffn2_norm[l], ffn2_wg[l], ffn2_wu[l], ffn2_wd[l])
        w = _prep_weights(lw, n_heads_a, n_heads_b)
        prev0 = jnp.zeros((bp, w["n_shift"]), xp.dtype)
        s00 = jnp.zeros((bp, n_heads_a, HEAD_DIM, HEAD_DIM), xp.dtype)
        xp, op = _layer(xp, prev0, s00, None, w)
        xs, os_ = _layer(xs, state_shift[l], state_wkv[l], (cache_k[l], cache_v[l], cache_logf[l], page_table), w)
        outs_p.append(op)
        outs_s.append(os_)
    stack = lambda outs, i: jnp.stack([o[i] for o in outs], 0)
    return ((xp, xs) + tuple(stack(outs_p, i) for i in range(5)) + tuple(stack(outs_s, i) for i in range(5)))
```

```python
import functools

import jax
import jax.numpy as jnp
from jax import lax
from jax.experimental import pallas as pl
from jax.experimental.pallas import tpu as pltpu

F32 = jnp.float32
BF16 = jnp.bfloat16

HEAD_DIM = 64
LANES = 128
PAIR = LANES // HEAD_DIM
NORM_EPS = 1e-6
LNX_EPS = 64e-5
DECAY_SCALE = 0.6065306597126334
NEG_BIG = -1e30
BIAS_GROUP = 16
CHUNK = 64
VMEM_LIMIT = 56 * 1024 * 1024

NN = (((1,), (0,)), ((), ()))
NT = (((1,), (1,)), ((), ()))
TN = (((0,), (0,)), ((), ()))


def _pieces(x, n):
    if x.dtype == BF16:
        return [x]
    out = []
    rem = x
    for i in range(n):
        p = rem.astype(BF16)
        out.append(p)
        if i + 1 < n:
            rem = rem - p.astype(F32)
    return out


def _mm(a, b, dims=NN, pa=1, pb=1):
    pa_l = _pieces(a, pa)
    pb_l = _pieces(b, pb)
    if len(pa_l) == 2 and len(pb_l) == 2:
        m_axis = 1 - dims[0][0][0]
        m = a.shape[m_axis]
        both = lax.dot_general(jnp.concatenate(pa_l, axis=m_axis), pb_l[0], dims, preferred_element_type=F32)
        return both[0:m] + both[m:] + lax.dot_general(pa_l[0], pb_l[1], dims, preferred_element_type=F32)
    order = max(len(pa_l), len(pb_l))
    acc = None
    for i in reversed(range(len(pa_l))):
        for j in reversed(range(len(pb_l))):
            if i + j >= order:
                continue
            t = lax.dot_general(pa_l[i], pb_l[j], dims, preferred_element_type=F32)
            acc = t if acc is None else acc + t
    return acc


def _sigmoid(x):
    return 1.0 / (1.0 + jnp.exp(-x))


def _rms(x, g):
    return x * lax.rsqrt(jnp.mean(x * x, axis=-1, keepdims=True) + NORM_EPS) * g


def _iota(shape, dim):
    return lax.broadcasted_iota(jnp.int32, shape, dim)


def _div_pow2(x, n):
    assert n & (n - 1) == 0
    return lax.shift_right_logical(x, n.bit_length() - 1)


def _mod_pow2(x, n):
    assert n & (n - 1) == 0
    return x & (n - 1)


def _resident(shape):
    nd = len(shape)
    return pl.BlockSpec(shape, lambda *_: (0,) * nd, pipeline_mode=pl.Buffered(1))


def _params(sem):
    return pltpu.CompilerParams(dimension_semantics=sem, vmem_limit_bytes=VMEM_LIMIT)


def _row_tile(n):
    for t in (512, 256, 128, 64, 32, 16, 8):
        if n % t == 0:
            return t
    raise ValueError(f"row count {n} is not a multiple of 8")


def _ffn_kernel(x_ref, g_ref, wg_ref, wu_ref, wd_ref, o_ref):
    x = x_ref[...]
    h = _rms(x, g_ref[...]).astype(BF16)
    gate = jnp.dot(h, wg_ref[...], preferred_element_type=F32)
    up = jnp.dot(h, wu_ref[...], preferred_element_type=F32)
    act = (gate * _sigmoid(gate) * up).astype(BF16)
    o_ref[...] = x + 0.5 * jnp.dot(act, wd_ref[...], preferred_element_type=F32)


def _ffn(x, g, wg, wu, wd):
    n, d = x.shape
    dff = wg.shape[1]
    tm = _row_tile(n)
    return pl.pallas_call(
        _ffn_kernel,
        grid=(n // tm,),
        in_specs=[pl.BlockSpec((tm, d), lambda i: (i, 0)), _resident((1, d)),
                  _resident((d, dff)), _resident((d, dff)), _resident((dff, d))],
        out_specs=pl.BlockSpec((tm, d), lambda i: (i, 0)),
        out_shape=jax.ShapeDtypeStruct((n, d), F32),
        compiler_params=_params(("arbitrary",)),
        name="ffn",
    )(x, g, wg, wu, wd)


def _head_sum(x, bd_ref, pieces):
    return _mm(x, bd_ref[...], NN, pa=pieces, pb=1)


def _rwkv_prep(p, i, init_ref, mu_ref, w0_ref, a0_ref, lora_ref, g2_ref, kk_ref, ka_ref, bd_ref,
               r_ref, k_ref, v_ref, kap_ref, b_ref, lw_ref, g_ref, last_ref, *, d_a, seq_len, tm):
    prev = pltpu.roll(p, 1, 0)
    rowi = _iota(p.shape, 0)
    if seq_len >= tm:
        tiles_per_seq = seq_len // tm

        @pl.when(i % tiles_per_seq == 0)
        def _():
            last_ref[...] = init_ref[...]

        prev = jnp.where(rowi == 0, last_ref[...], prev)
        last_ref[...] = p[tm - 1:tm, :]
    else:
        prev = jnp.where(_mod_pow2(rowi, seq_len) == 0, init_ref[...], prev)
    xm = p + (prev - p) * mu_ref[...]
    r = xm[:, 0:d_a]
    k = xm[:, d_a:2 * d_a]
    v = xm[:, 2 * d_a:3 * d_a]
    xwa = xm[:, 3 * d_a:3 * d_a + LANES]
    xg = xm[:, 3 * d_a + LANES:3 * d_a + 2 * LANES]
    xwa = jnp.where(_iota(xwa.shape, 1) < HEAD_DIM, jnp.tanh(xwa), xwa)
    lo = _mm(xwa, lora_ref[...], NN, pa=2, pb=2)
    lw_ref[...] = -DECAY_SCALE * _sigmoid(w0_ref[...] + lo[:, 0:d_a])
    a = _sigmoid(a0_ref[...] + lo[:, d_a:2 * d_a])
    g_ref[...] = _mm(_sigmoid(xg), g2_ref[...], NN, pa=2, pb=2)
    kk = k * kk_ref[...]
    ss = _head_sum(kk * kk, bd_ref, 2)
    kk = kk * lax.rsqrt(jnp.maximum(ss, 1e-24))
    r_ref[...] = r
    k_ref[...] = k * (1.0 + (a - 1.0) * ka_ref[...])
    v_ref[...] = v
    kap_ref[...] = kk
    b_ref[...] = kk * a


def _mixproj_kernel(x_ref, g_ref, w_ref, wf_ref, bf_ref, qn_ref, kn_ref, bd_ref, *rest,
                    n_shift, d_a, d_b, d_model, n_heads, seq_len, tm):
    prep_in = rest[0:8]
    q_ref, k_ref, v_ref, ga_ref, gb_ref, lf_ref, cum_ref, nck_ref = rest[8:16]
    prep_out = rest[16:23]
    sh_ref, carry_ref, last_ref = rest[23:]
    i = pl.program_id(0)
    h = _rms(x_ref[...], g_ref[...]).astype(BF16)
    c1 = n_shift
    c2 = c1 + d_b
    c3 = c2 + d_b
    c4 = c3 + d_b
    c5 = c4 + d_model
    c6 = c5 + d_model
    p_a = jnp.dot(h, w_ref[:, 0:c1], preferred_element_type=F32)
    q = jnp.dot(h, w_ref[:, c1:c2], preferred_element_type=F32)
    k = jnp.dot(h, w_ref[:, c2:c3], preferred_element_type=F32)
    v_ref[...] = jnp.dot(h, w_ref[:, c3:c4], preferred_element_type=F32)
    ga_ref[...] = jnp.dot(h, w_ref[:, c4:c5], preferred_element_type=F32)
    gb_ref[...] = jnp.dot(h, w_ref[:, c5:c6], preferred_element_type=F32)
    inv_hd = 1.0 / HEAD_DIM
    q = q * lax.rsqrt(_head_sum(q * q, bd_ref, 2) * inv_hd + NORM_EPS) * qn_ref[...]
    k = k * lax.rsqrt(_head_sum(k * k, bd_ref, 2) * inv_hd + NORM_EPS) * kn_ref[...]
    q_ref[...] = q * (1.0 / float(HEAD_DIM) ** 0.5)
    k_ref[...] = k
    z = jnp.dot(h, wf_ref[...], preferred_element_type=F32) + bf_ref[...]
    logf = jnp.minimum(z, 0.0) - jnp.log1p(jnp.exp(-jnp.abs(z)))
    logf = jnp.where(_iota(logf.shape, 1) < n_heads, logf, 0.0)
    row = _iota((tm, tm), 0)
    col = _iota((tm, tm), 1)
    keep = col <= row
    if seq_len < tm:
        keep = keep & (_div_pow2(row, seq_len) == _div_pow2(col, seq_len))
    tri = jnp.where(keep, 1.0, 0.0).astype(BF16)
    cum = _mm(tri, logf, NN, pa=1, pb=3)
    if seq_len > tm:
        tiles_per_seq = seq_len // tm

        @pl.when(i % tiles_per_seq == 0)
        def _():
            carry_ref[...] = jnp.zeros_like(carry_ref)

        cum = cum + carry_ref[...]
        carry_ref[...] = cum[tm - 1:tm, :]
    lf_ref[...] = logf[:, :n_heads]
    cum_ref[...] = cum[:, :n_heads]
    src = _iota((LANES, LANES), 0)
    dst = _iota((LANES, LANES), 1)
    sub = _mod_pow2(dst, BIAS_GROUP)
    hit = (_div_pow2(dst, BIAS_GROUP) == _div_pow2(src, PAIR)) & (sub < 3 * PAIR) & (src < n_heads) \
        & ((sub >= 3) == (_mod_pow2(src, PAIR) == 1))
    nck_ref[...] = _mm(cum, jnp.where(hit, -1.0, 0.0).astype(BF16), NN, pa=3, pb=1)
    _rwkv_prep(p_a, i, *prep_in, bd_ref, *prep_out, last_ref, d_a=d_a, seq_len=seq_len, tm=tm)
    if seq_len >= tm:
        sh_ref[...] = p_a[tm - 1:tm, :]
    else:
        n_seq = tm // seq_len
        last = _iota((n_seq, tm), 1) == _iota((n_seq, tm), 0) * seq_len + (seq_len - 1)
        sh_ref[...] = _mm(jnp.where(last, 1.0, 0.0).astype(BF16), p_a, NN, pa=1, pb=3)


def _mixproj(x, g, w_main, w_f, b_f, qn, kn, bd, init_rows, mu, w0, a0, lora, g2, k_k, k_a,
             *, n_shift, d_a, d_b, n_heads, seq_len):
    n, d = x.shape
    tm = min(_row_tile(n), 256)
    assert seq_len % tm == 0 or tm % seq_len == 0
    row = lambda w: pl.BlockSpec((tm, w), lambda i: (i, 0))
    shp = lambda w: jax.ShapeDtypeStruct((n, w), F32)
    n_seq = n // seq_len
    if seq_len >= tm:
        tps = seq_len // tm
        init_spec = pl.BlockSpec((None, 1, n_shift), lambda i: (i // tps, 0, 0))
        init_rows = init_rows.reshape(n_seq, 1, n_shift)
        sh_spec = pl.BlockSpec((None, 1, n_shift), lambda i: (i // tps, 0, 0))
        sh_shape = jax.ShapeDtypeStruct((n_seq, 1, n_shift), F32)
    else:
        init_spec = row(n_shift)
        sh_spec = pl.BlockSpec((tm // seq_len, n_shift), lambda i: (i, 0))
        sh_shape = jax.ShapeDtypeStruct((n_seq, n_shift), F32)
    kern = functools.partial(_mixproj_kernel, n_shift=n_shift, d_a=d_a, d_b=d_b, d_model=d, n_heads=n_heads,
                             seq_len=seq_len, tm=tm)
    outs = pl.pallas_call(
        kern,
        grid=(n // tm,),
        in_specs=[row(d), _resident((1, d)), _resident(w_main.shape), _resident(w_f.shape),
                  _resident((1, LANES)), _resident((1, d_b)), _resident((1, d_b)), _resident(bd.shape),
                  init_spec, _resident((1, n_shift)), _resident((1, d_a)), _resident((1, d_a)),
                  _resident(lora.shape), _resident(g2.shape), _resident((1, d_a)), _resident((1, d_a))],
        out_specs=[row(d_b), row(d_b), row(d_b), row(d), row(d), row(n_heads), row(n_heads), row(LANES)]
                  + [row(d_a)] * 7 + [sh_spec],
        out_shape=[shp(d_b), shp(d_b), shp(d_b), shp(d), shp(d), shp(n_heads), shp(n_heads), shp(LANES)]
                  + [shp(d_a)] * 7 + [sh_shape],
        scratch_shapes=[pltpu.VMEM((1, LANES), F32), pltpu.VMEM((1, n_shift), F32)],
        compiler_params=_params(("arbitrary",)),
        name="mixproj",
    )(x, g, w_main, w_f, b_f, qn, kn, bd, init_rows, mu, w0, a0, lora, g2, k_k, k_a)
    return tuple(outs[:-1]) + (outs[-1].reshape(n_seq, n_shift),)


def _stack(x, m0):
    return jnp.concatenate([jnp.where(m0, x, 0.0), jnp.where(m0, 0.0, x)], axis=0)


def _rscan_kernel(r_ref, k_ref, v_ref, kap_ref, b_ref, lw_ref, s0_ref, y_ref, st_ref, s_scr, *, bb, npair, hs, hc, hy):
    c = CHUNK
    c2 = PAIR * c
    ci_ = pl.program_id(1)
    probs = [(i, j) for i in range(bb) for j in range(npair)]
    each = lambda f, *ls: [f(*xs) for xs in zip(*ls)]

    @pl.when(ci_ == 0)
    def _():
        zero = jnp.zeros((HEAD_DIM, HEAD_DIM), F32)
        for p, (i, j) in enumerate(probs):
            top = jnp.concatenate([s0_ref[i, PAIR * j], zero], axis=1)
            bot = jnp.concatenate([zero, s0_ref[i, PAIR * j + 1]], axis=1)
            s_scr[p] = jnp.concatenate([top, bot], axis=0)

    m0 = _iota((c, LANES), 1) < HEAD_DIM
    ri = _iota((c2, c2), 0)
    ci = _iota((c2, c2), 1)
    same = _div_pow2(ri, c) == _div_pow2(ci, c)
    strict = same & (ci < ri)
    incl = same & (ci <= ri)
    eye = ri == ci
    tri_c = jnp.where(_iota((c, c), 1) <= _iota((c, c), 0), 1.0, 0.0).astype(BF16)
    ld = lambda ref: [ref[i, :, j * LANES:(j + 1) * LANES] for (i, j) in probs]

    lw = ld(lw_ref)
    cum = each(lambda x: _mm(tri_c, x, NN, pa=1, pb=3), lw)
    cum_end = each(lambda x: x[c - 1:c, :], cum)
    e_neg = each(lambda x: jnp.exp(-x), cum)
    e_rem = each(lambda x, e: jnp.exp(e - x), cum, cum_end)
    kap, b, k = ld(kap_ref), ld(b_ref), ld(k_ref)
    a_t = each(lambda kp, x, w: _stack(-kp * jnp.exp(x - w), m0), kap, cum, lw)
    r_t = each(lambda r, x: _stack(r * jnp.exp(x), m0), ld(r_ref), cum)
    b_t = each(lambda x, e: _stack(x * e, m0), b, e_neg)
    k_t = each(lambda x, e: _stack(x * e, m0), k, e_neg)
    b_p = each(lambda x, e: _stack(x * e, m0), b, e_rem)
    k_p = each(lambda x, e: _stack(x * e, m0), k, e_rem)
    v_s = each(lambda x: _stack(x, m0), ld(v_ref))

    side = lambda x, y: jnp.concatenate([x, y], axis=1)
    bk_t = each(lambda x, y: jnp.concatenate([x, y], axis=0), b_t, k_t)
    sc_a = each(lambda x, y: _mm(x, y, NT, hc, hc), a_t, bk_t)
    sc_r = each(lambda x, y: _mm(x, y, NT, hy, hy), r_t, bk_t)
    l_ab = each(lambda x: jnp.where(strict, x[:, 0:c2], 0.0), sc_a)
    a_ak = each(lambda x: jnp.where(strict, x[:, c2:], 0.0), sc_a)
    a_rb = each(lambda x: jnp.where(incl, x[:, 0:c2], 0.0), sc_r)
    a_rk = each(lambda x: jnp.where(incl, x[:, c2:], 0.0), sc_r)

    t_inv = each(lambda x: jnp.where(eye, 1.0, 0.0) + x, l_ab)
    l_pow = each(lambda x: _mm(x, x, NN, hc, hc), l_ab)
    n_sq = c.bit_length() - 2
    for i in range(1, n_sq + 1):
        if i < n_sq:
            both = each(lambda x, t: _mm(jnp.concatenate([x, t], axis=0), x, NN, hc, hc), l_pow, t_inv)
            l_pow = each(lambda x: x[0:c2], both)
            t_inv = each(lambda t, x: t + x[c2:], t_inv, both)
        else:
            t_inv = each(lambda t, x: t + _mm(t, x, NN, hc, hc), t_inv, l_pow)

    akv = each(lambda x, y: _mm(x, y, NN, hc, hc), a_ak, v_s)
    au = each(lambda t, x, y: _mm(t, side(x, y), NN, hc, hc), t_inv, a_t, akv)
    a_p = each(lambda x: x[:, 0:LANES], au)
    u0 = each(lambda x: x[:, LANES:], au)
    ry = each(lambda x, y: _mm(x, y, NN, hy, hy), a_rb, au)
    r_p = each(lambda r, x: r + x[:, 0:LANES], r_t, ry)
    y0 = each(lambda x, y, z: _mm(x, y, NN, hy, hy) + z[:, LANES:], a_rk, v_s, ry)
    m_c = each(lambda e, x, y: jnp.where(eye, jnp.exp(e), 0.0) + _mm(x, y, TN, hc, hc), cum_end, a_p, b_p)
    n_c = each(lambda x, y, z, w: _mm(x, y, TN, hs, hs) + _mm(z, w, TN, hs, hs), u0, b_p, v_s, k_p)

    s_in = [s_scr[p] for p in range(len(probs))]
    y_st = each(lambda r, s, y: _mm(r, s, NT, hy, hy) + y, r_p, s_in, y0)
    s_out = each(lambda s, m, n: _mm(s, m, NN, hs, hs) + n, s_in, m_c, n_c)
    for p, (i, j) in enumerate(probs):
        y_ref[i, :, j * LANES:(j + 1) * LANES] = y_st[p][0:c, :] + y_st[p][c:c2, :]
        s_scr[p] = s_out[p]

    @pl.when(ci_ == pl.num_programs(1) - 1)
    def _():
        for p, (i, j) in enumerate(probs):
            st_ref[i, PAIR * j] = s_out[p][0:HEAD_DIM, 0:HEAD_DIM]
            st_ref[i, PAIR * j + 1] = s_out[p][HEAD_DIM:, HEAD_DIM:]


def _rscan(r, k, v, kap, b, lw, s0, *, hs=2, hc=1, hy=1):
    bsz, t, d_a = r.shape
    npair = d_a // LANES
    bb = 2 if bsz % 2 == 0 else 1
    seq = pl.BlockSpec((bb, CHUNK, d_a), lambda g, ci: (g, ci, 0))
    st = pl.BlockSpec((bb, PAIR * npair, HEAD_DIM, HEAD_DIM), lambda g, ci: (g, 0, 0, 0))
    kern = functools.partial(_rscan_kernel, bb=bb, npair=npair, hs=hs, hc=hc, hy=hy)
    return pl.pallas_call(
        kern,
        grid=(bsz // bb, t // CHUNK),
        in_specs=[seq] * 6 + [st],
        out_specs=[seq, st],
        out_shape=[jax.ShapeDtypeStruct((bsz, t, d_a), F32),
                   jax.ShapeDtypeStruct(s0.shape, F32)],
        scratch_shapes=[pltpu.VMEM((bb * npair, LANES, LANES), F32)],
        compiler_params=_params(("arbitrary", "arbitrary")),
        name="rscan",
    )(r, k, v, kap, b, lw, s0)


def _lanes(x, n):
    return x if n == LANES else jnp.concatenate([x] * (n // LANES), axis=1)


def _attn_kernel(q_ref, k_ref, v_ref, kb_ref, o_ref, kaug_scr, vb_scr, s_scr, p_scr, m_scr, acc_scr,
                 *, tq, strip, ns):
    qi = pl.program_id(2)
    tk = tq
    rows2 = 2 * tq
    streams = range(ns)
    lanes_of = lambda st: slice(st * LANES, (st + 1) * LANES)

    @pl.when(qi == 0)
    def _():
        p0, p1, p2 = _pieces(kb_ref[...], 3)
        which = _mod_pow2(_iota(p0.shape, 1), BIAS_GROUP)
        bias = jnp.where((which == 0) | (which == 3), p0, jnp.where((which == 1) | (which == 4), p1, p2))
        for st in streams:
            kaug_scr[st, :, 0:LANES] = k_ref[:, lanes_of(st)].astype(BF16)
            kaug_scr[st, :, LANES:2 * LANES] = bias
            vb_scr[st, :, 0:LANES] = v_ref[:, lanes_of(st)].astype(BF16)
            vb_scr[st, :, LANES:2 * LANES] = jnp.ones((vb_scr.shape[1], LANES), BF16)

    m0 = _iota((tq, LANES), 1) < HEAD_DIM
    head1 = _iota((rows2, LANES), 0) >= tq
    q_aug = []
    for st in streams:
        lane = _iota((rows2, LANES), 1) - BIAS_GROUP * (ns * pl.program_id(1) + st)
        pick = (lane >= jnp.where(head1, 3, 0)) & (lane < jnp.where(head1, 6, 3))
        q_aug.append(jnp.concatenate([_stack(q_ref[:, lanes_of(st)], m0), jnp.where(pick, 1.0, 0.0)],
                                     axis=1).astype(BF16))
    m_scr[...] = jnp.full_like(m_scr, NEG_BIG)
    acc_scr[...] = jnp.zeros_like(acc_scr)

    def keys(ki):
        return pl.ds(pl.multiple_of(ki * tk, tk), tk)

    def tile(ki, diagonal):
        for st in streams:
            s_scr[st] = lax.dot_general(q_aug[st], kaug_scr[st, keys(ki), :], NT, preferred_element_type=F32)
        for st in streams:
            for r0 in range(0, rows2, strip):
                rs = slice(r0, r0 + strip)
                s = s_scr[st, rs, :]
                if diagonal:
                    s = jnp.where(_iota((strip, tk), 1) <= _iota((strip, tk), 0) + r0 % tq, s, NEG_BIG)
                m_old = m_scr[st, rs, :]
                m_new = jnp.maximum(m_old, jnp.max(s, axis=-1, keepdims=True))
                alpha = jnp.exp(m_old - m_new)
                acc_scr[st, rs, :] = _lanes(alpha, 2 * LANES) * acc_scr[st, rs, :]
                p_scr[st, rs, :] = jnp.exp((s - _lanes(m_new, tk)).astype(BF16))
                m_scr[st, rs, :] = m_new
            acc_scr[st] += jnp.dot(p_scr[st], vb_scr[st, keys(ki), :], preferred_element_type=F32)

    def off_diagonal(ki, carry):
        tile(ki, False)
        return carry

    lax.fori_loop(0, qi, off_diagonal, 0)
    tile(qi, True)
    for st in streams:
        o = acc_scr[st, :, 0:LANES] / acc_scr[st, :, LANES:2 * LANES]
        o_ref[:, lanes_of(st)] = jnp.where(m0, o[0:tq, :], o[tq:rows2, :])


def _attn(q, k, v, kb, *, tq):
    bsz, t, d_b = q.shape
    npair = d_b // LANES
    ns = 2 if npair % 2 == 0 else 1
    nq = t // tq
    qspec = pl.BlockSpec((None, tq, ns * LANES), lambda b, j, qi: (b, qi, j))
    kspec = pl.BlockSpec((None, t, ns * LANES), lambda b, j, qi: (b, 0, j))
    kbspec = pl.BlockSpec((None, t, LANES), lambda b, j, qi: (b, 0, 0))
    return pl.pallas_call(
        functools.partial(_attn_kernel, tq=tq, strip=min(64, tq), ns=ns),
        grid=(bsz, npair // ns, nq),
        in_specs=[qspec, kspec, kspec, kbspec],
        out_specs=qspec,
        out_shape=jax.ShapeDtypeStruct((bsz, t, d_b), F32),
        scratch_shapes=[pltpu.VMEM((ns, t, 2 * LANES), BF16), pltpu.VMEM((ns, t, 2 * LANES), BF16),
                        pltpu.VMEM((ns, 2 * tq, tq), F32), pltpu.VMEM((ns, 2 * tq, tq), BF16),
                        pltpu.VMEM((ns, 2 * tq, LANES), F32), pltpu.VMEM((ns, 2 * tq, 2 * LANES), F32)],
        compiler_params=_params(("arbitrary",) * 3),
        name="attn",
    )(q, k, v, kb)


def _dattn_kernel(pt_ref, q_ref, cn_ref, kn_ref, vn_ref, bn_ref, kc_hbm, vc_hbm, lf_hbm, o_ref,
                  kbuf, vbuf, lbuf, sems, wq_scr, m_scr, l_scr, acc_scr, suf_scr,
                  *, n_heads, t_new, page, pg, n_pages):
    b = pl.program_id(0)
    step = pl.program_id(1)
    n_b = pl.num_programs(0)
    n_s = pl.num_programs(1)
    n_rows = n_heads * t_new
    slot = (b * n_s + step) & 1

    def page_copies(bi, si, sl):
        out = []
        for i in range(pg):
            src = pt_ref[bi, n_pages - 1 - (si * pg + i)]
            out.append(pltpu.make_async_copy(kc_hbm.at[src], kbuf.at[sl, i], sems.at[0, sl]))
            out.append(pltpu.make_async_copy(vc_hbm.at[src], vbuf.at[sl, i], sems.at[1, sl]))
            out.append(pltpu.make_async_copy(lf_hbm.at[src], lbuf.at[sl, i], sems.at[2, sl]))
        return out

    @pl.when((b == 0) & (step == 0))
    def _():
        for c in page_copies(0, 0, 0):
            c.start()

    wraps = step + 1 == n_s

    @pl.when(jnp.logical_not(wraps & (b + 1 == n_b)))
    def _():
        for c in page_copies(jnp.where(wraps, b + 1, b), jnp.where(wraps, 0, step + 1), 1 - slot):
            c.start()

    @pl.when(step == 0)
    def _():
        q = q_ref[...]
        head = _div_pow2(_iota(q.shape, 1), HEAD_DIM)
        wq_scr[...] = jnp.concatenate([jnp.where(head == h, q, 0.0) for h in range(n_heads)],
                                      axis=0).astype(BF16)
        m_scr[...] = jnp.full_like(m_scr, NEG_BIG)
        l_scr[...] = jnp.zeros_like(l_scr)
        acc_scr[...] = jnp.zeros_like(acc_scr)
        suf_scr[...] = jnp.zeros_like(suf_scr)

    for c in page_copies(b, step, slot):
        c.wait()

    def update(s, vals_t):
        n = len(vals_t)
        m_old = m_scr[...]
        m_new = jnp.maximum(m_old, jnp.max(s, axis=-1, keepdims=True))
        alpha = jnp.exp(m_old - m_new)
        p = jnp.exp(s - _lanes(m_new, n * page))
        l_scr[...] = alpha * l_scr[...] + jnp.sum(p, axis=-1, keepdims=True)
        p = p.astype(BF16)
        pv = None
        for i in range(n):
            t = lax.dot_general(p[:, i * page:(i + 1) * page], vals_t[i].astype(BF16), NT,
                                preferred_element_type=F32)
            pv = t if pv is None else pv + t
        acc_scr[...] = _lanes(alpha, acc_scr.shape[1]) * acc_scr[...] + pv
        m_scr[...] = m_new

    wq = wq_scr[...]
    lfs = [lbuf[slot, i] for i in range(pg)]
    lf_all = jnp.concatenate(lfs, axis=0)
    later = jnp.where(_iota((page, page), 0) > _iota((page, page), 1), 1.0, 0.0).astype(BF16)
    within = _mm(lf_all, later, NN, pa=3, pb=1)
    carry = suf_scr[...]
    cn = jnp.broadcast_to(cn_ref[...], (n_rows, page))
    parts = []
    for i in range(pg):
        suf = within[i * n_heads:(i + 1) * n_heads, :] + carry
        carry = carry + jnp.sum(lfs[i], axis=-1, keepdims=True)
        bias = jnp.broadcast_to(suf[:, None, :], (n_heads, t_new, page)).reshape(n_rows, page)
        parts.append(jnp.dot(wq, kbuf[slot, i].astype(BF16), preferred_element_type=F32) + bias + cn)
    suf_scr[...] = carry
    update(jnp.concatenate(parts, axis=1), [vbuf[slot, i] for i in range(pg)])

    @pl.when(step == n_s - 1)
    def _():
        s_new = jnp.dot(wq, kn_ref[...].astype(BF16), preferred_element_type=F32)
        update(s_new + bn_ref[...], [vn_ref[...]])
        o = acc_scr[...] / _lanes(l_scr[...], acc_scr.shape[1])
        head = _div_pow2(_iota((t_new, o.shape[1]), 1), HEAD_DIM)
        out = jnp.zeros((t_new, o.shape[1]), F32)
        for h in range(n_heads):
            out = out + jnp.where(head == h, o[h * t_new:(h + 1) * t_new, :], 0.0)
        o_ref[...] = out


def _dattn(page_table, q, cn_col, cache_kt, cache_vt, cache_lf_t, kt_new, vt_new, bias_new, *, n_heads, t_new, pg):
    bsz, n_pages = page_table.shape
    _, d_b, page = cache_kt.shape
    n_rows = n_heads * t_new
    assert n_pages % pg == 0
    per_b = lambda *blk: pl.BlockSpec((None,) + blk, lambda b, s, pt: (b,) + (0,) * len(blk))
    hbm = pl.BlockSpec(memory_space=pl.ANY)
    grid_spec = pltpu.PrefetchScalarGridSpec(
        num_scalar_prefetch=1,
        grid=(bsz, n_pages // pg),
        in_specs=[per_b(t_new, d_b), per_b(n_rows, 1), per_b(d_b, page), per_b(d_b, page), per_b(n_rows, page),
                  hbm, hbm, hbm],
        out_specs=per_b(t_new, d_b),
        scratch_shapes=[pltpu.VMEM((2, pg, d_b, page), F32), pltpu.VMEM((2, pg, d_b, page), F32),
                        pltpu.VMEM((2, pg, n_heads, page), F32), pltpu.SemaphoreType.DMA((3, 2)),
                        pltpu.VMEM((n_rows, d_b), BF16), pltpu.VMEM((n_rows, LANES), F32),
                        pltpu.VMEM((n_rows, LANES), F32), pltpu.VMEM((n_rows, d_b), F32),
                        pltpu.VMEM((n_heads, 1), F32)],
    )
    kern = functools.partial(_dattn_kernel, n_heads=n_heads, t_new=t_new, page=page, pg=pg, n_pages=n_pages)
    return pl.pallas_call(
        kern,
        grid_spec=grid_spec,
        out_shape=jax.ShapeDtypeStruct((bsz, t_new, d_b), F32),
        compiler_params=_params(("arbitrary", "arbitrary")),
        name="dattn",
    )(page_table, q, cn_col, kt_new, vt_new, bias_new, cache_kt, cache_vt, cache_lf_t)


def _merge_kernel(x_ref, y_ref, r_ref, k_ref, v_ref, g_ref, yb_ref, ga_ref, gb_ref,
                  lnw_ref, lnb_ref, rk_ref, bd_ref, woa_ref, wob_ref, wout_ref, o_ref):
    inv_hd = 1.0 / HEAD_DIM
    y = y_ref[...]
    mean = _head_sum(y, bd_ref, 2) * inv_hd
    yc = y - mean
    var = _head_sum(yc * yc, bd_ref, 2) * inv_hd
    yn = yc * lax.rsqrt(var + LNX_EPS) * lnw_ref[...] + lnb_ref[...]
    v = v_ref[...]
    bonus = _head_sum(r_ref[...] * k_ref[...] * rk_ref[...], bd_ref, 2) * v
    ya = ((yn + bonus) * g_ref[...]).astype(BF16)
    pa = jnp.dot(ya, woa_ref[...], preferred_element_type=F32)
    pb = jnp.dot(yb_ref[...].astype(BF16), wob_ref[...], preferred_element_type=F32)
    merged = _sigmoid(ga_ref[...]) * pa + _sigmoid(gb_ref[...]) * pb
    o_ref[...] = x_ref[...] + jnp.dot(merged.astype(BF16), wout_ref[...], preferred_element_type=F32)


def _merge(x, y, r, k, v, g, yb, ga, gb, lnw, lnb, rk, bd, woa, wob, wout):
    n, d = x.shape
    d_a = y.shape[1]
    tm = _row_tile(n)
    row = lambda w: pl.BlockSpec((tm, w), lambda i: (i, 0))
    return pl.pallas_call(
        _merge_kernel,
        grid=(n // tm,),
        in_specs=[row(d)] + [row(d_a)] * 6 + [row(d), row(d)] + [_resident((1, d_a))] * 3
                 + [_resident(bd.shape), _resident(woa.shape), _resident(wob.shape), _resident(wout.shape)],
        out_specs=row(d),
        out_shape=jax.ShapeDtypeStruct((n, d), F32),
        compiler_params=_params(("arbitrary",)),
        name="merge",
    )(x, y, r, k, v, g, yb, ga, gb, lnw, lnb, rk, bd, woa, wob, wout)


def _prep_weights(lw, n_heads_a, n_heads_b):
    (ffn1_norm, ffn1_wg, ffn1_wu, ffn1_wd, mix_norm, w_in, shift_mu, w0, w2, a0, a2, g2, k_k, k_a, r_k,
     lnx_w, lnx_b, b_f, q_norm, k_norm, w_o_a, w_o_b, w_out, ffn2_norm, ffn2_wg, ffn2_wu, ffn2_wd) = lw
    d = w_in.shape[0]
    d_a = n_heads_a * HEAD_DIM
    d_b = n_heads_b * HEAD_DIM
    n_shift = shift_mu.shape[0]
    c4 = n_shift + 3 * d_b
    row = lambda t: t.reshape(1, -1)
    w_main = jnp.concatenate([w_in[:, :c4], w_in[:, c4 + n_heads_b:]], axis=1).astype(BF16)
    w_f = jnp.pad(w_in[:, c4:c4 + n_heads_b], ((0, 0), (0, LANES - n_heads_b))).astype(BF16)
    dl, al = w2.shape[0], a2.shape[0]
    assert dl == HEAD_DIM and al == HEAD_DIM and g2.shape[0] == LANES
    lora = jnp.zeros((LANES, 2 * d_a), F32).at[:dl, :d_a].set(w2).at[dl:, d_a:].set(a2)
    bd = jnp.kron(jnp.eye(max(d_a, d_b) // HEAD_DIM, dtype=F32), jnp.ones((HEAD_DIM, HEAD_DIM), F32)).astype(BF16)
    return dict(
        d=d, d_a=d_a, d_b=d_b, n_shift=n_shift, n_heads_a=n_heads_a, n_heads_b=n_heads_b,
        ffn1=(row(ffn1_norm), ffn1_wg.astype(BF16), ffn1_wu.astype(BF16), ffn1_wd.astype(BF16)),
        ffn2=(row(ffn2_norm), ffn2_wg.astype(BF16), ffn2_wu.astype(BF16), ffn2_wd.astype(BF16)),
        mix_norm=row(mix_norm), w_main=w_main, w_f=w_f,
        b_f=jnp.pad(b_f, (0, LANES - n_heads_b)).reshape(1, LANES),
        q_norm=row(jnp.tile(q_norm, n_heads_b)), k_norm=row(jnp.tile(k_norm, n_heads_b)), bd=bd,
        mu=row(shift_mu), w0=row(w0), a0=row(a0), lora=lora, g2=g2, k_k=row(k_k), k_a=row(k_a),
        r_k=row(r_k), lnx_w=row(lnx_w), lnx_b=row(lnx_b),
        w_o_a=w_o_a.astype(BF16), w_o_b=w_o_b.astype(BF16), w_out=w_out.astype(BF16),
    )


def _layer(x3, prev_shift, s0, past, w):
    bsz, t, d = x3.shape
    d_a, d_b, ha, hb = w["d_a"], w["d_b"], w["n_heads_a"], w["n_heads_b"]
    n = bsz * t
    x = x3.reshape(n, d)
    x1 = _ffn(x, *w["ffn1"])
    init_rows = prev_shift if t >= min(_row_tile(n), 256) else jnp.repeat(prev_shift, t, axis=0)
    q, k, v, g_a, g_b, logf, cum, nck, r, km, va, kap, b, lw, g, new_shift = _mixproj(
        x1, w["mix_norm"], w["w_main"], w["w_f"], w["b_f"], w["q_norm"], w["k_norm"], w["bd"],
        init_rows, w["mu"], w["w0"], w["a0"], w["lora"], w["g2"], w["k_k"], w["k_a"],
        n_shift=w["n_shift"], d_a=d_a, d_b=d_b, n_heads=hb, seq_len=t)

    t_pad = -(-t // CHUNK) * CHUNK
    seq = lambda a: jnp.pad(a.reshape(bsz, t, d_a), ((0, 0), (0, t_pad - t), (0, 0)))
    y_raw, new_s = _rscan(seq(r), seq(km), seq(va), seq(kap), seq(b), seq(lw), s0)
    y_raw = y_raw[:, :t].reshape(n, d_a)

    npair = d_b // LANES
    if past is None:
        tq = min(512, t)
        y_b = _attn(q.reshape(bsz, t, d_b), k.reshape(bsz, t, d_b), v.reshape(bsz, t, d_b),
                    nck.reshape(bsz, t, LANES), tq=tq)
    else:
        cache_k, cache_v, cache_lf, page_table = past
        n_pool, page = cache_k.shape[:2]
        cn = cum.reshape(bsz, t, hb)
        cn_col = cn.transpose(0, 2, 1).reshape(bsz, hb * t, 1)
        tpos = jnp.arange(t)
        ok = tpos[None, :] <= tpos[:, None]
        bn = cn.transpose(0, 2, 1)[:, :, :, None] - cn.transpose(0, 2, 1)[:, :, None, :]
        bn = jnp.where(ok[None, None], bn, NEG_BIG).reshape(bsz, hb * t, t)
        bn = jnp.pad(bn, ((0, 0), (0, 0), (0, page - t)), constant_values=NEG_BIG)
        pages_t = lambda c: c.transpose(0, 2, 3, 1).reshape(n_pool, d_b, page)
        new_t = lambda a: jnp.pad(a.reshape(bsz, t, d_b).transpose(0, 2, 1), ((0, 0), (0, 0), (0, page - t)))
        pg = next(g for g in (32, 16, 8, 4, 2, 1) if page_table.shape[1] % g == 0)
        y_b = _dattn(page_table, q.reshape(bsz, t, d_b), cn_col, pages_t(cache_k), pages_t(cache_v),
                     cache_lf.transpose(0, 2, 1), new_t(k), new_t(v), bn, n_heads=hb, t_new=t, pg=pg)
    y_b = y_b.reshape(n, d_b)

    x2 = _merge(x1, y_raw, r, km, va, g, y_b, g_a, g_b, w["lnx_w"], w["lnx_b"], w["r_k"], w["bd"],
                w["w_o_a"], w["w_o_b"], w["w_out"])
    x3o = _ffn(x2, *w["ffn2"]).reshape(bsz, t, d)
    return x3o, (k.reshape(bsz, t, hb, HEAD_DIM), v.reshape(bsz, t, hb, HEAD_DIM),
                 logf.reshape(bsz, t, hb), new_s, new_shift)


def kernel(x_prompt, x_sample, cache_k, cache_v, cache_logf, state_wkv, state_shift, page_table,
           ffn1_norm, ffn1_wg, ffn1_wu, ffn1_wd, mix_norm, w_in, shift_mu,
           rwkv_w0, rwkv_w2, rwkv_a0, rwkv_a2, rwkv_g2, rwkv_k_k, rwkv_k_a, rwkv_r_k, rwkv_lnx_w, rwkv_lnx_b,
           fox_b_f, fox_q_norm, fox_k_norm, w_o_a, w_o_b, w_out,
           ffn2_norm, ffn2_wg, ffn2_wu, ffn2_wd):
    depth = w_in.shape[0]
    n_heads_a = state_wkv.shape[2]
    n_heads_b = cache_k.shape[3]
    bp = x_prompt.shape[0]
    xp, xs = x_prompt, x_sample
    outs_p, outs_s = [], []
    for l in range(depth):
        lw = (ffn1_norm[l], ffn1_wg[l], ffn1_wu[l], ffn1_wd[l], mix_norm[l], w_in[l], shift_mu[l],
              rwkv_w0[l], rwkv_w2[l], rwkv_a0[l], rwkv_a2[l], rwkv_g2[l], rwkv_k_k[l], rwkv_k_a[l],
              rwkv_r_k[l].reshape(-1), rwkv_lnx_w[l], rwkv_lnx_b[l], fox_b_f[l], fox_q_norm[l], fox_k_norm[l],
              w_o_a[l], w_o_b[l], w_out[l], ffn2_norm[l], ffn2_wg[l], ffn2_wu[l], ffn2_wd[l])
        w = _prep_weights(lw, n_heads_a, n_heads_b)
        prev0 = jnp.zeros((bp, w["n_shift"]), xp.dtype)
        s00 = jnp.zeros((bp, n_heads_a, HEAD_DIM, HEAD_DIM), xp.dtype)
        xp, op = _layer(xp, prev0, s00, None, w)
        xs, os_ = _layer(xs, state_shift[l], state_wkv[l], (cache_k[l], cache_v[l], cache_logf[l], page_table), w)
        outs_p.append(op)
        outs_s.append(os_)
    stack = lambda outs, i: jnp.stack([o[i] for o in outs], 0)
    return ((xp, xs) + tuple(stack(outs_p, i) for i in range(5)) + tuple(stack(outs_s, i) for i in range(5)))
```

```python
import functools

import jax
import jax.numpy as jnp
from jax import lax
from jax.experimental import pallas as pl
from jax.experimental.pallas import tpu as pltpu

F32 = jnp.float32
BF16 = jnp.bfloat16

HEAD_DIM = 64
LANES = 128
PAIR = LANES // HEAD_DIM
NORM_EPS = 1e-6
LNX_EPS = 64e-5
DECAY_SCALE = 0.6065306597126334
NEG_BIG = -1e30
BIAS_GROUP = 16
CHUNK = 64
VMEM_LIMIT = 56 * 1024 * 1024

NN = (((1,), (0,)), ((), ()))
NT = (((1,), (1,)), ((), ()))
TN = (((0,), (0,)), ((), ()))


def _pieces(x, n):
    if x.dtype == BF16:
        return [x]
    out = []
    rem = x
    for i in range(n):
        p = rem.astype(BF16)
        out.append(p)
        if i + 1 < n:
            rem = rem - p.astype(F32)
    return out


def _mm(a, b, dims=NN, pa=1, pb=1):
    pa_l = _pieces(a, pa)
    pb_l = _pieces(b, pb)
    if len(pa_l) == 2 and len(pb_l) == 2:
        m_axis = 1 - dims[0][0][0]
        m = a.shape[m_axis]
        both = lax.dot_general(jnp.concatenate(pa_l, axis=m_axis), pb_l[0], dims, preferred_element_type=F32)
        return both[0:m] + both[m:] + lax.dot_general(pa_l[0], pb_l[1], dims, preferred_element_type=F32)
    order = max(len(pa_l), len(pb_l))
    acc = None
    for i in reversed(range(len(pa_l))):
        for j in reversed(range(len(pb_l))):
            if i + j >= order:
                continue
            t = lax.dot_general(pa_l[i], pb_l[j], dims, preferred_element_type=F32)
            acc = t if acc is None else acc + t
    return acc


def _sigmoid(x):
    return 1.0 / (1.0 + jnp.exp(-x))


def _rms(x, g):
    return x * lax.rsqrt(jnp.mean(x * x, axis=-1, keepdims=True) + NORM_EPS) * g


def _iota(shape, dim):
    return lax.broadcasted_iota(jnp.int32, shape, dim)


def _div_pow2(x, n):
    assert n & (n - 1) == 0
    return lax.shift_right_logical(x, n.bit_length() - 1)


def _mod_pow2(x, n):
    assert n & (n - 1) == 0
    return x & (n - 1)


def _resident(shape):
    nd = len(shape)
    return pl.BlockSpec(shape, lambda *_: (0,) * nd, pipeline_mode=pl.Buffered(1))


def _params(sem):
    return pltpu.CompilerParams(dimension_semantics=sem, vmem_limit_bytes=VMEM_LIMIT)


def _row_tile(n):
    for t in (512, 256, 128, 64, 32, 16, 8):
        if n % t == 0:
            return t
    raise ValueError(f"row count {n} is not a multiple of 8")


def _ffn_kernel(x_ref, g_ref, wg_ref, wu_ref, wd_ref, o_ref):
    x = x_ref[...]
    h = _rms(x, g_ref[...]).astype(BF16)
    gate = jnp.dot(h, wg_ref[...], preferred_element_type=F32)
    up = jnp.dot(h, wu_ref[...], preferred_element_type=F32)
    act = (gate * _sigmoid(gate) * up).astype(BF16)
    o_ref[...] = x + 0.5 * jnp.dot(act, wd_ref[...], preferred_element_type=F32)


def _ffn(x, g, wg, wu, wd):
    n, d = x.shape
    dff = wg.shape[1]
    tm = _row_tile(n)
    return pl.pallas_call(
        _ffn_kernel,
        grid=(n // tm,),
        in_specs=[pl.BlockSpec((tm, d), lambda i: (i, 0)), _resident((1, d)),
                  _resident((d, dff)), _resident((d, dff)), _resident((dff, d))],
        out_specs=pl.BlockSpec((tm, d), lambda i: (i, 0)),
        out_shape=jax.ShapeDtypeStruct((n, d), F32),
        compiler_params=_params(("arbitrary",)),
        name="ffn",
    )(x, g, wg, wu, wd)


def _head_sum(x, bd_ref, pieces):
    return _mm(x, bd_ref[...], NN, pa=pieces, pb=1)


def _rwkv_prep(p, i, init_ref, mu_ref, w0_ref, a0_ref, lora_ref, g2_ref, kk_ref, ka_ref, bd_ref,
               r_ref, k_ref, v_ref, kap_ref, b_ref, lw_ref, g_ref, last_ref, *, d_a, seq_len, tm):
    prev = pltpu.roll(p, 1, 0)
    rowi = _iota(p.shape, 0)
    if seq_len >= tm:
        tiles_per_seq = seq_len // tm

        @pl.when(i % tiles_per_seq == 0)
        def _():
            last_ref[...] = init_ref[...]

        prev = jnp.where(rowi == 0, last_ref[...], prev)
        last_ref[...] = p[tm - 1:tm, :]
    else:
        prev = jnp.where(_mod_pow2(rowi, seq_len) == 0, init_ref[...], prev)
    xm = p + (prev - p) * mu_ref[...]
    r = xm[:, 0:d_a]
    k = xm[:, d_a:2 * d_a]
    v = xm[:, 2 * d_a:3 * d_a]
    xwa = xm[:, 3 * d_a:3 * d_a + LANES]
    xg = xm[:, 3 * d_a + LANES:3 * d_a + 2 * LANES]
    xwa = jnp.where(_iota(xwa.shape, 1) < HEAD_DIM, jnp.tanh(xwa), xwa)
    lo = _mm(xwa, lora_ref[...], NN, pa=2, pb=2)
    lw_ref[...] = -DECAY_SCALE * _sigmoid(w0_ref[...] + lo[:, 0:d_a])
    a = _sigmoid(a0_ref[...] + lo[:, d_a:2 * d_a])
    g_ref[...] = _mm(_sigmoid(xg), g2_ref[...], NN)
    kk = k * kk_ref[...]
    ss = _head_sum(kk * kk, bd_ref, 1)
    kk = kk * lax.rsqrt(jnp.maximum(ss, 1e-24))
    r_ref[...] = r
    k_ref[...] = k * (1.0 + (a - 1.0) * ka_ref[...])
    v_ref[...] = v
    kap_ref[...] = kk
    b_ref[...] = kk * a


def _mixproj_kernel(x_ref, g_ref, w_ref, wf_ref, bf_ref, qn_ref, kn_ref, bd_ref, *rest,
                    n_shift, d_a, d_b, d_model, n_heads, seq_len, tm):
    prep_in = rest[0:8]
    q_ref, k_ref, v_ref, ga_ref, gb_ref, lf_ref, cum_ref, nck_ref = rest[8:16]
    prep_out = rest[16:23]
    sh_ref, carry_ref, last_ref = rest[23:]
    i = pl.program_id(0)
    h = _rms(x_ref[...], g_ref[...]).astype(BF16)
    c1 = n_shift
    c2 = c1 + d_b
    c3 = c2 + d_b
    c4 = c3 + d_b
    c5 = c4 + d_model
    c6 = c5 + d_model
    p_a = jnp.dot(h, w_ref[:, 0:c1], preferred_element_type=F32)
    q = jnp.dot(h, w_ref[:, c1:c2], preferred_element_type=F32)
    k = jnp.dot(h, w_ref[:, c2:c3], preferred_element_type=F32)
    v_ref[...] = jnp.dot(h, w_ref[:, c3:c4], preferred_element_type=F32)
    ga_ref[...] = jnp.dot(h, w_ref[:, c4:c5], preferred_element_type=F32)
    gb_ref[...] = jnp.dot(h, w_ref[:, c5:c6], preferred_element_type=F32)
    inv_hd = 1.0 / HEAD_DIM
    q = q * lax.rsqrt(_head_sum(q * q, bd_ref, 1) * inv_hd + NORM_EPS) * qn_ref[...]
    k = k * lax.rsqrt(_head_sum(k * k, bd_ref, 1) * inv_hd + NORM_EPS) * kn_ref[...]
    q_ref[...] = q * (1.0 / float(HEAD_DIM) ** 0.5)
    k_ref[...] = k
    z = jnp.dot(h, wf_ref[...], preferred_element_type=F32) + bf_ref[...]
    logf = jnp.minimum(z, 0.0) - jnp.log1p(jnp.exp(-jnp.abs(z)))
    logf = jnp.where(_iota(logf.shape, 1) < n_heads, logf, 0.0)
    row = _iota((tm, tm), 0)
    col = _iota((tm, tm), 1)
    keep = col <= row
    if seq_len < tm:
        keep = keep & (_div_pow2(row, seq_len) == _div_pow2(col, seq_len))
    tri = jnp.where(keep, 1.0, 0.0).astype(BF16)
    cum = _mm(tri, logf, NN, pa=1, pb=3)
    if seq_len > tm:
        tiles_per_seq = seq_len // tm

        @pl.when(i % tiles_per_seq == 0)
        def _():
            carry_ref[...] = jnp.zeros_like(carry_ref)

        cum = cum + carry_ref[...]
        carry_ref[...] = cum[tm - 1:tm, :]
    lf_ref[...] = logf[:, :n_heads]
    cum_ref[...] = cum[:, :n_heads]
    src = _iota((LANES, LANES), 0)
    dst = _iota((LANES, LANES), 1)
    sub = _mod_pow2(dst, BIAS_GROUP)
    hit = (_div_pow2(dst, BIAS_GROUP) == _div_pow2(src, PAIR)) & (sub < 3 * PAIR) & (src < n_heads) \
        & ((sub >= 3) == (_mod_pow2(src, PAIR) == 1))
    nck_ref[...] = _mm(cum, jnp.where(hit, -1.0, 0.0).astype(BF16), NN, pa=3, pb=1)
    _rwkv_prep(p_a, i, *prep_in, bd_ref, *prep_out, last_ref, d_a=d_a, seq_len=seq_len, tm=tm)
    if seq_len >= tm:
        sh_ref[...] = p_a[tm - 1:tm, :]
    else:
        n_seq = tm // seq_len
        last = _iota((n_seq, tm), 1) == _iota((n_seq, tm), 0) * seq_len + (seq_len - 1)
        sh_ref[...] = _mm(jnp.where(last, 1.0, 0.0).astype(BF16), p_a, NN, pa=1, pb=3)


def _mixproj(x, g, w_main, w_f, b_f, qn, kn, bd, init_rows, mu, w0, a0, lora, g2, k_k, k_a,
             *, n_shift, d_a, d_b, n_heads, seq_len):
    n, d = x.shape
    tm = min(_row_tile(n), 256)
    assert seq_len % tm == 0 or tm % seq_len == 0
    row = lambda w: pl.BlockSpec((tm, w), lambda i: (i, 0))
    shp = lambda w: jax.ShapeDtypeStruct((n, w), F32)
    n_seq = n // seq_len
    if seq_len >= tm:
        tps = seq_len // tm
        init_spec = pl.BlockSpec((None, 1, n_shift), lambda i: (i // tps, 0, 0))
        init_rows = init_rows.reshape(n_seq, 1, n_shift)
        sh_spec = pl.BlockSpec((None, 1, n_shift), lambda i: (i // tps, 0, 0))
        sh_shape = jax.ShapeDtypeStruct((n_seq, 1, n_shift), F32)
    else:
        init_spec = row(n_shift)
        sh_spec = pl.BlockSpec((tm // seq_len, n_shift), lambda i: (i, 0))
        sh_shape = jax.ShapeDtypeStruct((n_seq, n_shift), F32)
    kern = functools.partial(_mixproj_kernel, n_shift=n_shift, d_a=d_a, d_b=d_b, d_model=d, n_heads=n_heads,
                             seq_len=seq_len, tm=tm)
    outs = pl.pallas_call(
        kern,
        grid=(n // tm,),
        in_specs=[row(d), _resident((1, d)), _resident(w_main.shape), _resident(w_f.shape),
                  _resident((1, LANES)), _resident((1, d_b)), _resident((1, d_b)), _resident(bd.shape),
                  init_spec, _resident((1, n_shift)), _resident((1, d_a)), _resident((1, d_a)),
                  _resident(lora.shape), _resident(g2.shape), _resident((1, d_a)), _resident((1, d_a))],
        out_specs=[row(d_b), row(d_b), row(d_b), row(d), row(d), row(n_heads), row(n_heads), row(LANES)]
                  + [row(d_a)] * 7 + [sh_spec],
        out_shape=[shp(d_b), shp(d_b), shp(d_b), shp(d), shp(d), shp(n_heads), shp(n_heads), shp(LANES)]
                  + [shp(d_a)] * 7 + [sh_shape],
        scratch_shapes=[pltpu.VMEM((1, LANES), F32), pltpu.VMEM((1, n_shift), F32)],
        compiler_params=_params(("arbitrary",)),
        name="mixproj",
    )(x, g, w_main, w_f, b_f, qn, kn, bd, init_rows, mu, w0, a0, lora, g2, k_k, k_a)
    return tuple(outs[:-1]) + (outs[-1].reshape(n_seq, n_shift),)


def _stack(x, m0):
    return jnp.concatenate([jnp.where(m0, x, 0.0), jnp.where(m0, 0.0, x)], axis=0)


def _rscan_kernel(r_ref, k_ref, v_ref, kap_ref, b_ref, lw_ref, s0_ref, y_ref, st_ref, s_scr, *, bb, npair, c, hs, hc, hy):
    c2 = PAIR * c
    ci_ = pl.program_id(1)
    probs = [(i, j) for i in range(bb) for j in range(npair)]
    each = lambda f, *ls: [f(*xs) for xs in zip(*ls)]

    @pl.when(ci_ == 0)
    def _():
        zero = jnp.zeros((HEAD_DIM, HEAD_DIM), F32)
        for p, (i, j) in enumerate(probs):
            top = jnp.concatenate([s0_ref[i, PAIR * j], zero], axis=1)
            bot = jnp.concatenate([zero, s0_ref[i, PAIR * j + 1]], axis=1)
            s_scr[p] = jnp.concatenate([top, bot], axis=0)

    m0 = _iota((c, LANES), 1) < HEAD_DIM
    ri = _iota((c2, c2), 0)
    ci = _iota((c2, c2), 1)
    same = _div_pow2(ri, c) == _div_pow2(ci, c)
    strict = same & (ci < ri)
    incl = same & (ci <= ri)
    eye = ri == ci
    tri_c = jnp.where(_iota((c, c), 1) <= _iota((c, c), 0), 1.0, 0.0).astype(BF16)
    ld = lambda ref: [ref[i, :, j * LANES:(j + 1) * LANES] for (i, j) in probs]

    lw = ld(lw_ref)
    cum = each(lambda x: _mm(tri_c, x, NN, pa=1, pb=3), lw)
    cum_end = each(lambda x: x[c - 1:c, :], cum)
    e_neg = each(lambda x: jnp.exp(-x), cum)
    e_rem = each(lambda x, e: jnp.exp(e - x), cum, cum_end)
    kap, b, k = ld(kap_ref), ld(b_ref), ld(k_ref)
    a_t = each(lambda kp, x, w: _stack(-kp * jnp.exp(x - w), m0), kap, cum, lw)
    r_t = each(lambda r, x: _stack(r * jnp.exp(x), m0), ld(r_ref), cum)
    b_t = each(lambda x, e: _stack(x * e, m0), b, e_neg)
    k_t = each(lambda x, e: _stack(x * e, m0), k, e_neg)
    b_p = each(lambda x, e: _stack(x * e, m0), b, e_rem)
    k_p = each(lambda x, e: _stack(x * e, m0), k, e_rem)
    v_s = each(lambda x: _stack(x, m0), ld(v_ref))

    side = lambda x, y: jnp.concatenate([x, y], axis=1)
    bk_t = each(lambda x, y: jnp.concatenate([x, y], axis=0), b_t, k_t)
    sc_a = each(lambda x, y: _mm(x, y, NT, hc, hc), a_t, bk_t)
    sc_r = each(lambda x, y: _mm(x, y, NT, hy, hy), r_t, bk_t)
    l_ab = each(lambda x: jnp.where(strict, x[:, 0:c2], 0.0), sc_a)
    a_ak = each(lambda x: jnp.where(strict, x[:, c2:], 0.0), sc_a)
    a_rb = each(lambda x: jnp.where(incl, x[:, 0:c2], 0.0), sc_r)
    a_rk = each(lambda x: jnp.where(incl, x[:, c2:], 0.0), sc_r)

    t_inv = each(lambda x: jnp.where(eye, 1.0, 0.0) + x, l_ab)
    l_pow = each(lambda x: _mm(x, x, NN, hc, hc), l_ab)
    n_sq = c.bit_length() - 2
    for i in range(1, n_sq + 1):
        if i < n_sq:
            both = each(lambda x, t: _mm(jnp.concatenate([x, t], axis=0), x, NN, hc, hc), l_pow, t_inv)
            l_pow = each(lambda x: x[0:c2], both)
            t_inv = each(lambda t, x: t + x[c2:], t_inv, both)
        else:
            t_inv = each(lambda t, x: t + _mm(t, x, NN, hc, hc), t_inv, l_pow)

    akv = each(lambda x, y: _mm(x, y, NN, hc, hc), a_ak, v_s)
    au = each(lambda t, x, y: _mm(t, side(x, y), NN, hc, hc), t_inv, a_t, akv)
    a_p = each(lambda x: x[:, 0:LANES], au)
    u0 = each(lambda x: x[:, LANES:], au)
    ry = each(lambda x, y: _mm(x, y, NN, hy, hy), a_rb, au)
    r_p = each(lambda r, x: r + x[:, 0:LANES], r_t, ry)
    y0 = each(lambda x, y, z: _mm(x, y, NN, hy, hy) + z[:, LANES:], a_rk, v_s, ry)
    eye_k = _iota((LANES, LANES), 0) == _iota((LANES, LANES), 1)
    m_c = each(lambda e, x, y: jnp.where(eye_k, jnp.exp(e), 0.0) + _mm(x, y, TN, hc, hc), cum_end, a_p, b_p)
    n_c = each(lambda x, y, z, w: _mm(x, y, TN, hs, hs) + _mm(z, w, TN, hs, hs), u0, b_p, v_s, k_p)

    s_in = [s_scr[p] for p in range(len(probs))]
    y_st = each(lambda r, s, y: _mm(r, s, NT, hy, hy) + y, r_p, s_in, y0)
    s_out = each(lambda s, m, n: _mm(s, m, NN, hs, hs) + n, s_in, m_c, n_c)
    for p, (i, j) in enumerate(probs):
        y_ref[i, :, j * LANES:(j + 1) * LANES] = y_st[p][0:c, :] + y_st[p][c:c2, :]
        s_scr[p] = s_out[p]

    @pl.when(ci_ == pl.num_programs(1) - 1)
    def _():
        for p, (i, j) in enumerate(probs):
            st_ref[i, PAIR * j] = s_out[p][0:HEAD_DIM, 0:HEAD_DIM]
            st_ref[i, PAIR * j + 1] = s_out[p][HEAD_DIM:, HEAD_DIM:]


def _rscan(r, k, v, kap, b, lw, s0, *, hs=2, hc=1, hy=1):
    bsz, t, d_a = r.shape
    c = min(CHUNK, t)
    assert t % c == 0 and c % 8 == 0 and c & (c - 1) == 0
    npair = d_a // LANES
    bb = 2 if bsz % 2 == 0 else 1
    seq = pl.BlockSpec((bb, c, d_a), lambda g, ci: (g, ci, 0))
    st = pl.BlockSpec((bb, PAIR * npair, HEAD_DIM, HEAD_DIM), lambda g, ci: (g, 0, 0, 0))
    kern = functools.partial(_rscan_kernel, bb=bb, npair=npair, c=c, hs=hs, hc=hc, hy=hy)
    return pl.pallas_call(
        kern,
        grid=(bsz // bb, t // c),
        in_specs=[seq] * 6 + [st],
        out_specs=[seq, st],
        out_shape=[jax.ShapeDtypeStruct((bsz, t, d_a), F32),
                   jax.ShapeDtypeStruct(s0.shape, F32)],
        scratch_shapes=[pltpu.VMEM((bb * npair, LANES, LANES), F32)],
        compiler_params=_params(("arbitrary", "arbitrary")),
        name="rscan",
    )(r, k, v, kap, b, lw, s0)


def _lanes(x, n):
    return x if n == LANES else jnp.concatenate([x] * (n // LANES), axis=1)


def _attn_kernel(q_ref, k_ref, v_ref, kb_ref, o_ref, kaug_scr, vb_scr, s_scr, p_scr, m_scr, acc_scr,
                 *, tq, strip, ns):
    qi = pl.program_id(2)
    tk = tq
    rows2 = 2 * tq
    streams = range(ns)
    lanes_of = lambda st: slice(st * LANES, (st + 1) * LANES)

    @pl.when(qi == 0)
    def _():
        p0, p1, p2 = _pieces(kb_ref[...], 3)
        which = _mod_pow2(_iota(p0.shape, 1), BIAS_GROUP)
        bias = jnp.where((which == 0) | (which == 3), p0, jnp.where((which == 1) | (which == 4), p1, p2))
        for st in streams:
            kaug_scr[st, :, 0:LANES] = k_ref[:, lanes_of(st)].astype(BF16)
            kaug_scr[st, :, LANES:2 * LANES] = bias
            vb_scr[st, :, 0:LANES] = v_ref[:, lanes_of(st)].astype(BF16)
            vb_scr[st, :, LANES:2 * LANES] = jnp.ones((vb_scr.shape[1], LANES), BF16)

    m0 = _iota((tq, LANES), 1) < HEAD_DIM
    head1 = _iota((rows2, LANES), 0) >= tq
    q_aug = []
    for st in streams:
        lane = _iota((rows2, LANES), 1) - BIAS_GROUP * (ns * pl.program_id(1) + st)
        pick = (lane >= jnp.where(head1, 3, 0)) & (lane < jnp.where(head1, 6, 3))
        q_aug.append(jnp.concatenate([_stack(q_ref[:, lanes_of(st)], m0), jnp.where(pick, 1.0, 0.0)],
                                     axis=1).astype(BF16))
    m_scr[...] = jnp.full_like(m_scr, NEG_BIG)
    acc_scr[...] = jnp.zeros_like(acc_scr)

    def keys(ki):
        return pl.ds(pl.multiple_of(ki * tk, tk), tk)

    def tile(ki, diagonal):
        for st in streams:
            s_scr[st] = lax.dot_general(q_aug[st], kaug_scr[st, keys(ki), :], NT, preferred_element_type=F32)
        for st in streams:
            for r0 in range(0, rows2, strip):
                rs = slice(r0, r0 + strip)
                s = s_scr[st, rs, :]
                if diagonal:
                    s = jnp.where(_iota((strip, tk), 1) <= _iota((strip, tk), 0) + r0 % tq, s, NEG_BIG)
                m_old = m_scr[st, rs, :]
                m_new = jnp.maximum(m_old, jnp.max(s, axis=-1, keepdims=True))
                alpha = jnp.exp(m_old - m_new)
                acc_scr[st, rs, :] = _lanes(alpha, 2 * LANES) * acc_scr[st, rs, :]
                p_scr[st, rs, :] = jnp.exp((s - _lanes(m_new, tk)).astype(BF16))
                m_scr[st, rs, :] = m_new
            acc_scr[st] += jnp.dot(p_scr[st], vb_scr[st, keys(ki), :], preferred_element_type=F32)

    def off_diagonal(ki, carry):
        tile(ki, False)
        return carry

    lax.fori_loop(0, qi, off_diagonal, 0)
    tile(qi, True)
    for st in streams:
        o = acc_scr[st, :, 0:LANES] / acc_scr[st, :, LANES:2 * LANES]
        o_ref[:, lanes_of(st)] = jnp.where(m0, o[0:tq, :], o[tq:rows2, :])


def _attn(q, k, v, kb, *, tq):
    bsz, t, d_b = q.shape
    npair = d_b // LANES
    ns = 2 if npair % 2 == 0 else 1
    nq = t // tq
    qspec = pl.BlockSpec((None, tq, ns * LANES), lambda b, j, qi: (b, qi, j))
    kspec = pl.BlockSpec((None, t, ns * LANES), lambda b, j, qi: (b, 0, j))
    kbspec = pl.BlockSpec((None, t, LANES), lambda b, j, qi: (b, 0, 0))
    return pl.pallas_call(
        functools.partial(_attn_kernel, tq=tq, strip=min(64, tq), ns=ns),
        grid=(bsz, npair // ns, nq),
        in_specs=[qspec, kspec, kspec, kbspec],
        out_specs=qspec,
        out_shape=jax.ShapeDtypeStruct((bsz, t, d_b), F32),
        scratch_shapes=[pltpu.VMEM((ns, t, 2 * LANES), BF16), pltpu.VMEM((ns, t, 2 * LANES), BF16),
                        pltpu.VMEM((ns, 2 * tq, tq), F32), pltpu.VMEM((ns, 2 * tq, tq), BF16),
                        pltpu.VMEM((ns, 2 * tq, LANES), F32), pltpu.VMEM((ns, 2 * tq, 2 * LANES), F32)],
        compiler_params=_params(("arbitrary",) * 3),
        name="attn",
    )(q, k, v, kb)


def _dattn_kernel(pt_ref, q_ref, cn_ref, kn_ref, vn_ref, bn_ref, kc_hbm, vc_hbm, lf_hbm, o_ref,
                  kbuf, vbuf, lbuf, sems, wq_scr, m_scr, l_scr, acc_scr, suf_scr,
                  *, n_heads, t_new, page, pg, n_pages):
    b = pl.program_id(0)
    step = pl.program_id(1)
    n_b = pl.num_programs(0)
    n_s = pl.num_programs(1)
    n_rows = n_heads * t_new
    slot = (b * n_s + step) & 1

    def page_copies(bi, si, sl):
        out = []
        for i in range(pg):
            src = pt_ref[bi, n_pages - 1 - (si * pg + i)]
            out.append(pltpu.make_async_copy(kc_hbm.at[src], kbuf.at[sl, i], sems.at[0, sl]))
            out.append(pltpu.make_async_copy(vc_hbm.at[src], vbuf.at[sl, i], sems.at[1, sl]))
            out.append(pltpu.make_async_copy(lf_hbm.at[src], lbuf.at[sl, i], sems.at[2, sl]))
        return out

    @pl.when((b == 0) & (step == 0))
    def _():
        for c in page_copies(0, 0, 0):
            c.start()

    wraps = step + 1 == n_s

    @pl.when(jnp.logical_not(wraps & (b + 1 == n_b)))
    def _():
        for c in page_copies(jnp.where(wraps, b + 1, b), jnp.where(wraps, 0, step + 1), 1 - slot):
            c.start()

    @pl.when(step == 0)
    def _():
        q = q_ref[...]
        head = _div_pow2(_iota(q.shape, 1), HEAD_DIM)
        wq_scr[...] = jnp.concatenate([jnp.where(head == h, q, 0.0) for h in range(n_heads)],
                                      axis=0).astype(BF16)
        m_scr[...] = jnp.full_like(m_scr, NEG_BIG)
        l_scr[...] = jnp.zeros_like(l_scr)
        acc_scr[...] = jnp.zeros_like(acc_scr)
        suf_scr[...] = jnp.zeros_like(suf_scr)

    for c in page_copies(b, step, slot):
        c.wait()

    def update(s, vals_t):
        n = len(vals_t)
        m_old = m_scr[...]
        m_new = jnp.maximum(m_old, jnp.max(s, axis=-1, keepdims=True))
        alpha = jnp.exp(m_old - m_new)
        p = jnp.exp(s - _lanes(m_new, n * page))
        l_scr[...] = alpha * l_scr[...] + jnp.sum(p, axis=-1, keepdims=True)
        p = p.astype(BF16)
        pv = None
        for i in range(n):
            t = lax.dot_general(p[:, i * page:(i + 1) * page], vals_t[i].astype(BF16), NT,
                                preferred_element_type=F32)
            pv = t if pv is None else pv + t
        acc_scr[...] = _lanes(alpha, acc_scr.shape[1]) * acc_scr[...] + pv
        m_scr[...] = m_new

    wq = wq_scr[...]
    lfs = [lbuf[slot, i] for i in range(pg)]
    lf_all = jnp.concatenate(lfs, axis=0)
    later = jnp.where(_iota((page, page), 0) > _iota((page, page), 1), 1.0, 0.0).astype(BF16)
    within = _mm(lf_all, later, NN, pa=3, pb=1)
    carry = suf_scr[...]
    cn = jnp.broadcast_to(cn_ref[...], (n_rows, page))
    parts = []
    for i in range(pg):
        suf = within[i * n_heads:(i + 1) * n_heads, :] + carry
        carry = carry + jnp.sum(lfs[i], axis=-1, keepdims=True)
        bias = jnp.broadcast_to(suf[:, None, :], (n_heads, t_new, page)).reshape(n_rows, page)
        parts.append(jnp.dot(wq, kbuf[slot, i].astype(BF16), preferred_element_type=F32) + bias + cn)
    suf_scr[...] = carry
    update(jnp.concatenate(parts, axis=1), [vbuf[slot, i] for i in range(pg)])

    @pl.when(step == n_s - 1)
    def _():
        s_new = jnp.dot(wq, kn_ref[...].astype(BF16), preferred_element_type=F32)
        update(s_new + bn_ref[...], [vn_ref[...]])
        o = acc_scr[...] / _lanes(l_scr[...], acc_scr.shape[1])
        head = _div_pow2(_iota((t_new, o.shape[1]), 1), HEAD_DIM)
        out = jnp.zeros((t_new, o.shape[1]), F32)
        for h in range(n_heads):
            out = out + jnp.where(head == h, o[h * t_new:(h + 1) * t_new, :], 0.0)
        o_ref[...] = out


def _dattn(page_table, q, cn_col, cache_kt, cache_vt, cache_lf_t, kt_new, vt_new, bias_new, *, n_heads, t_new, pg):
    bsz, n_pages = page_table.shape
    _, d_b, page = cache_kt.shape
    n_rows = n_heads * t_new
    assert n_pages % pg == 0
    per_b = lambda *blk: pl.BlockSpec((None,) + blk, lambda b, s, pt: (b,) + (0,) * len(blk))
    hbm = pl.BlockSpec(memory_space=pl.ANY)
    grid_spec = pltpu.PrefetchScalarGridSpec(
        num_scalar_prefetch=1,
        grid=(bsz, n_pages // pg),
        in_specs=[per_b(t_new, d_b), per_b(n_rows, 1), per_b(d_b, page), per_b(d_b, page), per_b(n_rows, page),
                  hbm, hbm, hbm],
        out_specs=per_b(t_new, d_b),
        scratch_shapes=[pltpu.VMEM((2, pg, d_b, page), F32), pltpu.VMEM((2, pg, d_b, page), F32),
                        pltpu.VMEM((2, pg, n_heads, page), F32), pltpu.SemaphoreType.DMA((3, 2)),
                        pltpu.VMEM((n_rows, d_b), BF16), pltpu.VMEM((n_rows, LANES), F32),
                        pltpu.VMEM((n_rows, LANES), F32), pltpu.VMEM((n_rows, d_b), F32),
                        pltpu.VMEM((n_heads, 1), F32)],
    )
    kern = functools.partial(_dattn_kernel, n_heads=n_heads, t_new=t_new, page=page, pg=pg, n_pages=n_pages)
    return pl.pallas_call(
        kern,
        grid_spec=grid_spec,
        out_shape=jax.ShapeDtypeStruct((bsz, t_new, d_b), F32),
        compiler_params=_params(("arbitrary", "arbitrary")),
        name="dattn",
    )(page_table, q, cn_col, kt_new, vt_new, bias_new, cache_kt, cache_vt, cache_lf_t)


def _merge_kernel(x_ref, y_ref, r_ref, k_ref, v_ref, g_ref, yb_ref, ga_ref, gb_ref,
                  lnw_ref, lnb_ref, rk_ref, bd_ref, woa_ref, wob_ref, wout_ref, o_ref):
    inv_hd = 1.0 / HEAD_DIM
    y = y_ref[...]
    mean = _head_sum(y, bd_ref, 1) * inv_hd
    yc = y - mean
    var = _head_sum(yc * yc, bd_ref, 1) * inv_hd
    yn = yc * lax.rsqrt(var + LNX_EPS) * lnw_ref[...] + lnb_ref[...]
    v = v_ref[...]
    bonus = _head_sum(r_ref[...] * k_ref[...] * rk_ref[...], bd_ref, 1) * v
    ya = ((yn + bonus) * g_ref[...]).astype(BF16)
    pa = jnp.dot(ya, woa_ref[...], preferred_element_type=F32)
    pb = jnp.dot(yb_ref[...].astype(BF16), wob_ref[...], preferred_element_type=F32)
    merged = _sigmoid(ga_ref[...]) * pa + _sigmoid(gb_ref[...]) * pb
    o_ref[...] = x_ref[...] + jnp.dot(merged.astype(BF16), wout_ref[...], preferred_element_type=F32)


def _merge(x, y, r, k, v, g, yb, ga, gb, lnw, lnb, rk, bd, woa, wob, wout):
    n, d = x.shape
    d_a = y.shape[1]
    tm = _row_tile(n)
    row = lambda w: pl.BlockSpec((tm, w), lambda i: (i, 0))
    return pl.pallas_call(
        _merge_kernel,
        grid=(n // tm,),
        in_specs=[row(d)] + [row(d_a)] * 6 + [row(d), row(d)] + [_resident((1, d_a))] * 3
                 + [_resident(bd.shape), _resident(woa.shape), _resident(wob.shape), _resident(wout.shape)],
        out_specs=row(d),
        out_shape=jax.ShapeDtypeStruct((n, d), F32),
        compiler_params=_params(("arbitrary",)),
        name="merge",
    )(x, y, r, k, v, g, yb, ga, gb, lnw, lnb, rk, bd, woa, wob, wout)


def _prep_weights(lw, n_heads_a, n_heads_b):
    (ffn1_norm, ffn1_wg, ffn1_wu, ffn1_wd, mix_norm, w_in, shift_mu, w0, w2, a0, a2, g2, k_k, k_a, r_k,
     lnx_w, lnx_b, b_f, q_norm, k_norm, w_o_a, w_o_b, w_out, ffn2_norm, ffn2_wg, ffn2_wu, ffn2_wd) = lw
    d = w_in.shape[0]
    d_a = n_heads_a * HEAD_DIM
    d_b = n_heads_b * HEAD_DIM
    n_shift = shift_mu.shape[0]
    c4 = n_shift + 3 * d_b
    row = lambda t: t.reshape(1, -1)
    w_main = jnp.concatenate([w_in[:, :c4], w_in[:, c4 + n_heads_b:]], axis=1).astype(BF16)
    w_f = jnp.pad(w_in[:, c4:c4 + n_heads_b], ((0, 0), (0, LANES - n_heads_b))).astype(BF16)
    dl, al = w2.shape[0], a2.shape[0]
    assert dl == HEAD_DIM and al == HEAD_DIM and g2.shape[0] == LANES
    lora = jnp.zeros((LANES, 2 * d_a), F32).at[:dl, :d_a].set(w2).at[dl:, d_a:].set(a2)
    bd = jnp.kron(jnp.eye(max(d_a, d_b) // HEAD_DIM, dtype=F32), jnp.ones((HEAD_DIM, HEAD_DIM), F32)).astype(BF16)
    return dict(
        d=d, d_a=d_a, d_b=d_b, n_shift=n_shift, n_heads_a=n_heads_a, n_heads_b=n_heads_b,
        ffn1=(row(ffn1_norm), ffn1_wg.astype(BF16), ffn1_wu.astype(BF16), ffn1_wd.astype(BF16)),
        ffn2=(row(ffn2_norm), ffn2_wg.astype(BF16), ffn2_wu.astype(BF16), ffn2_wd.astype(BF16)),
        mix_norm=row(mix_norm), w_main=w_main, w_f=w_f,
        b_f=jnp.pad(b_f, (0, LANES - n_heads_b)).reshape(1, LANES),
        q_norm=row(jnp.tile(q_norm, n_heads_b)), k_norm=row(jnp.tile(k_norm, n_heads_b)), bd=bd,
        mu=row(shift_mu), w0=row(w0), a0=row(a0), lora=lora, g2=g2, k_k=row(k_k), k_a=row(k_a),
        r_k=row(r_k), lnx_w=row(lnx_w), lnx_b=row(lnx_b),
        w_o_a=w_o_a.astype(BF16), w_o_b=w_o_b.astype(BF16), w_out=w_out.astype(BF16),
    )


def _layer(x3, prev_shift, s0, past, w):
    bsz, t, d = x3.shape
    d_a, d_b, ha, hb = w["d_a"], w["d_b"], w["n_heads_a"], w["n_heads_b"]
    n = bsz * t
    x = x3.reshape(n, d)
    x1 = _ffn(x, *w["ffn1"])
    init_rows = prev_shift if t >= min(_row_tile(n), 256) else jnp.repeat(prev_shift, t, axis=0)
    q, k, v, g_a, g_b, logf, cum, nck, r, km, va, kap, b, lw, g, new_shift = _mixproj(
        x1, w["mix_norm"], w["w_main"], w["w_f"], w["b_f"], w["q_norm"], w["k_norm"], w["bd"],
        init_rows, w["mu"], w["w0"], w["a0"], w["lora"], w["g2"], w["k_k"], w["k_a"],
        n_shift=w["n_shift"], d_a=d_a, d_b=d_b, n_heads=hb, seq_len=t)

    seq = lambda a: a.reshape(bsz, t, d_a)
    y_raw, new_s = _rscan(seq(r), seq(km), seq(va), seq(kap), seq(b), seq(lw), s0)
    y_raw = y_raw.reshape(n, d_a)

    npair = d_b // LANES
    if past is None:
        tq = min(512, t)
        y_b = _attn(q.reshape(bsz, t, d_b), k.reshape(bsz, t, d_b), v.reshape(bsz, t, d_b),
                    nck.reshape(bsz, t, LANES), tq=tq)
    else:
        cache_k, cache_v, cache_lf, page_table = past
        n_pool, page = cache_k.shape[:2]
        cn = cum.reshape(bsz, t, hb)
        cn_col = cn.transpose(0, 2, 1).reshape(bsz, hb * t, 1)
        tpos = jnp.arange(t)
        ok = tpos[None, :] <= tpos[:, None]
        bn = cn.transpose(0, 2, 1)[:, :, :, None] - cn.transpose(0, 2, 1)[:, :, None, :]
        bn = jnp.where(ok[None, None], bn, NEG_BIG).reshape(bsz, hb * t, t)
        bn = jnp.pad(bn, ((0, 0), (0, 0), (0, page - t)), constant_values=NEG_BIG)
        pages_t = lambda c: c.transpose(0, 2, 3, 1).reshape(n_pool, d_b, page)
        new_t = lambda a: jnp.pad(a.reshape(bsz, t, d_b).transpose(0, 2, 1), ((0, 0), (0, 0), (0, page - t)))
        pg = next(g for g in (32, 16, 8, 4, 2, 1) if page_table.shape[1] % g == 0)
        y_b = _dattn(page_table, q.reshape(bsz, t, d_b), cn_col, pages_t(cache_k), pages_t(cache_v),
                     cache_lf.transpose(0, 2, 1), new_t(k), new_t(v), bn, n_heads=hb, t_new=t, pg=pg)
    y_b = y_b.reshape(n, d_b)

    x2 = _merge(x1, y_raw, r, km, va, g, y_b, g_a, g_b, w["lnx_w"], w["lnx_b"], w["r_k"], w["bd"],
                w["w_o_a"], w["w_o_b"], w["w_out"])
    x3o = _ffn(x2, *w["ffn2"]).reshape(bsz, t, d)
    return x3o, (k.reshape(bsz, t, hb, HEAD_DIM), v.reshape(bsz, t, hb, HEAD_DIM),
                 logf.reshape(bsz, t, hb), new_s, new_shift)


def kernel(x_prompt, x_sample, cache_k, cache_v, cache_logf, state_wkv, state_shift, page_table,
           ffn1_norm, ffn1_wg, ffn1_wu, ffn1_wd, mix_norm, w_in, shift_mu,
           rwkv_w0, rwkv_w2, rwkv_a0, rwkv_a2, rwkv_g2, rwkv_k_k, rwkv_k_a, rwkv_r_k, rwkv_lnx_w, rwkv_lnx_b,
           fox_b_f, fox_q_norm, fox_k_norm, w_o_a, w_o_b, w_out,
           ffn2_norm, ffn2_wg, ffn2_wu, ffn2_wd):
    depth = w_in.shape[0]
    n_heads_a = state_wkv.shape[2]
    n_heads_b = cache_k.shape[3]
    bp = x_prompt.shape[0]
    xp, xs = x_prompt, x_sample
    outs_p, outs_s = [], []
    for l in range(depth):
        lw = (ffn1_norm[l], ffn1_wg[l], ffn1_wu[l], ffn1_wd[l], mix_norm[l], w_in[l], shift_mu[l],
              rwkv_w0[l], rwkv_w2[l], rwkv_a0[l], rwkv_a2[l], rwkv_g2[l], rwkv_k_k[l], rwkv_k_a[l],
              rwkv_r_k[l].reshape(-1), rwkv_lnx_w[l], rwkv_lnx_b[l], fox_b_f[l], fox_q_norm[l], fox_k_norm[l],
              w_o_a[l], w_o_b[l], w_out[l], ffn2_norm[l], ffn2_wg[l], ffn2_wu[l], ffn2_wd[l])
        w = _prep_weights(lw, n_heads_a, n_heads_b)
        prev0 = jnp.zeros((bp, w["n_shift"]), xp.dtype)
        s00 = jnp.zeros((bp, n_heads_a, HEAD_DIM, HEAD_DIM), xp.dtype)
        xp, op = _layer(xp, prev0, s00, None, w)
        xs, os_ = _layer(xs, state_shift[l], state_wkv[l], (cache_k[l], cache_v[l], cache_logf[l], page_table), w)
        outs_p.append(op)
        outs_s.append(os_)
    stack = lambda outs, i: jnp.stack([o[i] for o in outs], 0)
    return ((xp, xs) + tuple(stack(outs_p, i) for i in range(5)) + tuple(stack(outs_s, i) for i in range(5)))
```

```python
import functools

import jax
import jax.numpy as jnp
from jax import lax
from jax.experimental import pallas as pl
from jax.experimental.pallas import tpu as pltpu

F32 = jnp.float32
BF16 = jnp.bfloat16

HEAD_DIM = 64
LANES = 128
PAIR = LANES // HEAD_DIM
NORM_EPS = 1e-6
LNX_EPS = 64e-5
DECAY_SCALE = 0.6065306597126334
NEG_BIG = -1e30
BIAS_GROUP = 16
CHUNK = 64
VMEM_LIMIT = 56 * 1024 * 1024

NN = (((1,), (0,)), ((), ()))
NT = (((1,), (1,)), ((), ()))
TN = (((0,), (0,)), ((), ()))


def _pieces(x, n):
    if x.dtype == BF16:
        return [x]
    out = []
    rem = x
    for i in range(n):
        p = rem.astype(BF16)
        out.append(p)
        if i + 1 < n:
            rem = rem - p.astype(F32)
    return out


def _mm(a, b, dims=NN, pa=1, pb=1):
    pa_l = _pieces(a, pa)
    pb_l = _pieces(b, pb)
    if len(pa_l) == 2 and len(pb_l) == 2:
        m_axis = 1 - dims[0][0][0]
        m = a.shape[m_axis]
        both = lax.dot_general(jnp.concatenate(pa_l, axis=m_axis), pb_l[0], dims, preferred_element_type=F32)
        return both[0:m] + both[m:] + lax.dot_general(pa_l[0], pb_l[1], dims, preferred_element_type=F32)
    order = max(len(pa_l), len(pb_l))
    acc = None
    for i in reversed(range(len(pa_l))):
        for j in reversed(range(len(pb_l))):
            if i + j >= order:
                continue
            t = lax.dot_general(pa_l[i], pb_l[j], dims, preferred_element_type=F32)
            acc = t if acc is None else acc + t
    return acc


def _sigmoid(x):
    return 1.0 / (1.0 + jnp.exp(-x))


def _rms(x, g):
    return x * lax.rsqrt(jnp.mean(x * x, axis=-1, keepdims=True) + NORM_EPS) * g


def _iota(shape, dim):
    return lax.broadcasted_iota(jnp.int32, shape, dim)


def _div_pow2(x, n):
    assert n & (n - 1) == 0
    return lax.shift_right_logical(x, n.bit_length() - 1)


def _mod_pow2(x, n):
    assert n & (n - 1) == 0
    return x & (n - 1)


def _resident(shape):
    nd = len(shape)
    return pl.BlockSpec(shape, lambda *_: (0,) * nd, pipeline_mode=pl.Buffered(1))


def _params(sem):
    return pltpu.CompilerParams(dimension_semantics=sem, vmem_limit_bytes=VMEM_LIMIT)


def _row_tile(n):
    for t in (512, 256, 128, 64, 32, 16, 8):
        if n % t == 0:
            return t
    raise ValueError(f"row count {n} is not a multiple of 8")


def _ffn_kernel(x_ref, g_ref, wg_ref, wu_ref, wd_ref, o_ref):
    x = x_ref[...]
    h = _rms(x, g_ref[...]).astype(BF16)
    gate = jnp.dot(h, wg_ref[...], preferred_element_type=F32)
    up = jnp.dot(h, wu_ref[...], preferred_element_type=F32)
    act = (gate * _sigmoid(gate) * up).astype(BF16)
    o_ref[...] = x + 0.5 * jnp.dot(act, wd_ref[...], preferred_element_type=F32)


def _ffn(x, g, wg, wu, wd):
    n, d = x.shape
    dff = wg.shape[1]
    tm = _row_tile(n)
    return pl.pallas_call(
        _ffn_kernel,
        grid=(n // tm,),
        in_specs=[pl.BlockSpec((tm, d), lambda i: (i, 0)), _resident((1, d)),
                  _resident((d, dff)), _resident((d, dff)), _resident((dff, d))],
        out_specs=pl.BlockSpec((tm, d), lambda i: (i, 0)),
        out_shape=jax.ShapeDtypeStruct((n, d), F32),
        compiler_params=_params(("arbitrary",)),
        name="ffn",
    )(x, g, wg, wu, wd)


def _head_sum(x, bd_ref, pieces):
    return _mm(x, bd_ref[...], NN, pa=pieces, pb=1)


def _rwkv_prep(p, i, init_ref, mu_ref, w0_ref, a0_ref, lora_ref, g2_ref, kk_ref, ka_ref, bd_ref,
               r_ref, k_ref, v_ref, kap_ref, b_ref, lw_ref, g_ref, last_ref, *, d_a, seq_len, tm):
    prev = pltpu.roll(p, 1, 0)
    rowi = _iota(p.shape, 0)
    if seq_len >= tm:
        tiles_per_seq = seq_len // tm

        @pl.when(i % tiles_per_seq == 0)
        def _():
            last_ref[...] = init_ref[...]

        prev = jnp.where(rowi == 0, last_ref[...], prev)
        last_ref[...] = p[tm - 1:tm, :]
    else:
        prev = jnp.where(_mod_pow2(rowi, seq_len) == 0, init_ref[...], prev)
    xm = p + (prev - p) * mu_ref[...]
    r = xm[:, 0:d_a]
    k = xm[:, d_a:2 * d_a]
    v = xm[:, 2 * d_a:3 * d_a]
    xwa = xm[:, 3 * d_a:3 * d_a + LANES]
    xg = xm[:, 3 * d_a + LANES:3 * d_a + 2 * LANES]
    xwa = jnp.where(_iota(xwa.shape, 1) < HEAD_DIM, jnp.tanh(xwa), xwa)
    lo = _mm(xwa, lora_ref[...], NN, pa=2, pb=2)
    lw_ref[...] = -DECAY_SCALE * _sigmoid(w0_ref[...] + lo[:, 0:d_a])
    a = _sigmoid(a0_ref[...] + lo[:, d_a:2 * d_a])
    g_ref[...] = _mm(_sigmoid(xg), g2_ref[...], NN)
    kk = k * kk_ref[...]
    ss = _head_sum(kk * kk, bd_ref, 1)
    kk = kk * lax.rsqrt(jnp.maximum(ss, 1e-24))
    r_ref[...] = r
    k_ref[...] = k * (1.0 + (a - 1.0) * ka_ref[...])
    v_ref[...] = v
    kap_ref[...] = kk
    b_ref[...] = kk * a


def _mixproj_kernel(x_ref, g_ref, w_ref, wf_ref, bf_ref, qn_ref, kn_ref, bd_ref, *rest,
                    n_shift, d_a, d_b, d_model, n_heads, seq_len, tm):
    prep_in = rest[0:8]
    q_ref, k_ref, v_ref, ga_ref, gb_ref, lf_ref, cum_ref, nck_ref = rest[8:16]
    prep_out = rest[16:23]
    sh_ref, carry_ref, last_ref = rest[23:]
    i = pl.program_id(0)
    h = _rms(x_ref[...], g_ref[...]).astype(BF16)
    c1 = n_shift
    c2 = c1 + d_b
    c3 = c2 + d_b
    c4 = c3 + d_b
    c5 = c4 + d_model
    c6 = c5 + d_model
    p_a = jnp.dot(h, w_ref[:, 0:c1], preferred_element_type=F32)
    q = jnp.dot(h, w_ref[:, c1:c2], preferred_element_type=F32)
    k = jnp.dot(h, w_ref[:, c2:c3], preferred_element_type=F32)
    v_ref[...] = jnp.dot(h, w_ref[:, c3:c4], preferred_element_type=F32)
    ga_ref[...] = _sigmoid(jnp.dot(h, w_ref[:, c4:c5], preferred_element_type=F32)).astype(BF16)
    gb_ref[...] = _sigmoid(jnp.dot(h, w_ref[:, c5:c6], preferred_element_type=F32)).astype(BF16)
    inv_hd = 1.0 / HEAD_DIM
    q = q * lax.rsqrt(_head_sum(q * q, bd_ref, 1) * inv_hd + NORM_EPS) * qn_ref[...]
    k = k * lax.rsqrt(_head_sum(k * k, bd_ref, 1) * inv_hd + NORM_EPS) * kn_ref[...]
    q_ref[...] = q * (1.0 / float(HEAD_DIM) ** 0.5)
    k_ref[...] = k
    z = jnp.dot(h, wf_ref[...], preferred_element_type=F32) + bf_ref[...]
    logf = jnp.minimum(z, 0.0) - jnp.log1p(jnp.exp(-jnp.abs(z)))
    logf = jnp.where(_iota(logf.shape, 1) < n_heads, logf, 0.0)
    row = _iota((tm, tm), 0)
    col = _iota((tm, tm), 1)
    keep = col <= row
    if seq_len < tm:
        keep = keep & (_div_pow2(row, seq_len) == _div_pow2(col, seq_len))
    tri = jnp.where(keep, 1.0, 0.0).astype(BF16)
    cum = _mm(tri, logf, NN, pa=1, pb=3)
    if seq_len > tm:
        tiles_per_seq = seq_len // tm

        @pl.when(i % tiles_per_seq == 0)
        def _():
            carry_ref[...] = jnp.zeros_like(carry_ref)

        cum = cum + carry_ref[...]
        carry_ref[...] = cum[tm - 1:tm, :]
    lf_ref[...] = logf[:, :n_heads]
    cum_ref[...] = cum[:, :n_heads]
    src = _iota((LANES, LANES), 0)
    dst = _iota((LANES, LANES), 1)
    sub = _mod_pow2(dst, BIAS_GROUP)
    hit = (_div_pow2(dst, BIAS_GROUP) == _div_pow2(src, PAIR)) & (sub < 3 * PAIR) & (src < n_heads) \
        & ((sub >= 3) == (_mod_pow2(src, PAIR) == 1))
    nck_ref[...] = _mm(cum, jnp.where(hit, -1.0, 0.0).astype(BF16), NN, pa=3, pb=1)
    _rwkv_prep(p_a, i, *prep_in, bd_ref, *prep_out, last_ref, d_a=d_a, seq_len=seq_len, tm=tm)
    if seq_len >= tm:
        sh_ref[...] = p_a[tm - 1:tm, :]
    else:
        n_seq = tm // seq_len
        last = _iota((n_seq, tm), 1) == _iota((n_seq, tm), 0) * seq_len + (seq_len - 1)
        sh_ref[...] = _mm(jnp.where(last, 1.0, 0.0).astype(BF16), p_a, NN, pa=1, pb=3)


def _mixproj(x, g, w_main, w_f, b_f, qn, kn, bd, init_rows, mu, w0, a0, lora, g2, k_k, k_a,
             *, n_shift, d_a, d_b, n_heads, seq_len):
    n, d = x.shape
    tm = min(_row_tile(n), 256)
    assert seq_len % tm == 0 or tm % seq_len == 0
    row = lambda w: pl.BlockSpec((tm, w), lambda i: (i, 0))
    shp = lambda w: jax.ShapeDtypeStruct((n, w), F32)
    n_seq = n // seq_len
    if seq_len >= tm:
        tps = seq_len // tm
        init_spec = pl.BlockSpec((None, 1, n_shift), lambda i: (i // tps, 0, 0))
        init_rows = init_rows.reshape(n_seq, 1, n_shift)
        sh_spec = pl.BlockSpec((None, 1, n_shift), lambda i: (i // tps, 0, 0))
        sh_shape = jax.ShapeDtypeStruct((n_seq, 1, n_shift), F32)
    else:
        init_spec = row(n_shift)
        sh_spec = pl.BlockSpec((tm // seq_len, n_shift), lambda i: (i, 0))
        sh_shape = jax.ShapeDtypeStruct((n_seq, n_shift), F32)
    kern = functools.partial(_mixproj_kernel, n_shift=n_shift, d_a=d_a, d_b=d_b, d_model=d, n_heads=n_heads,
                             seq_len=seq_len, tm=tm)
    outs = pl.pallas_call(
        kern,
        grid=(n // tm,),
        in_specs=[row(d), _resident((1, d)), _resident(w_main.shape), _resident(w_f.shape),
                  _resident((1, LANES)), _resident((1, d_b)), _resident((1, d_b)), _resident(bd.shape),
                  init_spec, _resident((1, n_shift)), _resident((1, d_a)), _resident((1, d_a)),
                  _resident(lora.shape), _resident(g2.shape), _resident((1, d_a)), _resident((1, d_a))],
        out_specs=[row(d_b), row(d_b), row(d_b), row(d), row(d), row(n_heads), row(n_heads), row(LANES)]
                  + [row(d_a)] * 7 + [sh_spec],
        out_shape=[shp(d_b), shp(d_b), shp(d_b), jax.ShapeDtypeStruct((n, d), BF16), jax.ShapeDtypeStruct((n, d), BF16),
                   shp(n_heads), shp(n_heads), shp(LANES)]
                  + [shp(d_a)] * 7 + [sh_shape],
        scratch_shapes=[pltpu.VMEM((1, LANES), F32), pltpu.VMEM((1, n_shift), F32)],
        compiler_params=_params(("arbitrary",)),
        name="mixproj",
    )(x, g, w_main, w_f, b_f, qn, kn, bd, init_rows, mu, w0, a0, lora, g2, k_k, k_a)
    return tuple(outs[:-1]) + (outs[-1].reshape(n_seq, n_shift),)


def _stack(x, m0):
    return jnp.concatenate([jnp.where(m0, x, 0.0), jnp.where(m0, 0.0, x)], axis=0)


def _rscan_kernel(r_ref, k_ref, v_ref, kap_ref, b_ref, lw_ref, s0_ref, y_ref, st_ref, s_scr, *, bb, npair, c, hs, hc, hy):
    c2 = PAIR * c
    ci_ = pl.program_id(1)
    probs = [(i, j) for i in range(bb) for j in range(npair)]
    each = lambda f, *ls: [f(*xs) for xs in zip(*ls)]

    @pl.when(ci_ == 0)
    def _():
        zero = jnp.zeros((HEAD_DIM, HEAD_DIM), F32)
        for p, (i, j) in enumerate(probs):
            top = jnp.concatenate([s0_ref[i, PAIR * j], zero], axis=1)
            bot = jnp.concatenate([zero, s0_ref[i, PAIR * j + 1]], axis=1)
            s_scr[p] = jnp.concatenate([top, bot], axis=0)

    m0 = _iota((c, LANES), 1) < HEAD_DIM
    ri = _iota((c2, c2), 0)
    ci = _iota((c2, c2), 1)
    same = _div_pow2(ri, c) == _div_pow2(ci, c)
    strict = same & (ci < ri)
    incl = same & (ci <= ri)
    eye = ri == ci
    tri_c = jnp.where(_iota((c, c), 1) <= _iota((c, c), 0), 1.0, 0.0).astype(BF16)
    ld = lambda ref: [ref[i, :, j * LANES:(j + 1) * LANES] for (i, j) in probs]

    lw = ld(lw_ref)
    cum = each(lambda x: _mm(tri_c, x, NN, pa=1, pb=3), lw)
    cum_end = each(lambda x: x[c - 1:c, :], cum)
    e_neg = each(lambda x: jnp.exp(-x), cum)
    e_rem = each(lambda x, e: jnp.exp(e - x), cum, cum_end)
    kap, b, k = ld(kap_ref), ld(b_ref), ld(k_ref)
    a_t = each(lambda kp, x, w: _stack(-kp * jnp.exp(x - w), m0), kap, cum, lw)
    r_t = each(lambda r, x: _stack(r * jnp.exp(x), m0), ld(r_ref), cum)
    b_t = each(lambda x, e: _stack(x * e, m0), b, e_neg)
    k_t = each(lambda x, e: _stack(x * e, m0), k, e_neg)
    b_p = each(lambda x, e: _stack(x * e, m0), b, e_rem)
    k_p = each(lambda x, e: _stack(x * e, m0), k, e_rem)
    v_s = each(lambda x: _stack(x, m0), ld(v_ref))

    side = lambda x, y: jnp.concatenate([x, y], axis=1)
    bk_t = each(lambda x, y: jnp.concatenate([x, y], axis=0), b_t, k_t)
    sc_a = each(lambda x, y: _mm(x, y, NT, hc, hc), a_t, bk_t)
    sc_r = each(lambda x, y: _mm(x, y, NT, hy, hy), r_t, bk_t)
    l_ab = each(lambda x: jnp.where(strict, x[:, 0:c2], 0.0), sc_a)
    a_ak = each(lambda x: jnp.where(strict, x[:, c2:], 0.0), sc_a)
    a_rb = each(lambda x: jnp.where(incl, x[:, 0:c2], 0.0), sc_r)
    a_rk = each(lambda x: jnp.where(incl, x[:, c2:], 0.0), sc_r)

    t_inv = each(lambda x: jnp.where(eye, 1.0, 0.0) + x, l_ab)
    l_pow = each(lambda x: _mm(x, x, NN, hc, hc), l_ab)
    n_sq = c.bit_length() - 2
    for i in range(1, n_sq + 1):
        if i < n_sq:
            both = each(lambda x, t: _mm(jnp.concatenate([x, t], axis=0), x, NN, hc, hc), l_pow, t_inv)
            l_pow = each(lambda x: x[0:c2], both)
            t_inv = each(lambda t, x: t + x[c2:], t_inv, both)
        else:
            t_inv = each(lambda t, x: t + _mm(t, x, NN, hc, hc), t_inv, l_pow)

    akv = each(lambda x, y: _mm(x, y, NN, hc, hc), a_ak, v_s)
    au = each(lambda t, x, y: _mm(t, side(x, y), NN, hc, hc), t_inv, a_t, akv)
    a_p = each(lambda x: x[:, 0:LANES], au)
    u0 = each(lambda x: x[:, LANES:], au)
    ry = each(lambda x, y: _mm(x, y, NN, hy, hy), a_rb, au)
    r_p = each(lambda r, x: r + x[:, 0:LANES], r_t, ry)
    y0 = each(lambda x, y, z: _mm(x, y, NN, hy, hy) + z[:, LANES:], a_rk, v_s, ry)
    eye_k = _iota((LANES, LANES), 0) == _iota((LANES, LANES), 1)
    m_c = each(lambda e, x, y: jnp.where(eye_k, jnp.exp(e), 0.0) + _mm(x, y, TN, hc, hc), cum_end, a_p, b_p)
    n_c = each(lambda x, y, z, w: _mm(x, y, TN, hs, hs) + _mm(z, w, TN, hs, hs), u0, b_p, v_s, k_p)

    s_in = [s_scr[p] for p in range(len(probs))]
    y_st = each(lambda r, s, y: _mm(r, s, NT, hy, hy) + y, r_p, s_in, y0)
    s_out = each(lambda s, m, n: _mm(s, m, NN, hs, hs) + n, s_in, m_c, n_c)
    for p, (i, j) in enumerate(probs):
        y_ref[i, :, j * LANES:(j + 1) * LANES] = y_st[p][0:c, :] + y_st[p][c:c2, :]
        s_scr[p] = s_out[p]

    @pl.when(ci_ == pl.num_programs(1) - 1)
    def _():
        for p, (i, j) in enumerate(probs):
            st_ref[i, PAIR * j] = s_out[p][0:HEAD_DIM, 0:HEAD_DIM]
            st_ref[i, PAIR * j + 1] = s_out[p][HEAD_DIM:, HEAD_DIM:]


def _rscan(r, k, v, kap, b, lw, s0, *, hs=2, hc=1, hy=1):
    bsz, t, d_a = r.shape
    c = min(CHUNK, t)
    assert t % c == 0 and c % 8 == 0 and c & (c - 1) == 0
    npair = d_a // LANES
    bb = 2 if bsz % 2 == 0 else 1
    seq = pl.BlockSpec((bb, c, d_a), lambda g, ci: (g, ci, 0))
    st = pl.BlockSpec((bb, PAIR * npair, HEAD_DIM, HEAD_DIM), lambda g, ci: (g, 0, 0, 0))
    kern = functools.partial(_rscan_kernel, bb=bb, npair=npair, c=c, hs=hs, hc=hc, hy=hy)
    return pl.pallas_call(
        kern,
        grid=(bsz // bb, t // c),
        in_specs=[seq] * 6 + [st],
        out_specs=[seq, st],
        out_shape=[jax.ShapeDtypeStruct((bsz, t, d_a), F32),
                   jax.ShapeDtypeStruct(s0.shape, F32)],
        scratch_shapes=[pltpu.VMEM((bb * npair, LANES, LANES), F32)],
        compiler_params=_params(("arbitrary", "arbitrary")),
        name="rscan",
    )(r, k, v, kap, b, lw, s0)


def _lanes(x, n):
    return x if n == LANES else jnp.concatenate([x] * (n // LANES), axis=1)


def _attn_kernel(q_ref, k_ref, v_ref, kb_ref, o_ref, kaug_scr, vb_scr, s_scr, p_scr, m_scr, acc_scr,
                 *, tq, strip, ns):
    qi = pl.program_id(2)
    tk = tq
    rows2 = 2 * tq
    streams = range(ns)
    lanes_of = lambda st: slice(st * LANES, (st + 1) * LANES)

    @pl.when(qi == 0)
    def _():
        p0, p1, p2 = _pieces(kb_ref[...], 3)
        which = _mod_pow2(_iota(p0.shape, 1), BIAS_GROUP)
        bias = jnp.where((which == 0) | (which == 3), p0, jnp.where((which == 1) | (which == 4), p1, p2))
        for st in streams:
            kaug_scr[st, :, 0:LANES] = k_ref[:, lanes_of(st)].astype(BF16)
            kaug_scr[st, :, LANES:2 * LANES] = bias
            vb_scr[st, :, 0:LANES] = v_ref[:, lanes_of(st)].astype(BF16)
            vb_scr[st, :, LANES:2 * LANES] = jnp.ones((vb_scr.shape[1], LANES), BF16)

    m0 = _iota((tq, LANES), 1) < HEAD_DIM
    head1 = _iota((rows2, LANES), 0) >= tq
    q_aug = []
    for st in streams:
        lane = _iota((rows2, LANES), 1) - BIAS_GROUP * (ns * pl.program_id(1) + st)
        pick = (lane >= jnp.where(head1, 3, 0)) & (lane < jnp.where(head1, 6, 3))
        q_aug.append(jnp.concatenate([_stack(q_ref[:, lanes_of(st)], m0), jnp.where(pick, 1.0, 0.0)],
                                     axis=1).astype(BF16))
    m_scr[...] = jnp.full_like(m_scr, NEG_BIG)
    acc_scr[...] = jnp.zeros_like(acc_scr)

    def keys(ki):
        return pl.ds(pl.multiple_of(ki * tk, tk), tk)

    def tile(ki, diagonal):
        for st in streams:
            s_scr[st] = lax.dot_general(q_aug[st], kaug_scr[st, keys(ki), :], NT, preferred_element_type=F32)
        for st in streams:
            for r0 in range(0, rows2, strip):
                rs = slice(r0, r0 + strip)
                s = s_scr[st, rs, :]
                if diagonal:
                    s = jnp.where(_iota((strip, tk), 1) <= _iota((strip, tk), 0) + r0 % tq, s, NEG_BIG)
                m_old = m_scr[st, rs, :]
                m_new = jnp.maximum(m_old, jnp.max(s, axis=-1, keepdims=True))
                alpha = jnp.exp(m_old - m_new)
                acc_scr[st, rs, :] = _lanes(alpha, 2 * LANES) * acc_scr[st, rs, :]
                p_scr[st, rs, :] = jnp.exp((s - _lanes(m_new, tk)).astype(BF16))
                m_scr[st, rs, :] = m_new
            acc_scr[st] += jnp.dot(p_scr[st], vb_scr[st, keys(ki), :], preferred_element_type=F32)

    def off_diagonal(ki, carry):
        tile(ki, False)
        return carry

    lax.fori_loop(0, qi, off_diagonal, 0)
    tile(qi, True)
    for st in streams:
        o = acc_scr[st, :, 0:LANES] / acc_scr[st, :, LANES:2 * LANES]
        o_ref[:, lanes_of(st)] = jnp.where(m0, o[0:tq, :], o[tq:rows2, :])


def _attn(q, k, v, kb, *, tq):
    bsz, t, d_b = q.shape
    npair = d_b // LANES
    ns = 2 if npair % 2 == 0 else 1
    nq = t // tq
    qspec = pl.BlockSpec((None, tq, ns * LANES), lambda b, j, qi: (b, qi, j))
    kspec = pl.BlockSpec((None, t, ns * LANES), lambda b, j, qi: (b, 0, j))
    kbspec = pl.BlockSpec((None, t, LANES), lambda b, j, qi: (b, 0, 0))
    return pl.pallas_call(
        functools.partial(_attn_kernel, tq=tq, strip=min(64, tq), ns=ns),
        grid=(bsz, npair // ns, nq),
        in_specs=[qspec, kspec, kspec, kbspec],
        out_specs=qspec,
        out_shape=jax.ShapeDtypeStruct((bsz, t, d_b), F32),
        scratch_shapes=[pltpu.VMEM((ns, t, 2 * LANES), BF16), pltpu.VMEM((ns, t, 2 * LANES), BF16),
                        pltpu.VMEM((ns, 2 * tq, tq), F32), pltpu.VMEM((ns, 2 * tq, tq), BF16),
                        pltpu.VMEM((ns, 2 * tq, LANES), F32), pltpu.VMEM((ns, 2 * tq, 2 * LANES), F32)],
        compiler_params=_params(("arbitrary",) * 3),
        name="attn",
    )(q, k, v, kb)


def _dattn_kernel(pt_ref, q_ref, cn_ref, kn_ref, vn_ref, bn_ref, kc_hbm, vc_hbm, lf_hbm, o_ref,
                  kbuf, vbuf, lbuf, sems, wq_scr, m_scr, l_scr, acc_scr, suf_scr,
                  *, n_heads, t_new, page, pg, n_pages):
    b = pl.program_id(0)
    step = pl.program_id(1)
    n_b = pl.num_programs(0)
    n_s = pl.num_programs(1)
    n_rows = n_heads * t_new
    slot = (b * n_s + step) & 1

    def page_copies(bi, si, sl):
        out = []
        for i in range(pg):
            src = pt_ref[bi, n_pages - 1 - (si * pg + i)]
            out.append(pltpu.make_async_copy(kc_hbm.at[src], kbuf.at[sl, i], sems.at[0, sl]))
            out.append(pltpu.make_async_copy(vc_hbm.at[src], vbuf.at[sl, i], sems.at[1, sl]))
            out.append(pltpu.make_async_copy(lf_hbm.at[src], lbuf.at[sl, i], sems.at[2, sl]))
        return out

    @pl.when((b == 0) & (step == 0))
    def _():
        for c in page_copies(0, 0, 0):
            c.start()

    wraps = step + 1 == n_s

    @pl.when(jnp.logical_not(wraps & (b + 1 == n_b)))
    def _():
        for c in page_copies(jnp.where(wraps, b + 1, b), jnp.where(wraps, 0, step + 1), 1 - slot):
            c.start()

    @pl.when(step == 0)
    def _():
        q = q_ref[...]
        head = _div_pow2(_iota(q.shape, 1), HEAD_DIM)
        wq_scr[...] = jnp.concatenate([jnp.where(head == h, q, 0.0) for h in range(n_heads)],
                                      axis=0).astype(BF16)
        m_scr[...] = jnp.full_like(m_scr, NEG_BIG)
        l_scr[...] = jnp.zeros_like(l_scr)
        acc_scr[...] = jnp.zeros_like(acc_scr)
        suf_scr[...] = jnp.zeros_like(suf_scr)

    for c in page_copies(b, step, slot):
        c.wait()

    def update(s, vals_t):
        n = len(vals_t)
        m_old = m_scr[...]
        m_new = jnp.maximum(m_old, jnp.max(s, axis=-1, keepdims=True))
        alpha = jnp.exp(m_old - m_new)
        p = jnp.exp(s - _lanes(m_new, n * page))
        l_scr[...] = alpha * l_scr[...] + jnp.sum(p, axis=-1, keepdims=True)
        p = p.astype(BF16)
        pv = None
        for i in range(n):
            t = lax.dot_general(p[:, i * page:(i + 1) * page], vals_t[i].astype(BF16), NT,
                                preferred_element_type=F32)
            pv = t if pv is None else pv + t
        acc_scr[...] = _lanes(alpha, acc_scr.shape[1]) * acc_scr[...] + pv
        m_scr[...] = m_new

    wq = wq_scr[...]
    lfs = [lbuf[slot, i] for i in range(pg)]
    lf_all = jnp.concatenate(lfs, axis=0)
    later = jnp.where(_iota((page, page), 0) > _iota((page, page), 1), 1.0, 0.0).astype(BF16)
    within = _mm(lf_all, later, NN, pa=3, pb=1)
    carry = suf_scr[...]
    cn = jnp.broadcast_to(cn_ref[...], (n_rows, page))
    parts = []
    for i in range(pg):
        suf = within[i * n_heads:(i + 1) * n_heads, :] + carry
        carry = carry + jnp.sum(lfs[i], axis=-1, keepdims=True)
        bias = jnp.broadcast_to(suf[:, None, :], (n_heads, t_new, page)).reshape(n_rows, page)
        parts.append(jnp.dot(wq, kbuf[slot, i].astype(BF16), preferred_element_type=F32) + bias + cn)
    suf_scr[...] = carry
    update(jnp.concatenate(parts, axis=1), [vbuf[slot, i] for i in range(pg)])

    @pl.when(step == n_s - 1)
    def _():
        s_new = jnp.dot(wq, kn_ref[...].astype(BF16), preferred_element_type=F32)
        update(s_new + bn_ref[...], [vn_ref[...]])
        o = acc_scr[...] / _lanes(l_scr[...], acc_scr.shape[1])
        head = _div_pow2(_iota((t_new, o.shape[1]), 1), HEAD_DIM)
        out = jnp.zeros((t_new, o.shape[1]), F32)
        for h in range(n_heads):
            out = out + jnp.where(head == h, o[h * t_new:(h + 1) * t_new, :], 0.0)
        o_ref[...] = out


def _dattn(page_table, q, cn_col, cache_kt, cache_vt, cache_lf_t, kt_new, vt_new, bias_new, *, n_heads, t_new, pg):
    bsz, n_pages = page_table.shape
    _, d_b, page = cache_kt.shape
    n_rows = n_heads * t_new
    assert n_pages % pg == 0
    per_b = lambda *blk: pl.BlockSpec((None,) + blk, lambda b, s, pt: (b,) + (0,) * len(blk))
    hbm = pl.BlockSpec(memory_space=pl.ANY)
    grid_spec = pltpu.PrefetchScalarGridSpec(
        num_scalar_prefetch=1,
        grid=(bsz, n_pages // pg),
        in_specs=[per_b(t_new, d_b), per_b(n_rows, 1), per_b(d_b, page), per_b(d_b, page), per_b(n_rows, page),
                  hbm, hbm, hbm],
        out_specs=per_b(t_new, d_b),
        scratch_shapes=[pltpu.VMEM((2, pg, d_b, page), F32), pltpu.VMEM((2, pg, d_b, page), F32),
                        pltpu.VMEM((2, pg, n_heads, page), F32), pltpu.SemaphoreType.DMA((3, 2)),
                        pltpu.VMEM((n_rows, d_b), BF16), pltpu.VMEM((n_rows, LANES), F32),
                        pltpu.VMEM((n_rows, LANES), F32), pltpu.VMEM((n_rows, d_b), F32),
                        pltpu.VMEM((n_heads, 1), F32)],
    )
    kern = functools.partial(_dattn_kernel, n_heads=n_heads, t_new=t_new, page=page, pg=pg, n_pages=n_pages)
    return pl.pallas_call(
        kern,
        grid_spec=grid_spec,
        out_shape=jax.ShapeDtypeStruct((bsz, t_new, d_b), F32),
        compiler_params=_params(("arbitrary", "arbitrary")),
        name="dattn",
    )(page_table, q, cn_col, kt_new, vt_new, bias_new, cache_kt, cache_vt, cache_lf_t)


def _merge_kernel(x_ref, y_ref, r_ref, k_ref, v_ref, g_ref, yb_ref, ga_ref, gb_ref,
                  lnw_ref, lnb_ref, rk_ref, bd_ref, woa_ref, wob_ref, wout_ref, o_ref):
    inv_hd = 1.0 / HEAD_DIM
    y = y_ref[...]
    mean = _head_sum(y, bd_ref, 1) * inv_hd
    yc = y - mean
    var = _head_sum(yc * yc, bd_ref, 1) * inv_hd
    yn = yc * lax.rsqrt(var + LNX_EPS) * lnw_ref[...] + lnb_ref[...]
    v = v_ref[...]
    bonus = _head_sum(r_ref[...] * k_ref[...] * rk_ref[...], bd_ref, 1) * v
    ya = ((yn + bonus) * g_ref[...]).astype(BF16)
    pa = jnp.dot(ya, woa_ref[...], preferred_element_type=F32)
    pb = jnp.dot(yb_ref[...].astype(BF16), wob_ref[...], preferred_element_type=F32)
    merged = ga_ref[...].astype(F32) * pa + gb_ref[...].astype(F32) * pb
    o_ref[...] = x_ref[...] + jnp.dot(merged.astype(BF16), wout_ref[...], preferred_element_type=F32)


def _merge(x, y, r, k, v, g, yb, ga, gb, lnw, lnb, rk, bd, woa, wob, wout):
    n, d = x.shape
    d_a = y.shape[1]
    tm = _row_tile(n)
    row = lambda w: pl.BlockSpec((tm, w), lambda i: (i, 0))
    return pl.pallas_call(
        _merge_kernel,
        grid=(n // tm,),
        in_specs=[row(d)] + [row(d_a)] * 6 + [row(d), row(d)] + [_resident((1, d_a))] * 3
                 + [_resident(bd.shape), _resident(woa.shape), _resident(wob.shape), _resident(wout.shape)],
        out_specs=row(d),
        out_shape=jax.ShapeDtypeStruct((n, d), F32),
        compiler_params=_params(("arbitrary",)),
        name="merge",
    )(x, y, r, k, v, g, yb, ga, gb, lnw, lnb, rk, bd, woa, wob, wout)


def _prep_weights(lw, n_heads_a, n_heads_b):
    (ffn1_norm, ffn1_wg, ffn1_wu, ffn1_wd, mix_norm, w_in, shift_mu, w0, w2, a0, a2, g2, k_k, k_a, r_k,
     lnx_w, lnx_b, b_f, q_norm, k_norm, w_o_a, w_o_b, w_out, ffn2_norm, ffn2_wg, ffn2_wu, ffn2_wd) = lw
    d = w_in.shape[0]
    d_a = n_heads_a * HEAD_DIM
    d_b = n_heads_b * HEAD_DIM
    n_shift = shift_mu.shape[0]
    c4 = n_shift + 3 * d_b
    row = lambda t: t.reshape(1, -1)
    w_main = jnp.concatenate([w_in[:, :c4], w_in[:, c4 + n_heads_b:]], axis=1).astype(BF16)
    w_f = jnp.pad(w_in[:, c4:c4 + n_heads_b], ((0, 0), (0, LANES - n_heads_b))).astype(BF16)
    dl, al = w2.shape[0], a2.shape[0]
    assert dl == HEAD_DIM and al == HEAD_DIM and g2.shape[0] == LANES
    lora = jnp.zeros((LANES, 2 * d_a), F32).at[:dl, :d_a].set(w2).at[dl:, d_a:].set(a2)
    bd = jnp.kron(jnp.eye(max(d_a, d_b) // HEAD_DIM, dtype=F32), jnp.ones((HEAD_DIM, HEAD_DIM), F32)).astype(BF16)
    return dict(
        d=d, d_a=d_a, d_b=d_b, n_shift=n_shift, n_heads_a=n_heads_a, n_heads_b=n_heads_b,
        ffn1=(row(ffn1_norm), ffn1_wg.astype(BF16), ffn1_wu.astype(BF16), ffn1_wd.astype(BF16)),
        ffn2=(row(ffn2_norm), ffn2_wg.astype(BF16), ffn2_wu.astype(BF16), ffn2_wd.astype(BF16)),
        mix_norm=row(mix_norm), w_main=w_main, w_f=w_f,
        b_f=jnp.pad(b_f, (0, LANES - n_heads_b)).reshape(1, LANES),
        q_norm=row(jnp.tile(q_norm, n_heads_b)), k_norm=row(jnp.tile(k_norm, n_heads_b)), bd=bd,
        mu=row(shift_mu), w0=row(w0), a0=row(a0), lora=lora, g2=g2, k_k=row(k_k), k_a=row(k_a),
        r_k=row(r_k), lnx_w=row(lnx_w), lnx_b=row(lnx_b),
        w_o_a=w_o_a.astype(BF16), w_o_b=w_o_b.astype(BF16), w_out=w_out.astype(BF16),
    )


def _layer(x3, prev_shift, s0, past, w):
    bsz, t, d = x3.shape
    d_a, d_b, ha, hb = w["d_a"], w["d_b"], w["n_heads_a"], w["n_heads_b"]
    n = bsz * t
    x = x3.reshape(n, d)
    x1 = _ffn(x, *w["ffn1"])
    init_rows = prev_shift if t >= min(_row_tile(n), 256) else jnp.repeat(prev_shift, t, axis=0)
    q, k, v, g_a, g_b, logf, cum, nck, r, km, va, kap, b, lw, g, new_shift = _mixproj(
        x1, w["mix_norm"], w["w_main"], w["w_f"], w["b_f"], w["q_norm"], w["k_norm"], w["bd"],
        init_rows, w["mu"], w["w0"], w["a0"], w["lora"], w["g2"], w["k_k"], w["k_a"],
        n_shift=w["n_shift"], d_a=d_a, d_b=d_b, n_heads=hb, seq_len=t)

    seq = lambda a: a.reshape(bsz, t, d_a)
    y_raw, new_s = _rscan(seq(r), seq(km), seq(va), seq(kap), seq(b), seq(lw), s0)
    y_raw = y_raw.reshape(n, d_a)

    npair = d_b // LANES
    if past is None:
        tq = min(512, t)
        y_b = _attn(q.reshape(bsz, t, d_b), k.reshape(bsz, t, d_b), v.reshape(bsz, t, d_b),
                    nck.reshape(bsz, t, LANES), tq=tq)
    else:
        cache_k, cache_v, cache_lf, page_table = past
        n_pool, page = cache_k.shape[:2]
        cn = cum.reshape(bsz, t, hb)
        cn_col = cn.transpose(0, 2, 1).reshape(bsz, hb * t, 1)
        tpos = jnp.arange(t)
        ok = tpos[None, :] <= tpos[:, None]
        bn = cn.transpose(0, 2, 1)[:, :, :, None] - cn.transpose(0, 2, 1)[:, :, None, :]
        bn = jnp.where(ok[None, None], bn, NEG_BIG).reshape(bsz, hb * t, t)
        bn = jnp.pad(bn, ((0, 0), (0, 0), (0, page - t)), constant_values=NEG_BIG)
        pages_t = lambda c: c.transpose(0, 2, 3, 1).reshape(n_pool, d_b, page)
        new_t = lambda a: jnp.pad(a.reshape(bsz, t, d_b).transpose(0, 2, 1), ((0, 0), (0, 0), (0, page - t)))
        pg = next(g for g in (32, 16, 8, 4, 2, 1) if page_table.shape[1] % g == 0)
        y_b = _dattn(page_table, q.reshape(bsz, t, d_b), cn_col, pages_t(cache_k), pages_t(cache_v),
                     cache_lf.transpose(0, 2, 1), new_t(k), new_t(v), bn, n_heads=hb, t_new=t, pg=pg)
    y_b = y_b.reshape(n, d_b)

    x2 = _merge(x1, y_raw, r, km, va, g, y_b, g_a, g_b, w["lnx_w"], w["lnx_b"], w["r_k"], w["bd"],
                w["w_o_a"], w["w_o_b"], w["w_out"])
    x3o = _ffn(x2, *w["ffn2"]).reshape(bsz, t, d)
    return x3o, (k.reshape(bsz, t, hb, HEAD_DIM), v.reshape(bsz, t, hb, HEAD_DIM),
                 logf.reshape(bsz, t, hb), new_s, new_shift)


def kernel(x_prompt, x_sample, cache_k, cache_v, cache_logf, state_wkv, state_shift, page_table,
           ffn1_norm, ffn1_wg, ffn1_wu, ffn1_wd, mix_norm, w_in, shift_mu,
           rwkv_w0, rwkv_w2, rwkv_a0, rwkv_a2, rwkv_g2, rwkv_k_k, rwkv_k_a, rwkv_r_k, rwkv_lnx_w, rwkv_lnx_b,
           fox_b_f, fox_q_norm, fox_k_norm, w_o_a, w_o_b, w_out,
           ffn2_norm, ffn2_wg, ffn2_wu, ffn2_wd):
    depth = w_in.shape[0]
    n_heads_a = state_wkv.shape[2]
    n_heads_b = cache_k.shape[3]
    bp = x_prompt.shape[0]
    xp, xs = x_prompt, x_sample
    outs_p, outs_s = [], []
    for l in range(depth):
        lw = (ffn1_norm[l], ffn1_wg[l], ffn1_wu[l], ffn1_wd[l], mix_norm[l], w_in[l], shift_mu[l],
              rwkv_w0[l], rwkv_w2[l], rwkv_a0[l], rwkv_a2[l], rwkv_g2[l], rwkv_k_k[l], rwkv_k_a[l],
              rwkv_r_k[l].reshape(-1), rwkv_lnx_w[l], rwkv_lnx_b[l], fox_b_f[l], fox_q_norm[l], fox_k_norm[l],
              w_o_a[l], w_o_b[l], w_out[l], ffn2_norm[l], ffn2_wg[l], ffn2_wu[l], ffn2_wd[l])
        w = _prep_weights(lw, n_heads_a, n_heads_b)
        prev0 = jnp.zeros((bp, w["n_shift"]), xp.dtype)
        s00 = jnp.zeros((bp, n_heads_a, HEAD_DIM, HEAD_DIM), xp.dtype)
        xp, op = _layer(xp, prev0, s00, None, w)
        xs, os_ = _layer(xs, state_shift[l], state_wkv[l], (cache_k[l], cache_v[l], cache_logf[l], page_table), w)
        outs_p.append(op)
        outs_s.append(os_)
    stack = lambda outs, i: jnp.stack([o[i] for o in outs], 0)
    return ((xp, xs) + tuple(stack(outs_p, i) for i in range(5)) + tuple(stack(outs_s, i) for i in range(5)))
```

```python
import functools

import jax
import jax.numpy as jnp
from jax import lax
from jax.experimental import pallas as pl
from jax.experimental.pallas import tpu as pltpu

F32 = jnp.float32
BF16 = jnp.bfloat16

HEAD_DIM = 64
LANES = 128
PAIR = LANES // HEAD_DIM
NORM_EPS = 1e-6
LNX_EPS = 64e-5
DECAY_SCALE = 0.6065306597126334
NEG_BIG = -1e30
BIAS_GROUP = 16
CHUNK = 64
VMEM_LIMIT = 56 * 1024 * 1024

NN = (((1,), (0,)), ((), ()))
NT = (((1,), (1,)), ((), ()))
TN = (((0,), (0,)), ((), ()))


def _pieces(x, n):
    if x.dtype == BF16:
        return [x]
    out = []
    rem = x
    for i in range(n):
        p = rem.astype(BF16)
        out.append(p)
        if i + 1 < n:
            rem = rem - p.astype(F32)
    return out


def _mm(a, b, dims=NN, pa=1, pb=1):
    pa_l = _pieces(a, pa)
    pb_l = _pieces(b, pb)
    if len(pa_l) == 2 and len(pb_l) == 2:
        m_axis = 1 - dims[0][0][0]
        m = a.shape[m_axis]
        both = lax.dot_general(jnp.concatenate(pa_l, axis=m_axis), pb_l[0], dims, preferred_element_type=F32)
        return both[0:m] + both[m:] + lax.dot_general(pa_l[0], pb_l[1], dims, preferred_element_type=F32)
    order = max(len(pa_l), len(pb_l))
    acc = None
    for i in reversed(range(len(pa_l))):
        for j in reversed(range(len(pb_l))):
            if i + j >= order:
                continue
            t = lax.dot_general(pa_l[i], pb_l[j], dims, preferred_element_type=F32)
            acc = t if acc is None else acc + t
    return acc


def _sigmoid(x):
    return 1.0 / (1.0 + jnp.exp(-x))


def _rms(x, g):
    return x * lax.rsqrt(jnp.mean(x * x, axis=-1, keepdims=True) + NORM_EPS) * g


def _iota(shape, dim):
    return lax.broadcasted_iota(jnp.int32, shape, dim)


def _div_pow2(x, n):
    assert n & (n - 1) == 0
    return lax.shift_right_logical(x, n.bit_length() - 1)


def _mod_pow2(x, n):
    assert n & (n - 1) == 0
    return x & (n - 1)


def _resident(shape):
    nd = len(shape)
    return pl.BlockSpec(shape, lambda *_: (0,) * nd, pipeline_mode=pl.Buffered(1))


def _params(sem):
    return pltpu.CompilerParams(dimension_semantics=sem, vmem_limit_bytes=VMEM_LIMIT)


def _row_tile(n):
    for t in (512, 256, 128, 64, 32, 16, 8):
        if n % t == 0:
            return t
    raise ValueError(f"row count {n} is not a multiple of 8")


def _ffn_kernel(x_ref, g_ref, wg_ref, wu_ref, wd_ref, o_ref):
    x = x_ref[...]
    h = _rms(x, g_ref[...]).astype(BF16)
    gate = jnp.dot(h, wg_ref[...], preferred_element_type=F32)
    up = jnp.dot(h, wu_ref[...], preferred_element_type=F32)
    act = (gate * _sigmoid(gate) * up).astype(BF16)
    o_ref[...] = x + 0.5 * jnp.dot(act, wd_ref[...], preferred_element_type=F32)


def _ffn(x, g, wg, wu, wd):
    n, d = x.shape
    dff = wg.shape[1]
    tm = _row_tile(n)
    return pl.pallas_call(
        _ffn_kernel,
        grid=(n // tm,),
        in_specs=[pl.BlockSpec((tm, d), lambda i: (i, 0)), _resident((1, d)),
                  _resident((d, dff)), _resident((d, dff)), _resident((dff, d))],
        out_specs=pl.BlockSpec((tm, d), lambda i: (i, 0)),
        out_shape=jax.ShapeDtypeStruct((n, d), F32),
        compiler_params=_params(("arbitrary",)),
        name="ffn",
    )(x, g, wg, wu, wd)


def _head_sum(x, bd_ref, pieces):
    return _mm(x, bd_ref[...], NN, pa=pieces, pb=1)


def _rwkv_prep(p, i, init_ref, mu_ref, w0_ref, a0_ref, lora_ref, g2_ref, kk_ref, ka_ref, bd_ref,
               r_ref, k_ref, v_ref, kap_ref, b_ref, lw_ref, g_ref, last_ref, *, d_a, seq_len, tm):
    prev = pltpu.roll(p, 1, 0)
    rowi = _iota(p.shape, 0)
    if seq_len >= tm:
        tiles_per_seq = seq_len // tm

        @pl.when(i % tiles_per_seq == 0)
        def _():
            last_ref[...] = init_ref[...]

        prev = jnp.where(rowi == 0, last_ref[...], prev)
        last_ref[...] = p[tm - 1:tm, :]
    else:
        prev = jnp.where(_mod_pow2(rowi, seq_len) == 0, init_ref[...], prev)
    xm = p + (prev - p) * mu_ref[...]
    r = xm[:, 0:d_a]
    k = xm[:, d_a:2 * d_a]
    v = xm[:, 2 * d_a:3 * d_a]
    xwa = xm[:, 3 * d_a:3 * d_a + LANES]
    xg = xm[:, 3 * d_a + LANES:3 * d_a + 2 * LANES]
    xwa = jnp.where(_iota(xwa.shape, 1) < HEAD_DIM, jnp.tanh(xwa), xwa)
    lo = _mm(xwa, lora_ref[...], NN, pa=2, pb=2)
    lw_ref[...] = -DECAY_SCALE * _sigmoid(w0_ref[...] + lo[:, 0:d_a])
    a = _sigmoid(a0_ref[...] + lo[:, d_a:2 * d_a])
    g_ref[...] = _mm(_sigmoid(xg), g2_ref[...], NN)
    kk = k * kk_ref[...]
    ss = _head_sum(kk * kk, bd_ref, 1)
    kk = kk * lax.rsqrt(jnp.maximum(ss, 1e-24))
    r_ref[...] = r
    k_ref[...] = k * (1.0 + (a - 1.0) * ka_ref[...])
    v_ref[...] = v
    kap_ref[...] = kk
    b_ref[...] = kk * a


def _mixproj_kernel(x_ref, g_ref, w_ref, wf_ref, bf_ref, qn_ref, kn_ref, bd_ref, *rest,
                    n_shift, d_a, d_b, d_model, n_heads, seq_len, tm):
    prep_in = rest[0:8]
    q_ref, k_ref, v_ref, ga_ref, gb_ref, lf_ref, cum_ref, nck_ref = rest[8:16]
    prep_out = rest[16:23]
    sh_ref, carry_ref, last_ref = rest[23:]
    i = pl.program_id(0)
    h = _rms(x_ref[...], g_ref[...]).astype(BF16)
    c1 = n_shift
    c2 = c1 + d_b
    c3 = c2 + d_b
    c4 = c3 + d_b
    c5 = c4 + d_model
    c6 = c5 + d_model
    p_a = jnp.dot(h, w_ref[:, 0:c1], preferred_element_type=F32)
    q = jnp.dot(h, w_ref[:, c1:c2], preferred_element_type=F32)
    k = jnp.dot(h, w_ref[:, c2:c3], preferred_element_type=F32)
    v_ref[...] = jnp.dot(h, w_ref[:, c3:c4], preferred_element_type=F32)
    ga_ref[...] = _sigmoid(jnp.dot(h, w_ref[:, c4:c5], preferred_element_type=F32)).astype(BF16)
    gb_ref[...] = _sigmoid(jnp.dot(h, w_ref[:, c5:c6], preferred_element_type=F32)).astype(BF16)
    inv_hd = 1.0 / HEAD_DIM
    q = q * lax.rsqrt(_head_sum(q * q, bd_ref, 1) * inv_hd + NORM_EPS) * qn_ref[...]
    k = k * lax.rsqrt(_head_sum(k * k, bd_ref, 1) * inv_hd + NORM_EPS) * kn_ref[...]
    q_ref[...] = q * (1.0 / float(HEAD_DIM) ** 0.5)
    k_ref[...] = k
    z = jnp.dot(h, wf_ref[...], preferred_element_type=F32) + bf_ref[...]
    logf = jnp.minimum(z, 0.0) - jnp.log1p(jnp.exp(-jnp.abs(z)))
    logf = jnp.where(_iota(logf.shape, 1) < n_heads, logf, 0.0)
    row = _iota((tm, tm), 0)
    col = _iota((tm, tm), 1)
    keep = col <= row
    if seq_len < tm:
        keep = keep & (_div_pow2(row, seq_len) == _div_pow2(col, seq_len))
    tri = jnp.where(keep, 1.0, 0.0).astype(BF16)
    cum = _mm(tri, logf, NN, pa=1, pb=3)
    if seq_len > tm:
        tiles_per_seq = seq_len // tm

        @pl.when(i % tiles_per_seq == 0)
        def _():
            carry_ref[...] = jnp.zeros_like(carry_ref)

        cum = cum + carry_ref[...]
        carry_ref[...] = cum[tm - 1:tm, :]
    lf_ref[...] = logf[:, :n_heads]
    cum_ref[...] = cum[:, :n_heads]
    src = _iota((LANES, LANES), 0)
    dst = _iota((LANES, LANES), 1)
    sub = _mod_pow2(dst, BIAS_GROUP)
    hit = (_div_pow2(dst, BIAS_GROUP) == _div_pow2(src, PAIR)) & (sub < 3 * PAIR) & (src < n_heads) \
        & ((sub >= 3) == (_mod_pow2(src, PAIR) == 1))
    nck_ref[...] = _mm(cum, jnp.where(hit, -1.0, 0.0).astype(BF16), NN, pa=3, pb=1)
    _rwkv_prep(p_a, i, *prep_in, bd_ref, *prep_out, last_ref, d_a=d_a, seq_len=seq_len, tm=tm)
    if seq_len >= tm:
        sh_ref[...] = p_a[tm - 1:tm, :]
    else:
        n_seq = tm // seq_len
        last = _iota((n_seq, tm), 1) == _iota((n_seq, tm), 0) * seq_len + (seq_len - 1)
        sh_ref[...] = _mm(jnp.where(last, 1.0, 0.0).astype(BF16), p_a, NN, pa=1, pb=3)


def _mixproj(x, g, w_main, w_f, b_f, qn, kn, bd, init_rows, mu, w0, a0, lora, g2, k_k, k_a,
             *, n_shift, d_a, d_b, n_heads, seq_len):
    n, d = x.shape
    tm = min(_row_tile(n), 256)
    assert seq_len % tm == 0 or tm % seq_len == 0
    row = lambda w: pl.BlockSpec((tm, w), lambda i: (i, 0))
    shp = lambda w: jax.ShapeDtypeStruct((n, w), F32)
    n_seq = n // seq_len
    if seq_len >= tm:
        tps = seq_len // tm
        init_spec = pl.BlockSpec((None, 1, n_shift), lambda i: (i // tps, 0, 0))
        init_rows = init_rows.reshape(n_seq, 1, n_shift)
        sh_spec = pl.BlockSpec((None, 1, n_shift), lambda i: (i // tps, 0, 0))
        sh_shape = jax.ShapeDtypeStruct((n_seq, 1, n_shift), F32)
    else:
        init_spec = row(n_shift)
        sh_spec = pl.BlockSpec((tm // seq_len, n_shift), lambda i: (i, 0))
        sh_shape = jax.ShapeDtypeStruct((n_seq, n_shift), F32)
    kern = functools.partial(_mixproj_kernel, n_shift=n_shift, d_a=d_a, d_b=d_b, d_model=d, n_heads=n_heads,
                             seq_len=seq_len, tm=tm)
    outs = pl.pallas_call(
        kern,
        grid=(n // tm,),
        in_specs=[row(d), _resident((1, d)), _resident(w_main.shape), _resident(w_f.shape),
                  _resident((1, LANES)), _resident((1, d_b)), _resident((1, d_b)), _resident(bd.shape),
                  init_spec, _resident((1, n_shift)), _resident((1, d_a)), _resident((1, d_a)),
                  _resident(lora.shape), _resident(g2.shape), _resident((1, d_a)), _resident((1, d_a))],
        out_specs=[row(d_b), row(d_b), row(d_b), row(d), row(d), row(n_heads), row(n_heads), row(LANES)]
                  + [row(d_a)] * 7 + [sh_spec],
        out_shape=[shp(d_b), shp(d_b), shp(d_b), jax.ShapeDtypeStruct((n, d), BF16), jax.ShapeDtypeStruct((n, d), BF16),
                   shp(n_heads), shp(n_heads), shp(LANES)]
                  + [shp(d_a)] * 7 + [sh_shape],
        scratch_shapes=[pltpu.VMEM((1, LANES), F32), pltpu.VMEM((1, n_shift), F32)],
        compiler_params=_params(("arbitrary",)),
        name="mixproj",
    )(x, g, w_main, w_f, b_f, qn, kn, bd, init_rows, mu, w0, a0, lora, g2, k_k, k_a)
    return tuple(outs[:-1]) + (outs[-1].reshape(n_seq, n_shift),)


def _stack(x, m0):
    return jnp.concatenate([jnp.where(m0, x, 0.0), jnp.where(m0, 0.0, x)], axis=0)


def _rscan_kernel(r_ref, k_ref, v_ref, kap_ref, b_ref, lw_ref, s0_ref, y_ref, st_ref, s_scr, *, bb, npair, c, hs, hc, hy):
    c2 = PAIR * c
    ci_ = pl.program_id(1)
    probs = [(i, j) for i in range(bb) for j in range(npair)]
    each = lambda f, *ls: [f(*xs) for xs in zip(*ls)]

    @pl.when(ci_ == 0)
    def _():
        zero = jnp.zeros((HEAD_DIM, HEAD_DIM), F32)
        for p, (i, j) in enumerate(probs):
            top = jnp.concatenate([s0_ref[i, PAIR * j], zero], axis=1)
            bot = jnp.concatenate([zero, s0_ref[i, PAIR * j + 1]], axis=1)
            s_scr[p] = jnp.concatenate([top, bot], axis=0)

    m0 = _iota((c, LANES), 1) < HEAD_DIM
    ri = _iota((c2, c2), 0)
    ci = _iota((c2, c2), 1)
    same = _div_pow2(ri, c) == _div_pow2(ci, c)
    strict = same & (ci < ri)
    incl = same & (ci <= ri)
    eye = ri == ci
    tri_c = jnp.where(_iota((c, c), 1) <= _iota((c, c), 0), 1.0, 0.0).astype(BF16)
    ld = lambda ref: [ref[i, :, j * LANES:(j + 1) * LANES] for (i, j) in probs]

    lw = ld(lw_ref)
    cum = each(lambda x: _mm(tri_c, x, NN, pa=1, pb=3), lw)
    cum_end = each(lambda x: x[c - 1:c, :], cum)
    e_neg = each(lambda x: jnp.exp(-x), cum)
    e_rem = each(lambda x, e: jnp.exp(e - x), cum, cum_end)
    kap, b, k = ld(kap_ref), ld(b_ref), ld(k_ref)
    a_t = each(lambda kp, x, w: _stack(-kp * jnp.exp(x - w), m0), kap, cum, lw)
    r_t = each(lambda r, x: _stack(r * jnp.exp(x), m0), ld(r_ref), cum)
    b_t = each(lambda x, e: _stack(x * e, m0), b, e_neg)
    k_t = each(lambda x, e: _stack(x * e, m0), k, e_neg)
    b_p = each(lambda x, e: _stack(x * e, m0), b, e_rem)
    k_p = each(lambda x, e: _stack(x * e, m0), k, e_rem)
    v_s = each(lambda x: _stack(x, m0), ld(v_ref))

    side = lambda x, y: jnp.concatenate([x, y], axis=1)
    bk_t = each(lambda x, y: jnp.concatenate([x, y], axis=0), b_t, k_t)
    sc_a = each(lambda x, y: _mm(x, y, NT, hc, hc), a_t, bk_t)
    sc_r = each(lambda x, y: _mm(x, y, NT, hy, hy), r_t, bk_t)
    l_ab = each(lambda x: jnp.where(strict, x[:, 0:c2], 0.0), sc_a)
    a_ak = each(lambda x: jnp.where(strict, x[:, c2:], 0.0), sc_a)
    a_rb = each(lambda x: jnp.where(incl, x[:, 0:c2], 0.0), sc_r)
    a_rk = each(lambda x: jnp.where(incl, x[:, c2:], 0.0), sc_r)

    t_inv = each(lambda x: jnp.where(eye, 1.0, 0.0) + x, l_ab)
    l_pow = each(lambda x: _mm(x, x, NN, hc, hc), l_ab)
    n_sq = c.bit_length() - 2
    for i in range(1, n_sq + 1):
        if i < n_sq:
            both = each(lambda x, t: _mm(jnp.concatenate([x, t], axis=0), x, NN, hc, hc), l_pow, t_inv)
            l_pow = each(lambda x: x[0:c2], both)
            t_inv = each(lambda t, x: t + x[c2:], t_inv, both)
        else:
            t_inv = each(lambda t, x: t + _mm(t, x, NN, hc, hc), t_inv, l_pow)

    akv = each(lambda x, y: _mm(x, y, NN, hc, hc), a_ak, v_s)
    au = each(lambda t, x, y: _mm(t, side(x, y), NN, hc, hc), t_inv, a_t, akv)
    a_p = each(lambda x: x[:, 0:LANES], au)
    u0 = each(lambda x: x[:, LANES:], au)
    ry = each(lambda x, y: _mm(x, y, NN, hy, hy), a_rb, au)
    r_p = each(lambda r, x: r + x[:, 0:LANES], r_t, ry)
    y0 = each(lambda x, y, z: _mm(x, y, NN, hy, hy) + z[:, LANES:], a_rk, v_s, ry)
    eye_k = _iota((LANES, LANES), 0) == _iota((LANES, LANES), 1)
    m_c = each(lambda e, x, y: jnp.where(eye_k, jnp.exp(e), 0.0) + _mm(x, y, TN, hc, hc), cum_end, a_p, b_p)
    n_c = each(lambda x, y, z, w: _mm(x, y, TN, hs, hs) + _mm(z, w, TN, hs, hs), u0, b_p, v_s, k_p)

    s_in = [s_scr[p] for p in range(len(probs))]
    y_st = each(lambda r, s, y: _mm(r, s, NT, hy, hy) + y, r_p, s_in, y0)
    s_out = each(lambda s, m, n: _mm(s, m, NN, hs, hs) + n, s_in, m_c, n_c)
    for p, (i, j) in enumerate(probs):
        y_ref[i, :, j * LANES:(j + 1) * LANES] = y_st[p][0:c, :] + y_st[p][c:c2, :]
        s_scr[p] = s_out[p]

    @pl.when(ci_ == pl.num_programs(1) - 1)
    def _():
        for p, (i, j) in enumerate(probs):
            st_ref[i, PAIR * j] = s_out[p][0:HEAD_DIM, 0:HEAD_DIM]
            st_ref[i, PAIR * j + 1] = s_out[p][HEAD_DIM:, HEAD_DIM:]


def _rscan(r, k, v, kap, b, lw, s0, *, hs=2, hc=1, hy=1):
    bsz, t, d_a = r.shape
    c = min(CHUNK, t)
    assert t % c == 0 and c % 8 == 0 and c & (c - 1) == 0
    npair = d_a // LANES
    bb = 4 if bsz % 4 == 0 else (2 if bsz % 2 == 0 else 1)
    seq = pl.BlockSpec((bb, c, d_a), lambda g, ci: (g, ci, 0))
    st = pl.BlockSpec((bb, PAIR * npair, HEAD_DIM, HEAD_DIM), lambda g, ci: (g, 0, 0, 0))
    kern = functools.partial(_rscan_kernel, bb=bb, npair=npair, c=c, hs=hs, hc=hc, hy=hy)
    return pl.pallas_call(
        kern,
        grid=(bsz // bb, t // c),
        in_specs=[seq] * 6 + [st],
        out_specs=[seq, st],
        out_shape=[jax.ShapeDtypeStruct((bsz, t, d_a), F32),
                   jax.ShapeDtypeStruct(s0.shape, F32)],
        scratch_shapes=[pltpu.VMEM((bb * npair, LANES, LANES), F32)],
        compiler_params=_params(("arbitrary", "arbitrary")),
        name="rscan",
    )(r, k, v, kap, b, lw, s0)


def _lanes(x, n):
    return x if n == LANES else jnp.concatenate([x] * (n // LANES), axis=1)


def _attn_kernel(q_ref, k_ref, v_ref, kb_ref, o_ref, kaug_scr, vb_scr, s_scr, p_scr, m_scr, acc_scr,
                 *, tq, strip, ns):
    qi = pl.program_id(2)
    tk = tq
    rows2 = 2 * tq
    streams = range(ns)
    lanes_of = lambda st: slice(st * LANES, (st + 1) * LANES)

    @pl.when(qi == 0)
    def _():
        p0, p1, p2 = _pieces(kb_ref[...], 3)
        which = _mod_pow2(_iota(p0.shape, 1), BIAS_GROUP)
        bias = jnp.where((which == 0) | (which == 3), p0, jnp.where((which == 1) | (which == 4), p1, p2))
        for st in streams:
            kaug_scr[st, :, 0:LANES] = k_ref[:, lanes_of(st)].astype(BF16)
            kaug_scr[st, :, LANES:2 * LANES] = bias
            vb_scr[st, :, 0:LANES] = v_ref[:, lanes_of(st)].astype(BF16)
            vb_scr[st, :, LANES:2 * LANES] = jnp.ones((vb_scr.shape[1], LANES), BF16)

    m0 = _iota((tq, LANES), 1) < HEAD_DIM
    head1 = _iota((rows2, LANES), 0) >= tq
    q_aug = []
    for st in streams:
        lane = _iota((rows2, LANES), 1) - BIAS_GROUP * (ns * pl.program_id(1) + st)
        pick = (lane >= jnp.where(head1, 3, 0)) & (lane < jnp.where(head1, 6, 3))
        q_aug.append(jnp.concatenate([_stack(q_ref[:, lanes_of(st)], m0), jnp.where(pick, 1.0, 0.0)],
                                     axis=1).astype(BF16))
    m_scr[...] = jnp.full_like(m_scr, NEG_BIG)
    acc_scr[...] = jnp.zeros_like(acc_scr)

    def keys(ki):
        return pl.ds(pl.multiple_of(ki * tk, tk), tk)

    def tile(ki, diagonal):
        for st in streams:
            s_scr[st] = lax.dot_general(q_aug[st], kaug_scr[st, keys(ki), :], NT, preferred_element_type=F32)
        for st in streams:
            for r0 in range(0, rows2, strip):
                rs = slice(r0, r0 + strip)
                s = s_scr[st, rs, :]
                if diagonal:
                    s = jnp.where(_iota((strip, tk), 1) <= _iota((strip, tk), 0) + r0 % tq, s, NEG_BIG)
                m_old = m_scr[st, rs, :]
                m_new = jnp.maximum(m_old, jnp.max(s, axis=-1, keepdims=True))
                alpha = jnp.exp(m_old - m_new)
                acc_scr[st, rs, :] = _lanes(alpha, 2 * LANES) * acc_scr[st, rs, :]
                p_scr[st, rs, :] = jnp.exp((s - _lanes(m_new, tk)).astype(BF16))
                m_scr[st, rs, :] = m_new
            acc_scr[st] += jnp.dot(p_scr[st], vb_scr[st, keys(ki), :], preferred_element_type=F32)

    def off_diagonal(ki, carry):
        tile(ki, False)
        return carry

    lax.fori_loop(0, qi, off_diagonal, 0)
    tile(qi, True)
    for st in streams:
        o = acc_scr[st, :, 0:LANES] / acc_scr[st, :, LANES:2 * LANES]
        o_ref[:, lanes_of(st)] = jnp.where(m0, o[0:tq, :], o[tq:rows2, :])


def _attn(q, k, v, kb, *, tq):
    bsz, t, d_b = q.shape
    npair = d_b // LANES
    ns = 2 if npair % 2 == 0 else 1
    nq = t // tq
    qspec = pl.BlockSpec((None, tq, ns * LANES), lambda b, j, qi: (b, qi, j))
    kspec = pl.BlockSpec((None, t, ns * LANES), lambda b, j, qi: (b, 0, j))
    kbspec = pl.BlockSpec((None, t, LANES), lambda b, j, qi: (b, 0, 0))
    return pl.pallas_call(
        functools.partial(_attn_kernel, tq=tq, strip=min(64, tq), ns=ns),
        grid=(bsz, npair // ns, nq),
        in_specs=[qspec, kspec, kspec, kbspec],
        out_specs=qspec,
        out_shape=jax.ShapeDtypeStruct((bsz, t, d_b), F32),
        scratch_shapes=[pltpu.VMEM((ns, t, 2 * LANES), BF16), pltpu.VMEM((ns, t, 2 * LANES), BF16),
                        pltpu.VMEM((ns, 2 * tq, tq), F32), pltpu.VMEM((ns, 2 * tq, tq), BF16),
                        pltpu.VMEM((ns, 2 * tq, LANES), F32), pltpu.VMEM((ns, 2 * tq, 2 * LANES), F32)],
        compiler_params=_params(("arbitrary",) * 3),
        name="attn",
    )(q, k, v, kb)


def _dattn_kernel(pt_ref, q_ref, cn_ref, kn_ref, vn_ref, bn_ref, kc_hbm, vc_hbm, lf_hbm, o_ref,
                  kbuf, vbuf, lbuf, sems, wq_scr, m_scr, l_scr, acc_scr, suf_scr,
                  *, n_heads, t_new, page, pg, n_pages):
    b = pl.program_id(0)
    step = pl.program_id(1)
    n_b = pl.num_programs(0)
    n_s = pl.num_programs(1)
    n_rows = n_heads * t_new
    slot = (b * n_s + step) & 1

    def page_copies(bi, si, sl):
        out = []
        for i in range(pg):
            src = pt_ref[bi, n_pages - 1 - (si * pg + i)]
            out.append(pltpu.make_async_copy(kc_hbm.at[src], kbuf.at[sl, i], sems.at[0, sl]))
            out.append(pltpu.make_async_copy(vc_hbm.at[src], vbuf.at[sl, i], sems.at[1, sl]))
            out.append(pltpu.make_async_copy(lf_hbm.at[src], lbuf.at[sl, i], sems.at[2, sl]))
        return out

    @pl.when((b == 0) & (step == 0))
    def _():
        for c in page_copies(0, 0, 0):
            c.start()

    wraps = step + 1 == n_s

    @pl.when(jnp.logical_not(wraps & (b + 1 == n_b)))
    def _():
        for c in page_copies(jnp.where(wraps, b + 1, b), jnp.where(wraps, 0, step + 1), 1 - slot):
            c.start()

    @pl.when(step == 0)
    def _():
        q = q_ref[...]
        head = _div_pow2(_iota(q.shape, 1), HEAD_DIM)
        wq_scr[...] = jnp.concatenate([jnp.where(head == h, q, 0.0) for h in range(n_heads)],
                                      axis=0).astype(BF16)
        m_scr[...] = jnp.full_like(m_scr, NEG_BIG)
        l_scr[...] = jnp.zeros_like(l_scr)
        acc_scr[...] = jnp.zeros_like(acc_scr)
        suf_scr[...] = jnp.zeros_like(suf_scr)

    for c in page_copies(b, step, slot):
        c.wait()

    def update(s, vals_t):
        n = len(vals_t)
        m_old = m_scr[...]
        m_new = jnp.maximum(m_old, jnp.max(s, axis=-1, keepdims=True))
        alpha = jnp.exp(m_old - m_new)
        p = jnp.exp(s - _lanes(m_new, n * page))
        l_scr[...] = alpha * l_scr[...] + jnp.sum(p, axis=-1, keepdims=True)
        p = p.astype(BF16)
        pv = None
        for i in range(n):
            t = lax.dot_general(p[:, i * page:(i + 1) * page], vals_t[i].astype(BF16), NT,
                                preferred_element_type=F32)
            pv = t if pv is None else pv + t
        acc_scr[...] = _lanes(alpha, acc_scr.shape[1]) * acc_scr[...] + pv
        m_scr[...] = m_new

    wq = wq_scr[...]
    lfs = [lbuf[slot, i] for i in range(pg)]
    lf_all = jnp.concatenate(lfs, axis=0)
    later = jnp.where(_iota((page, page), 0) > _iota((page, page), 1), 1.0, 0.0).astype(BF16)
    within = _mm(lf_all, later, NN, pa=3, pb=1)
    carry = suf_scr[...]
    cn = jnp.broadcast_to(cn_ref[...], (n_rows, page))
    parts = []
    for i in range(pg):
        suf = within[i * n_heads:(i + 1) * n_heads, :] + carry
        carry = carry + jnp.sum(lfs[i], axis=-1, keepdims=True)
        bias = jnp.broadcast_to(suf[:, None, :], (n_heads, t_new, page)).reshape(n_rows, page)
        parts.append(jnp.dot(wq, kbuf[slot, i].astype(BF16), preferred_element_type=F32) + bias + cn)
    suf_scr[...] = carry
    update(jnp.concatenate(parts, axis=1), [vbuf[slot, i] for i in range(pg)])

    @pl.when(step == n_s - 1)
    def _():
        s_new = jnp.dot(wq, kn_ref[...].astype(BF16), preferred_element_type=F32)
        update(s_new + bn_ref[...], [vn_ref[...]])
        o = acc_scr[...] / _lanes(l_scr[...], acc_scr.shape[1])
        head = _div_pow2(_iota((t_new, o.shape[1]), 1), HEAD_DIM)
        out = jnp.zeros((t_new, o.shape[1]), F32)
        for h in range(n_heads):
            out = out + jnp.where(head == h, o[h * t_new:(h + 1) * t_new, :], 0.0)
        o_ref[...] = out


def _dattn(page_table, q, cn_col, cache_kt, cache_vt, cache_lf_t, kt_new, vt_new, bias_new, *, n_heads, t_new, pg):
    bsz, n_pages = page_table.shape
    _, d_b, page = cache_kt.shape
    n_rows = n_heads * t_new
    assert n_pages % pg == 0
    per_b = lambda *blk: pl.BlockSpec((None,) + blk, lambda b, s, pt: (b,) + (0,) * len(blk))
    hbm = pl.BlockSpec(memory_space=pl.ANY)
    grid_spec = pltpu.PrefetchScalarGridSpec(
        num_scalar_prefetch=1,
        grid=(bsz, n_pages // pg),
        in_specs=[per_b(t_new, d_b), per_b(n_rows, 1), per_b(d_b, page), per_b(d_b, page), per_b(n_rows, page),
                  hbm, hbm, hbm],
        out_specs=per_b(t_new, d_b),
        scratch_shapes=[pltpu.VMEM((2, pg, d_b, page), F32), pltpu.VMEM((2, pg, d_b, page), F32),
                        pltpu.VMEM((2, pg, n_heads, page), F32), pltpu.SemaphoreType.DMA((3, 2)),
                        pltpu.VMEM((n_rows, d_b), BF16), pltpu.VMEM((n_rows, LANES), F32),
                        pltpu.VMEM((n_rows, LANES), F32), pltpu.VMEM((n_rows, d_b), F32),
                        pltpu.VMEM((n_heads, 1), F32)],
    )
    kern = functools.partial(_dattn_kernel, n_heads=n_heads, t_new=t_new, page=page, pg=pg, n_pages=n_pages)
    return pl.pallas_call(
        kern,
        grid_spec=grid_spec,
        out_shape=jax.ShapeDtypeStruct((bsz, t_new, d_b), F32),
        compiler_params=_params(("arbitrary", "arbitrary")),
        name="dattn",
    )(page_table, q, cn_col, kt_new, vt_new, bias_new, cache_kt, cache_vt, cache_lf_t)


def _merge_kernel(x_ref, y_ref, r_ref, k_ref, v_ref, g_ref, yb_ref, ga_ref, gb_ref,
                  lnw_ref, lnb_ref, rk_ref, bd_ref, woa_ref, wob_ref, wout_ref, o_ref):
    inv_hd = 1.0 / HEAD_DIM
    y = y_ref[...]
    mean = _head_sum(y, bd_ref, 1) * inv_hd
    yc = y - mean
    var = _head_sum(yc * yc, bd_ref, 1) * inv_hd
    yn = yc * lax.rsqrt(var + LNX_EPS) * lnw_ref[...] + lnb_ref[...]
    v = v_ref[...]
    bonus = _head_sum(r_ref[...] * k_ref[...] * rk_ref[...], bd_ref, 1) * v
    ya = ((yn + bonus) * g_ref[...]).astype(BF16)
    pa = jnp.dot(ya, woa_ref[...], preferred_element_type=F32)
    pb = jnp.dot(yb_ref[...].astype(BF16), wob_ref[...], preferred_element_type=F32)
    merged = ga_ref[...].astype(F32) * pa + gb_ref[...].astype(F32) * pb
    o_ref[...] = x_ref[...] + jnp.dot(merged.astype(BF16), wout_ref[...], preferred_element_type=F32)


def _merge(x, y, r, k, v, g, yb, ga, gb, lnw, lnb, rk, bd, woa, wob, wout):
    n, d = x.shape
    d_a = y.shape[1]
    tm = _row_tile(n)
    row = lambda w: pl.BlockSpec((tm, w), lambda i: (i, 0))
    return pl.pallas_call(
        _merge_kernel,
        grid=(n // tm,),
        in_specs=[row(d)] + [row(d_a)] * 6 + [row(d), row(d)] + [_resident((1, d_a))] * 3
                 + [_resident(bd.shape), _resident(woa.shape), _resident(wob.shape), _resident(wout.shape)],
        out_specs=row(d),
        out_shape=jax.ShapeDtypeStruct((n, d), F32),
        compiler_params=_params(("arbitrary",)),
        name="merge",
    )(x, y, r, k, v, g, yb, ga, gb, lnw, lnb, rk, bd, woa, wob, wout)


def _prep_weights(lw, n_heads_a, n_heads_b):
    (ffn1_norm, ffn1_wg, ffn1_wu, ffn1_wd, mix_norm, w_in, shift_mu, w0, w2, a0, a2, g2, k_k, k_a, r_k,
     lnx_w, lnx_b, b_f, q_norm, k_norm, w_o_a, w_o_b, w_out, ffn2_norm, ffn2_wg, ffn2_wu, ffn2_wd) = lw
    d = w_in.shape[0]
    d_a = n_heads_a * HEAD_DIM
    d_b = n_heads_b * HEAD_DIM
    n_shift = shift_mu.shape[0]
    c4 = n_shift + 3 * d_b
    row = lambda t: t.reshape(1, -1)
    w_main = jnp.concatenate([w_in[:, :c4], w_in[:, c4 + n_heads_b:]], axis=1).astype(BF16)
    w_f = jnp.pad(w_in[:, c4:c4 + n_heads_b], ((0, 0), (0, LANES - n_heads_b))).astype(BF16)
    dl, al = w2.shape[0], a2.shape[0]
    assert dl == HEAD_DIM and al == HEAD_DIM and g2.shape[0] == LANES
    lora = jnp.zeros((LANES, 2 * d_a), F32).at[:dl, :d_a].set(w2).at[dl:, d_a:].set(a2)
    bd = jnp.kron(jnp.eye(max(d_a, d_b) // HEAD_DIM, dtype=F32), jnp.ones((HEAD_DIM, HEAD_DIM), F32)).astype(BF16)
    return dict(
        d=d, d_a=d_a, d_b=d_b, n_shift=n_shift, n_heads_a=n_heads_a, n_heads_b=n_heads_b,
        ffn1=(row(ffn1_norm), ffn1_wg.astype(BF16), ffn1_wu.astype(BF16), ffn1_wd.astype(BF16)),
        ffn2=(row(ffn2_norm), ffn2_wg.astype(BF16), ffn2_wu.astype(BF16), ffn2_wd.astype(BF16)),
        mix_norm=row(mix_norm), w_main=w_main, w_f=w_f,
        b_f=jnp.pad(b_f, (0, LANES - n_heads_b)).reshape(1, LANES),
        q_norm=row(jnp.tile(q_norm, n_heads_b)), k_norm=row(jnp.tile(k_norm, n_heads_b)), bd=bd,
        mu=row(shift_mu), w0=row(w0), a0=row(a0), lora=lora, g2=g2, k_k=row(k_k), k_a=row(k_a),
        r_k=row(r_k), lnx_w=row(lnx_w), lnx_b=row(lnx_b),
        w_o_a=w_o_a.astype(BF16), w_o_b=w_o_b.astype(BF16), w_out=w_out.astype(BF16),
    )


def _layer(x3, prev_shift, s0, past, w):
    bsz, t, d = x3.shape
    d_a, d_b, ha, hb = w["d_a"], w["d_b"], w["n_heads_a"], w["n_heads_b"]
    n = bsz * t
    x = x3.reshape(n, d)
    x1 = _ffn(x, *w["ffn1"])
    init_rows = prev_shift if t >= min(_row_tile(n), 256) else jnp.repeat(prev_shift, t, axis=0)
    q, k, v, g_a, g_b, logf, cum, nck, r, km, va, kap, b, lw, g, new_shift = _mixproj(
        x1, w["mix_norm"], w["w_main"], w["w_f"], w["b_f"], w["q_norm"], w["k_norm"], w["bd"],
        init_rows, w["mu"], w["w0"], w["a0"], w["lora"], w["g2"], w["k_k"], w["k_a"],
        n_shift=w["n_shift"], d_a=d_a, d_b=d_b, n_heads=hb, seq_len=t)

    seq = lambda a: a.reshape(bsz, t, d_a)
    y_raw, new_s = _rscan(seq(r), seq(km), seq(va), seq(kap), seq(b), seq(lw), s0)
    y_raw = y_raw.reshape(n, d_a)

    npair = d_b // LANES
    if past is None:
        tq = min(512, t)
        y_b = _attn(q.reshape(bsz, t, d_b), k.reshape(bsz, t, d_b), v.reshape(bsz, t, d_b),
                    nck.reshape(bsz, t, LANES), tq=tq)
    else:
        cache_k, cache_v, cache_lf, page_table = past
        n_pool, page = cache_k.shape[:2]
        cn = cum.reshape(bsz, t, hb)
        cn_col = cn.transpose(0, 2, 1).reshape(bsz, hb * t, 1)
        tpos = jnp.arange(t)
        ok = tpos[None, :] <= tpos[:, None]
        bn = cn.transpose(0, 2, 1)[:, :, :, None] - cn.transpose(0, 2, 1)[:, :, None, :]
        bn = jnp.where(ok[None, None], bn, NEG_BIG).reshape(bsz, hb * t, t)
        bn = jnp.pad(bn, ((0, 0), (0, 0), (0, page - t)), constant_values=NEG_BIG)
        pages_t = lambda c: c.transpose(0, 2, 3, 1).reshape(n_pool, d_b, page)
        new_t = lambda a: jnp.pad(a.reshape(bsz, t, d_b).transpose(0, 2, 1), ((0, 0), (0, 0), (0, page - t)))
        pg = next(g for g in (32, 16, 8, 4, 2, 1) if page_table.shape[1] % g == 0)
        y_b = _dattn(page_table, q.reshape(bsz, t, d_b), cn_col, pages_t(cache_k), pages_t(cache_v),
                     cache_lf.transpose(0, 2, 1), new_t(k), new_t(v), bn, n_heads=hb, t_new=t, pg=pg)
    y_b = y_b.reshape(n, d_b)

    x2 = _merge(x1, y_raw, r, km, va, g, y_b, g_a, g_b, w["lnx_w"], w["lnx_b"], w["r_k"], w["bd"],
                w["w_o_a"], w["w_o_b"], w["w_out"])
    x3o = _ffn(x2, *w["ffn2"]).reshape(bsz, t, d)
    return x3o, (k.reshape(bsz, t, hb, HEAD_DIM), v.reshape(bsz, t, hb, HEAD_DIM),
                 logf.reshape(bsz, t, hb), new_s, new_shift)


def kernel(x_prompt, x_sample, cache_k, cache_v, cache_logf, state_wkv, state_shift, page_table,
           ffn1_norm, ffn1_wg, ffn1_wu, ffn1_wd, mix_norm, w_in, shift_mu,
           rwkv_w0, rwkv_w2, rwkv_a0, rwkv_a2, rwkv_g2, rwkv_k_k, rwkv_k_a, rwkv_r_k, rwkv_lnx_w, rwkv_lnx_b,
           fox_b_f, fox_q_norm, fox_k_norm, w_o_a, w_o_b, w_out,
           ffn2_norm, ffn2_wg, ffn2_wu, ffn2_wd):
    depth = w_in.shape[0]
    n_heads_a = state_wkv.shape[2]
    n_heads_b = cache_k.shape[3]
    bp = x_prompt.shape[0]
    xp, xs = x_prompt, x_sample
    outs_p, outs_s = [], []
    for l in range(depth):
        lw = (ffn1_norm[l], ffn1_wg[l], ffn1_wu[l], ffn1_wd[l], mix_norm[l], w_in[l], shift_mu[l],
              rwkv_w0[l], rwkv_w2[l], rwkv_a0[l], rwkv_a2[l], rwkv_g2[l], rwkv_k_k[l], rwkv_k_a[l],
              rwkv_r_k[l].reshape(-1), rwkv_lnx_w[l], rwkv_lnx_b[l], fox_b_f[l], fox_q_norm[l], fox_k_norm[l],
              w_o_a[l], w_o_b[l], w_out[l], ffn2_norm[l], ffn2_wg[l], ffn2_wu[l], ffn2_wd[l])
        w = _prep_weights(lw, n_heads_a, n_heads_b)
        prev0 = jnp.zeros((bp, w["n_shift"]), xp.dtype)
        s00 = jnp.zeros((bp, n_heads_a, HEAD_DIM, HEAD_DIM), xp.dtype)
        xp, op = _layer(xp, prev0, s00, None, w)
        xs, os_ = _layer(xs, state_shift[l], state_wkv[l], (cache_k[l], cache_v[l], cache_logf[l], page_table), w)
        outs_p.append(op)
        outs_s.append(os_)
    stack = lambda outs, i: jnp.stack([o[i] for o in outs], 0)
    return ((xp, xs) + tuple(stack(outs_p, i) for i in range(5)) + tuple(stack(outs_s, i) for i in range(5)))
```
